```python
import jax, jax.numpy as jnp
from jax import lax
import numpy as np

D_MODEL = 1024
BATCH = 2
SEQ = 8192
DEPTH = 4

CHUNK = 64
Q_BLOCK = 128
HEAD_DIM = 64
NORM_EPS = 1e-6

SB_HEADS = D_MODEL // (2 * HEAD_DIM)
SB_WIDTH = SB_HEADS * HEAD_DIM
POOL_WINDOWS = (2, 4, 8, 16)
N_POOL = len(POOL_WINDOWS)
POOL_WIDTH = D_MODEL - SB_WIDTH
POOL_GROUP = POOL_WIDTH // N_POOL
EVEN_IN = 3 * SB_WIDTH + POOL_WIDTH

FOX_HEADS = D_MODEL // (2 * HEAD_DIM)
FOX_WIDTH = FOX_HEADS * HEAD_DIM
FOX_IN = 4 * FOX_WIDTH + FOX_HEADS
RWKV_HEADS = D_MODEL // (2 * HEAD_DIM)
RWKV_WIDTH = RWKV_HEADS * HEAD_DIM
DECAY_LORA = 64
ICLR_LORA = 64
GATE_LORA = 128
RWKV_IN = 3 * RWKV_WIDTH + DECAY_LORA + ICLR_LORA + GATE_LORA
ODD_IN = FOX_IN + RWKV_IN
RWKV_GN_EPS = 64e-5

PEER_HEADS = 8
PEER_NKEYS = 128
PEER_EXPERTS = PEER_NKEYS * PEER_NKEYS
PEER_TOPK = 16
PEER_DKEY = 256
PEER_TOKEN_BLOCK = 128

N_EVEN = (DEPTH + 1) // 2
N_ODD = DEPTH // 2

kernel_name = 'hybrid_sb_pool_fox_rwkv7_peer_encoder'


def rms_norm(x, gain, eps=NORM_EPS):
    xf = x.astype(jnp.float32)
    y = xf * lax.rsqrt(jnp.mean(xf * xf, axis=-1, keepdims=True) + eps)
    return (y * gain.astype(jnp.float32)).astype(x.dtype)


def modulate(h, shift, scale):
    return h * (1.0 + scale[:, None, :]) + shift[:, None, :]


def split_heads(t, n_heads):
    b, s, _ = t.shape
    return t.reshape(b, s, n_heads, -1).transpose(0, 2, 1, 3)


def merge_heads(t):
    b, h, s, d = t.shape
    return t.transpose(0, 2, 1, 3).reshape(b, s, h * d)


def stick_breaking_attention(q, k, v):
    b, h, s, dh = q.shape
    nb = s // Q_BLOCK
    scale = dh ** -0.5
    qb = q.reshape(b, h, nb, Q_BLOCK, dh).transpose(2, 0, 1, 3, 4)
    key_pos = jnp.arange(s)

    def block(args):
        qi, i = args
        q_pos = i * Q_BLOCK + jnp.arange(Q_BLOCK)
        strict = key_pos[None, :] < q_pos[:, None]
        z = jnp.einsum('bhqd,bhkd->bhqk', qi, k).astype(jnp.float32) * scale
        log_beta = jax.nn.log_sigmoid(z)
        log_1m = jnp.where(strict, jax.nn.log_sigmoid(-z), 0.0)
        stick = lax.cumsum(log_1m, axis=3, reverse=True) - log_1m
        w = jnp.where(strict, jnp.exp(log_beta + stick), 0.0)
        return jnp.einsum('bhqk,bhkd->bhqd', w.astype(v.dtype), v)

    out = lax.map(block, (qb, jnp.arange(nb)))
    return out.transpose(1, 2, 0, 3, 4).reshape(b, h, s, dh)


def forgetting_attention(q, k, v, log_f):
    b, h, s, dh = q.shape
    nb = s // Q_BLOCK
    scale = dh ** -0.5
    dcum = jnp.cumsum(log_f.astype(jnp.float32), axis=-1)
    qb = q.reshape(b, h, nb, Q_BLOCK, dh).transpose(2, 0, 1, 3, 4)
    db = dcum.reshape(b, h, nb, Q_BLOCK).transpose(2, 0, 1, 3)
    key_pos = jnp.arange(s)

    def block(args):
        qi, di, i = args
        q_pos = i * Q_BLOCK + jnp.arange(Q_BLOCK)
        logits = (jnp.einsum('bhqd,bhkd->bhqk', qi, k).astype(jnp.float32) * scale
                  + di[..., :, None] - dcum[:, :, None, :])
        logits = jnp.where(key_pos[None, :] <= q_pos[:, None], logits, -jnp.inf)
        p = jax.nn.softmax(logits, axis=-1)
        return jnp.einsum('bhqk,bhkd->bhqd', p.astype(v.dtype), v)

    out = lax.map(block, (qb, db, jnp.arange(nb)))
    return out.transpose(1, 2, 0, 3, 4).reshape(b, h, s, dh)


def multiscale_pool(xp, pool_w, pool_scale):
    b, s, _ = xp.shape
    t = jnp.arange(s)
    chunk_end = jnp.minimum((t // CHUNK + 1) * CHUNK, s)
    xg = xp.reshape(b, s, N_POOL, POOL_GROUP).astype(jnp.float32)
    cs = jnp.pad(jnp.cumsum(xg, axis=1), ((0, 0), (1, 0), (0, 0), (0, 0)))
    outs = []
    for g, w in enumerate(POOL_WINDOWS):
        lo = jnp.maximum(t - w // 2, 0)
        hi = jnp.minimum(t + (w - w // 2), chunk_end)
        window_sum = jnp.take(cs[:, :, g], hi, axis=1) - jnp.take(cs[:, :, g], lo, axis=1)
        count = (hi - lo).astype(jnp.float32)[None, :, None]
        outs.append(window_sum / count - xg[:, :, g])
    pooled = jnp.stack(outs, axis=2)
    y = jnp.einsum('bsgc,gcd->bsgd', pooled, pool_w) * pool_scale.reshape(N_POOL, POOL_GROUP)
    return y.reshape(b, s, POOL_WIDTH)


def rwkv7_recurrence(r, w, k, v, kk, a):
    b, s, h, n = r.shape
    xs = tuple(jnp.moveaxis(t.astype(jnp.float32), 1, 0) for t in (r, w, k, v, kk, a))

    def step(state, inp):
        r_t, w_t, k_t, v_t, kk_t, a_t = inp
        sa = jnp.einsum('bhvk,bhk->bhv', state, kk_t)
        state = (state * w_t[:, :, None, :]
                 - sa[..., None] * (kk_t * a_t)[:, :, None, :]
                 + v_t[..., None] * k_t[:, :, None, :])
        y = jnp.einsum('bhvk,bhk->bhv', state, r_t)
        return state, y

    state0 = jnp.zeros((b, h, n, n), jnp.float32)
    _, ys = lax.scan(step, state0, xs)
    return jnp.moveaxis(ys, 0, 1)


def even_mixer(h, w_in, w_out, pool_w, pool_scale):
    proj = h @ w_in
    q = split_heads(proj[..., :SB_WIDTH], SB_HEADS)
    k = split_heads(proj[..., SB_WIDTH:2 * SB_WIDTH], SB_HEADS)
    v = split_heads(proj[..., 2 * SB_WIDTH:3 * SB_WIDTH], SB_HEADS)
    sb = merge_heads(stick_breaking_attention(q, k, v))
    pool = multiscale_pool(proj[..., 3 * SB_WIDTH:], pool_w, pool_scale)
    return jnp.concatenate([sb, pool.astype(sb.dtype)], axis=-1) @ w_out


def odd_mixer(h, w_in, w_out, fox_qnorm, fox_knorm, fox_fbias, rwkv_mix, rwkv_w0, rwkv_w2,
              rwkv_a0, rwkv_a2, rwkv_g2, rwkv_kk, rwkv_ka, rwkv_rk, rwkv_ln_w, rwkv_ln_b):
    b, s, _ = h.shape
    proj = h @ w_in
    pc, pd = proj[..., :FOX_IN], proj[..., FOX_IN:]

    q = rms_norm(split_heads(pc[..., :FOX_WIDTH], FOX_HEADS), fox_qnorm)
    k = rms_norm(split_heads(pc[..., FOX_WIDTH:2 * FOX_WIDTH], FOX_HEADS), fox_knorm)
    v = split_heads(pc[..., 2 * FOX_WIDTH:3 * FOX_WIDTH], FOX_HEADS)
    out_gate = pc[..., 3 * FOX_WIDTH:4 * FOX_WIDTH]
    log_f = jax.nn.log_sigmoid((pc[..., 4 * FOX_WIDTH:] + fox_fbias).astype(jnp.float32))
    att = forgetting_attention(q, k, v, log_f.transpose(0, 2, 1))
    fox_out = merge_heads(att) * jax.nn.sigmoid(out_gate)

    prev = jnp.pad(pd, ((0, 0), (1, 0), (0, 0)))[:, :s]
    pd = pd + (prev - pd) * rwkv_mix
    sizes = (RWKV_WIDTH, RWKV_WIDTH, RWKV_WIDTH, DECAY_LORA, ICLR_LORA, GATE_LORA)
    offs = [0]
    for n in sizes:
        offs.append(offs[-1] + n)
    r, kr, vr, xw, xa, xg = [pd[..., offs[i]:offs[i + 1]] for i in range(6)]
    w_log = -jax.nn.softplus(-(rwkv_w0 + jnp.tanh(xw) @ rwkv_w2)) - 0.5
    decay = jnp.exp(-jnp.exp(w_log.astype(jnp.float32)))
    a = jax.nn.sigmoid(rwkv_a0 + xa @ rwkv_a2)
    g = jax.nn.sigmoid(xg) @ rwkv_g2
    kk = (kr * rwkv_kk).reshape(b, s, RWKV_HEADS, HEAD_DIM).astype(jnp.float32)
    kk = kk / jnp.maximum(jnp.sqrt(jnp.sum(kk * kk, axis=-1, keepdims=True)), 1e-12)
    kr = kr * (1.0 + (a - 1.0) * rwkv_ka)
    hs = (b, s, RWKV_HEADS, HEAD_DIM)
    rh, kh, vh = r.reshape(hs), kr.reshape(hs), vr.reshape(hs)
    y = rwkv7_recurrence(rh, decay.reshape(hs), kh, vh, kk, a.reshape(hs))
    mu = jnp.mean(y, axis=-1, keepdims=True)
    var = jnp.mean(jnp.square(y - mu), axis=-1, keepdims=True)
    y = ((y - mu) * lax.rsqrt(var + RWKV_GN_EPS)).reshape(b, s, RWKV_WIDTH) * rwkv_ln_w + rwkv_ln_b
    bonus = jnp.sum(rh * kh * rwkv_rk, axis=-1, keepdims=True) * vh
    rwkv_out = (y + bonus.reshape(b, s, RWKV_WIDTH)) * g

    return jnp.concatenate([fox_out, rwkv_out.astype(fox_out.dtype)], axis=-1) @ w_out


def peer_ffn(h, w_query, sub_k1, sub_k2, expert_u, expert_v):
    b, s, d = h.shape
    n_tok = b * s
    xt = h.reshape(n_tok, d)
    q = (xt @ w_query).reshape(n_tok, PEER_HEADS, 2, PEER_DKEY // 2)
    s1 = jnp.einsum('thd,nd->thn', q[:, :, 0], sub_k1)
    s2 = jnp.einsum('thd,nd->thn', q[:, :, 1], sub_k2)
    v1, i1 = lax.top_k(s1, PEER_TOPK)
    v2, i2 = lax.top_k(s2, PEER_TOPK)
    cand = (v1[..., :, None] + v2[..., None, :]).reshape(n_tok, PEER_HEADS, PEER_TOPK * PEER_TOPK)
    cand_idx = (i1[..., :, None] * PEER_NKEYS + i2[..., None, :]).reshape(n_tok, PEER_HEADS, PEER_TOPK * PEER_TOPK)
    top_s, pos = lax.top_k(cand.astype(jnp.float32), PEER_TOPK)
    idx = jnp.take_along_axis(cand_idx, pos, axis=-1)
    gate = jax.nn.softmax(top_s, axis=-1)
    nb = n_tok // PEER_TOKEN_BLOCK

    def block(args):
        xb, ib, gb = args
        u = expert_u[ib]
        act = jax.nn.gelu(jnp.einsum('thkd,td->thk', u, xb), approximate=False)
        ve = expert_v[ib]
        return jnp.einsum('thk,thkd->td', (gb * act).astype(ve.dtype), ve)

    out = lax.map(block, (xt.reshape(nb, PEER_TOKEN_BLOCK, d),
                          idx.reshape(nb, PEER_TOKEN_BLOCK, PEER_HEADS, PEER_TOPK),
                          gate.reshape(nb, PEER_TOKEN_BLOCK, PEER_HEADS, PEER_TOPK)))
    return out.reshape(b, s, d)


def setup_inputs(seed: int = 0) -> dict:
    key = jax.random.key(seed)
    ks = iter(jax.random.split(key, 40))

    def nrm(shape, scale):
        return jax.random.normal(next(ks), shape, jnp.float32) * scale

    def gain(shape):
        return 1.0 + nrm(shape, 0.02)

    W = RWKV_WIDTH
    return {
        'x': nrm((BATCH, SEQ, D_MODEL), 1.0),
        'c': nrm((BATCH, D_MODEL), 1.0),
        'ada_w': nrm((DEPTH, D_MODEL, 6 * D_MODEL), 0.5 * D_MODEL ** -0.5),
        'ada_b': nrm((DEPTH, 6 * D_MODEL), 0.01),
        'norm_mix': gain((DEPTH, D_MODEL)),
        'norm_ffn': gain((DEPTH, D_MODEL)),
        'ev_w_in': nrm((N_EVEN, D_MODEL, EVEN_IN), D_MODEL ** -0.5),
        'ev_w_out': nrm((N_EVEN, D_MODEL, D_MODEL), D_MODEL ** -0.5),
        'pool_w': nrm((N_EVEN, N_POOL, POOL_GROUP, POOL_GROUP), POOL_GROUP ** -0.5),
        'pool_scale': gain((N_EVEN, POOL_WIDTH)),
        'od_w_in': nrm((N_ODD, D_MODEL, ODD_IN), D_MODEL ** -0.5),
        'od_w_out': nrm((N_ODD, D_MODEL, D_MODEL), D_MODEL ** -0.5),
        'fox_qnorm': gain((N_ODD, HEAD_DIM)),
        'fox_knorm': gain((N_ODD, HEAD_DIM)),
        'fox_fbias': jnp.linspace(1.0, 6.0, FOX_HEADS)[None, :] + nrm((N_ODD, FOX_HEADS), 0.1),
        'rwkv_mix': jax.random.uniform(next(ks), (N_ODD, RWKV_IN), jnp.float32),
        'rwkv_w0': jax.random.uniform(next(ks), (N_ODD, W), jnp.float32, minval=-6.0, maxval=1.0),
        'rwkv_w2': nrm((N_ODD, DECAY_LORA, W), 0.1 * DECAY_LORA ** -0.5),
        'rwkv_a0': nrm((N_ODD, W), 0.1),
        'rwkv_a2': nrm((N_ODD, ICLR_LORA, W), 0.5 * ICLR_LORA ** -0.5),
        'rwkv_g2': nrm((N_ODD, GATE_LORA, W), GATE_LORA ** -0.5),
        'rwkv_kk': 0.85 + nrm((N_ODD, W), 0.02),
        'rwkv_ka': gain((N_ODD, W)),
        'rwkv_rk': nrm((N_ODD, RWKV_HEADS, HEAD_DIM), 0.1),
        'rwkv_ln_w': gain((N_ODD, W)),
        'rwkv_ln_b': nrm((N_ODD, W), 0.01),
        'peer_wq': nrm((DEPTH, D_MODEL, PEER_HEADS * PEER_DKEY), D_MODEL ** -0.5),
        'peer_k1': nrm((DEPTH, PEER_NKEYS, PEER_DKEY // 2), (PEER_DKEY // 2) ** -0.5),
        'peer_k2': nrm((DEPTH, PEER_NKEYS, PEER_DKEY // 2), (PEER_DKEY // 2) ** -0.5),
        'peer_u': nrm((DEPTH, PEER_EXPERTS, D_MODEL), D_MODEL ** -0.5),
        'peer_v': nrm((DEPTH, PEER_EXPERTS, D_MODEL), PEER_HEADS ** -0.5),
        'final_norm': gain((D_MODEL,)),
    }


def reference(x, c, ada_w, ada_b, norm_mix, norm_ffn, ev_w_in, ev_w_out, pool_w, pool_scale,
              od_w_in, od_w_out, fox_qnorm, fox_knorm, fox_fbias, rwkv_mix, rwkv_w0, rwkv_w2,
              rwkv_a0, rwkv_a2, rwkv_g2, rwkv_kk, rwkv_ka, rwkv_rk, rwkv_ln_w, rwkv_ln_b,
              peer_wq, peer_k1, peer_k2, peer_u, peer_v, final_norm):
    c_act = jax.nn.silu(c)
    for layer in range(DEPTH):
        mod = c_act @ ada_w[layer] + ada_b[layer]
        sh1, sc1, g1, sh2, sc2, g2 = jnp.split(mod, 6, axis=-1)
        h = modulate(rms_norm(x, norm_mix[layer]), sh1, sc1)
        j = layer // 2
        if layer % 2 == 0:
            y = even_mixer(h, ev_w_in[j], ev_w_out[j], pool_w[j], pool_scale[j])
        else:
            y = odd_mixer(h, od_w_in[j], od_w_out[j], fox_qnorm[j], fox_knorm[j], fox_fbias[j],
                          rwkv_mix[j], rwkv_w0[j], rwkv_w2[j], rwkv_a0[j], rwkv_a2[j], rwkv_g2[j],
                          rwkv_kk[j], rwkv_ka[j], rwkv_rk[j], rwkv_ln_w[j], rwkv_ln_b[j])
        x = x + g1[:, None, :] * y
        h = modulate(rms_norm(x, norm_ffn[layer]), sh2, sc2)
        x = x + g2[:, None, :] * peer_ffn(h, peer_wq[layer], peer_k1[layer], peer_k2[layer],
                                          peer_u[layer], peer_v[layer])
    return rms_norm(x, final_norm)
```

```python
import functools

import jax
import jax.numpy as jnp
from jax import lax
from jax.experimental import pallas as pl
from jax.experimental.pallas import tpu as pltpu

F32 = jnp.float32
BF16 = jnp.bfloat16
I32 = jnp.int32
HIGHEST = lax.Precision.HIGHEST

LANES = 128
SUBLANES = 8
VMEM_BYTES_V7X = 64 * 1024 * 1024

HEAD_DIM = 64
HEADS_PER_LANE_BLOCK = LANES // HEAD_DIM
NORM_EPS = 1e-6
RWKV_GN_EPS = 64e-5
CHUNK = 64
POOL_WINDOWS = (2, 4, 8, 16)
PEER_TOPK = 16
PEER_NKEYS = 128
PEER_HEADS = 8
PEER_SLOTS = PEER_HEADS * PEER_TOPK

TOKEN_TILE = 256
ATTN_TILE = 256
PEER_TOKENS = 128
TILE_PITCH = PEER_SLOTS + 8


def _compiler_params(semantics, vmem_mb=48):
    return pltpu.CompilerParams(dimension_semantics=semantics,
                                vmem_limit_bytes=min(vmem_mb * 1024 * 1024, VMEM_BYTES_V7X))


def _split_bf16(x, n):
    parts = []
    r = x
    for _ in range(n):
        p = r.astype(BF16)
        parts.append(p)
        r = r - p.astype(F32)
    return parts


def _dot_exact_rhs(x, m, n=3):
    acc = None
    for p in _split_bf16(x, n):
        t = jnp.dot(p, m, preferred_element_type=F32)
        acc = t if acc is None else acc + t
    return acc


def _dot_exact_lhs(m, x, n=3):
    acc = None
    for p in _split_bf16(x, n):
        t = jnp.dot(m, p, preferred_element_type=F32)
        acc = t if acc is None else acc + t
    return acc


def _log_sigmoid(z):
    return jnp.minimum(z, 0.0) - jnp.log1p(jnp.exp(-jnp.abs(z)))


def _sigmoid(z):
    return 1.0 / (1.0 + jnp.exp(-z))


def _head_sum_matrix(width):
    r = lax.broadcasted_iota(I32, (width, width), 0) // HEAD_DIM
    c = lax.broadcasted_iota(I32, (width, width), 1) // HEAD_DIM
    return (r == c).astype(BF16)


def _adaln_kernel(c_ref, w_ref, b_ref, o_ref):
    c = c_ref[...]
    c_act = c * _sigmoid(c)
    o_ref[0] = jnp.dot(c_act, w_ref[0], precision=HIGHEST, preferred_element_type=F32) + b_ref[0]


def _adaln(c, ada_w, ada_b):
    b, d = c.shape
    depth, _, n = ada_w.shape
    tn = 1536
    assert n % tn == 0 and b <= SUBLANES
    c_pad = jnp.zeros((SUBLANES, d), F32).at[:b].set(c)
    out = pl.pallas_call(
        _adaln_kernel,
        grid=(depth, n // tn),
        in_specs=[pl.BlockSpec((SUBLANES, d), lambda l, j: (0, 0)),
                  pl.BlockSpec((1, d, tn), lambda l, j: (l, 0, j)),
                  pl.BlockSpec((1, 1, tn), lambda l, j: (l, 0, j))],
        out_specs=pl.BlockSpec((1, SUBLANES, tn), lambda l, j: (l, 0, j)),
        out_shape=jax.ShapeDtypeStruct((depth, SUBLANES, n), F32),
        compiler_params=_compiler_params(("arbitrary", "arbitrary")),
        name="adaln",
    )(c_pad, ada_w, ada_b.reshape(depth, 1, n))
    return out[:, :b, :]


def _norm_proj_kernel(x_ref, gain_ref, shift_ref, scale_ref, w_ref, *out_refs, groups, emit_h):
    x = x_ref[...]
    ms = jnp.mean(x * x, axis=-1, keepdims=True)
    y = x * lax.rsqrt(ms + NORM_EPS) * gain_ref[...]
    h = y * (1.0 + scale_ref[0]) + shift_ref[0]
    hb = h.astype(BF16)
    for (start, width), o_ref in zip(groups, out_refs):
        o_ref[...] = jnp.dot(hb, w_ref[:, start:start + width],
                             preferred_element_type=F32).astype(o_ref.dtype)
    if emit_h:
        out_refs[-1][...] = h


def _norm_proj(x2d, seq, gain, shift, scale, w_bf16, groups, dtypes, emit_h=False):
    t, d = x2d.shape
    n = w_bf16.shape[1]
    tm = TOKEN_TILE
    assert t % tm == 0 and seq % tm == 0
    tiles_per_seq = seq // tm
    out_shape = [jax.ShapeDtypeStruct((t, wd), dt) for (_, wd), dt in zip(groups, dtypes)]
    out_specs = [pl.BlockSpec((tm, wd), lambda i: (i, 0)) for (_, wd) in groups]
    if emit_h:
        out_shape.append(jax.ShapeDtypeStruct((t, d), F32))
        out_specs.append(pl.BlockSpec((tm, d), lambda i: (i, 0)))
    return pl.pallas_call(
        functools.partial(_norm_proj_kernel, groups=tuple(groups), emit_h=emit_h),
        grid=(t // tm,),
        in_specs=[pl.BlockSpec((tm, d), lambda i: (i, 0)),
                  pl.BlockSpec((1, d), lambda i: (0, 0)),
                  pl.BlockSpec((1, 1, d), lambda i: (i // tiles_per_seq, 0, 0)),
                  pl.BlockSpec((1, 1, d), lambda i: (i // tiles_per_seq, 0, 0)),
                  pl.BlockSpec((d, n), lambda i: (0, 0))],
        out_specs=out_specs,
        out_shape=out_shape,
        compiler_params=_compiler_params(("arbitrary",)),
        name="norm_proj",
    )(x2d, gain.reshape(1, d), shift, scale, w_bf16)


def _sb_attention_kernel(q_ref, k_ref, v_ref, o_ref, acc_ref, carry_ref, *, tile, scale):
    i = pl.program_id(2)
    lane = lax.broadcasted_iota(I32, (1, LANES), 1)
    row = lax.broadcasted_iota(I32, (tile, tile), 0)
    col = lax.broadcasted_iota(I32, (tile, tile), 1)
    strict = col < row
    later = (row > col).astype(BF16)
    q = q_ref[0]
    outs = []
    for h in range(HEADS_PER_LANE_BLOCK):
        qh = jnp.where((lane >= HEAD_DIM * h) & (lane < HEAD_DIM * (h + 1)), q, jnp.zeros_like(q))

        def block(j, diag, qh=qh):
            start = pl.multiple_of(j * tile, tile)
            kb = k_ref[0, pl.ds(start, tile), :]
            vb = v_ref[0, pl.ds(start, tile), :]
            z = lax.dot_general(qh, kb, (((1,), (1,)), ((), ())), preferred_element_type=F32) * scale
            log_beta = _log_sigmoid(z)
            log_1m = log_beta - z
            if diag:
                log_1m = jnp.where(strict, log_1m, 0.0)
            stick = _dot_exact_rhs(log_1m, later) + carry_ref[...]
            w = jnp.exp(log_beta + stick)
            if diag:
                w = jnp.where(strict, w, 0.0)
            acc_ref[...] += jnp.dot(w.astype(BF16), vb, preferred_element_type=F32)
            carry_ref[...] += jnp.sum(log_1m, axis=1, keepdims=True)

        acc_ref[...] = jnp.zeros_like(acc_ref)
        carry_ref[...] = jnp.zeros_like(carry_ref)
        block(i, True)

        def body(it, c):
            block(i - 1 - it, False)
            return c

        lax.fori_loop(0, i, body, 0)
        outs.append(acc_ref[...])
    o_ref[0] = jnp.where(lane < HEAD_DIM, outs[0], outs[1]).astype(o_ref.dtype)


def _sb_attention(qkv, n_heads):
    b, s, _ = qkv.shape
    nblk = n_heads // HEADS_PER_LANE_BLOCK
    tile = ATTN_TILE
    assert s % tile == 0
    return pl.pallas_call(
        functools.partial(_sb_attention_kernel, tile=tile, scale=HEAD_DIM ** -0.5),
        grid=(b, nblk, s // tile),
        in_specs=[pl.BlockSpec((1, tile, LANES), lambda bi, hp, i: (bi, i, hp)),
                  pl.BlockSpec((1, s, LANES), lambda bi, hp, i: (bi, 0, nblk + hp)),
                  pl.BlockSpec((1, s, LANES), lambda bi, hp, i: (bi, 0, 2 * nblk + hp))],
        out_specs=pl.BlockSpec((1, tile, LANES), lambda bi, hp, i: (bi, i, hp)),
        out_shape=jax.ShapeDtypeStruct((b, s, n_heads * HEAD_DIM), BF16),
        scratch_shapes=[pltpu.VMEM((tile, LANES), F32), pltpu.VMEM((tile, 1), F32)],
        compiler_params=_compiler_params(("arbitrary", "arbitrary", "arbitrary")),
        name="sb_attention",
    )(qkv, qkv, qkv)


def _even_out_kernel(sb_ref, pool_ref, prev_ref, pw_ref, ps_ref, wo_ref, x_ref, g_ref, o_ref,
                     *, tm, tiles_per_seq):
    i = pl.program_id(0)
    t0 = (i % tiles_per_seq) * tm
    t = t0 + lax.broadcasted_iota(I32, (tm, 1), 0)
    s = t0 - tm + lax.broadcasted_iota(I32, (1, 2 * tm), 1)
    chunk_end = ((t // CHUNK) + 1) * CHUNK
    group = LANES
    acc = jnp.dot(sb_ref[...], wo_ref[0:4 * group, :], preferred_element_type=F32)
    for g, w in enumerate(POOL_WINDOWS):
        lo = jnp.maximum(t - w // 2, 0)
        hi = jnp.minimum(t + (w - w // 2), chunk_end)
        band = ((s >= lo) & (s < hi)).astype(BF16)
        cur = pool_ref[:, g * group:(g + 1) * group]
        ext = jnp.concatenate([prev_ref[:, g * group:(g + 1) * group], cur], axis=0)
        window_sum = _dot_exact_lhs(band, ext)
        pooled = window_sum / (hi - lo).astype(F32) - cur
        y = jnp.dot(pooled.astype(BF16), pw_ref[g], preferred_element_type=F32)
        y = y * ps_ref[:, g * group:(g + 1) * group]
        acc += jnp.dot(y.astype(BF16), wo_ref[(4 + g) * group:(5 + g) * group, :],
                       preferred_element_type=F32)
    o_ref[...] = x_ref[...] + g_ref[0] * acc


def _even_out(sb2d, pool2d, seq, pool_w_bf16, pool_scale, w_out_bf16, x2d, gate):
    t, d = x2d.shape
    tm = TOKEN_TILE
    tiles_per_seq = seq // tm
    assert seq % CHUNK == 0 and len(POOL_WINDOWS) * LANES == pool2d.shape[1]
    pw = pool2d.shape[1]
    return pl.pallas_call(
        functools.partial(_even_out_kernel, tm=tm, tiles_per_seq=tiles_per_seq),
        grid=(t // tm,),
        in_specs=[pl.BlockSpec((tm, sb2d.shape[1]), lambda i: (i, 0)),
                  pl.BlockSpec((tm, pw), lambda i: (i, 0)),
                  pl.BlockSpec((tm, pw), lambda i: (jnp.maximum(i - 1, 0), 0)),
                  pl.BlockSpec(pool_w_bf16.shape, lambda i: (0, 0, 0)),
                  pl.BlockSpec((1, pw), lambda i: (0, 0)),
                  pl.BlockSpec((d, d), lambda i: (0, 0)),
                  pl.BlockSpec((tm, d), lambda i: (i, 0)),
                  pl.BlockSpec((1, 1, d), lambda i: (i // tiles_per_seq, 0, 0))],
        out_specs=pl.BlockSpec((tm, d), lambda i: (i, 0)),
        out_shape=jax.ShapeDtypeStruct((t, d), F32),
        compiler_params=_compiler_params(("arbitrary",)),
        name="even_out",
    )(sb2d, pool2d, pool2d, pool_w_bf16, pool_scale.reshape(1, pw), w_out_bf16, x2d, gate)


def _odd_prep_kernel(qk_ref, f_ref, rw_ref, rwprev_ref, hs_ref, tri_ref, ctri_ref,
                     qg_ref, kg_ref, fb_ref, mix_ref, w0_ref, w2_ref, a0_ref, a2_ref, g2_ref,
                     kkp_ref, ka_ref, rk_ref,
                     qn_ref, kn_ref, dcol_ref, drow_ref, rt_ref, kt_ref, at_ref, bt_ref, v_ref,
                     cw_ref, bonus_ref, g_ref, carry_ref, *, tm, tiles_per_seq, width):
    i = pl.program_id(0)
    first = (i % tiles_per_seq) == 0
    hs = hs_ref[...]

    def head_sum(x):
        return _dot_exact_rhs(x, hs, 2)

    inv_hd = 1.0 / HEAD_DIM
    q = qk_ref[:, :width]
    k = qk_ref[:, width:]
    qn_ref[...] = (q * lax.rsqrt(head_sum(q * q) * inv_hd + NORM_EPS) * qg_ref[...]).astype(qn_ref.dtype)
    kn_ref[...] = (k * lax.rsqrt(head_sum(k * k) * inv_hd + NORM_EPS) * kg_ref[...]).astype(kn_ref.dtype)

    log_f = _log_sigmoid(f_ref[...] + fb_ref[...])

    @pl.when(first)
    def _():
        carry_ref[...] = jnp.zeros_like(carry_ref)

    dcum = _dot_exact_lhs(tri_ref[...], log_f) + carry_ref[...]
    carry_ref[...] = dcum[tm - 1:tm, :]
    dcol_ref[...] = dcum
    drow_ref[0, 0] = jnp.transpose(dcum)[:SUBLANES, :]

    pd = rw_ref[...]
    prev_row = jnp.where(first, 0.0, rwprev_ref[SUBLANES - 1:SUBLANES, :])
    row0 = lax.broadcasted_iota(I32, (tm, 1), 0) == 0
    prev = jnp.where(row0, prev_row, pltpu.roll(pd, 1, axis=0))
    pd = pd + (prev - pd) * mix_ref[...]
    r = pd[:, :width]
    kr = pd[:, width:2 * width]
    vr = pd[:, 2 * width:3 * width]
    lora_in = pd[:, 3 * width:3 * width + LANES]
    xg = pd[:, 3 * width + LANES:3 * width + 2 * LANES]
    w_log = _log_sigmoid(w0_ref[...] + jnp.dot(jnp.tanh(lora_in).astype(BF16), w2_ref[...],
                                              preferred_element_type=F32)) - 0.5
    log_w = -jnp.exp(w_log)
    a = _sigmoid(a0_ref[...] + jnp.dot(lora_in.astype(BF16), a2_ref[...], preferred_element_type=F32))
    g_ref[...] = jnp.dot(_sigmoid(xg).astype(BF16), g2_ref[...], preferred_element_type=F32)
    kk = kr * kkp_ref[...]
    kk = kk / jnp.maximum(jnp.sqrt(head_sum(kk * kk)), 1e-12)
    k2 = kr * (1.0 + (a - 1.0) * ka_ref[...])
    bonus_ref[...] = head_sum(r * k2 * rk_ref[...]) * vr
    cw = _dot_exact_lhs(ctri_ref[...], log_w)
    grow = jnp.exp(-cw)
    rt_ref[...] = r * jnp.exp(cw)
    kt_ref[...] = k2 * grow
    bt_ref[...] = kk * a * grow
    at_ref[...] = -kk * jnp.exp(cw - log_w)
    v_ref[...] = vr
    cw_ref[...] = cw


def _odd_prep(qk, fproj, rw, seq, p):
    t = qk.shape[0]
    width = qk.shape[1] // 2
    tm = TOKEN_TILE
    tiles_per_seq = seq // tm
    nrw = rw.shape[1]
    assert tm % CHUNK == 0 and seq % tm == 0 and nrw == 3 * width + 2 * LANES
    ri = lax.broadcasted_iota(I32, (tm, tm), 0)
    ci = lax.broadcasted_iota(I32, (tm, tm), 1)
    tri = (ci <= ri).astype(BF16)
    ctri = ((ci <= ri) & (ci // CHUNK == ri // CHUNK)).astype(BF16)
    hs = _head_sum_matrix(width)
    row = lambda a: a.reshape(1, -1)
    full = lambda a: pl.BlockSpec(a.shape, lambda i: (0,) * a.ndim)
    tile = lambda n: pl.BlockSpec((tm, n), lambda i: (i, 0))
    consts = [hs, tri, ctri, row(p['qg']), row(p['kg']), row(p['fb']), row(p['mix']), row(p['w0']),
              p['w2'], row(p['a0']), p['a2'], p['g2'], row(p['kk']), row(p['ka']), row(p['rk'])]
    wide = jax.ShapeDtypeStruct((t, width), F32)
    return pl.pallas_call(
        functools.partial(_odd_prep_kernel, tm=tm, tiles_per_seq=tiles_per_seq, width=width),
        grid=(t // tm,),
        in_specs=[tile(2 * width), tile(LANES), tile(nrw),
                  pl.BlockSpec((SUBLANES, nrw), lambda i: (jnp.maximum(i * (tm // SUBLANES) - 1, 0), 0))]
                 + [full(c) for c in consts],
        out_specs=[tile(width), tile(width), tile(LANES),
                   pl.BlockSpec((1, 1, SUBLANES, tm), lambda i: (i // tiles_per_seq, i % tiles_per_seq, 0, 0))]
                  + [tile(width)] * 8,
        out_shape=[jax.ShapeDtypeStruct((t, width), BF16), jax.ShapeDtypeStruct((t, width), BF16),
                   jax.ShapeDtypeStruct((t, LANES), F32),
                   jax.ShapeDtypeStruct((t // seq, tiles_per_seq, SUBLANES, tm), F32)] + [wide] * 8,
        scratch_shapes=[pltpu.VMEM((1, LANES), F32)],
        compiler_params=_compiler_params(("arbitrary",)),
        name="odd_prep",
    )(qk, fproj, rw, rw, *consts)


def _fox_attention_kernel(q_ref, k_ref, v_ref, dcol_ref, drow_ref, o_ref, acc_ref, m_ref, l_ref,
                          *, tile, scale):
    hp = pl.program_id(1)
    i = pl.program_id(2)
    lane = lax.broadcasted_iota(I32, (1, LANES), 1)
    row = lax.broadcasted_iota(I32, (tile, tile), 0)
    col = lax.broadcasted_iota(I32, (tile, tile), 1)
    causal = col <= row
    sub = lax.broadcasted_iota(I32, (SUBLANES, tile), 0)
    q = q_ref[0]
    dq_all = dcol_ref[0]
    outs = []
    for h in range(HEADS_PER_LANE_BLOCK):
        head = hp * HEADS_PER_LANE_BLOCK + h
        qh = jnp.where((lane >= HEAD_DIM * h) & (lane < HEAD_DIM * (h + 1)), q, jnp.zeros_like(q))
        dq = jnp.sum(jnp.where(lane == head, dq_all, 0.0), axis=1, keepdims=True)

        def block(j, diag, qh=qh, dq=dq, head=head):
            start = pl.multiple_of(j * tile, tile)
            kb = k_ref[0, pl.ds(start, tile), :]
            vb = v_ref[0, pl.ds(start, tile), :]
            dk = jnp.sum(jnp.where(sub == head, drow_ref[0, j], 0.0), axis=0, keepdims=True)
            s = lax.dot_general(qh, kb, (((1,), (1,)), ((), ())), preferred_element_type=F32) * scale
            s = s + dq - dk
            if diag:
                s = jnp.where(causal, s, -jnp.inf)
            m_prev = m_ref[...]
            m_new = jnp.maximum(m_prev, jnp.max(s, axis=1, keepdims=True))
            p = jnp.exp(s - m_new)
            alpha = jnp.exp(m_prev - m_new)
            l_ref[...] = alpha * l_ref[...] + jnp.sum(p, axis=1, keepdims=True)
            acc_ref[...] = alpha * acc_ref[...] + jnp.dot(p.astype(BF16), vb, preferred_element_type=F32)
            m_ref[...] = m_new

        acc_ref[...] = jnp.zeros_like(acc_ref)
        l_ref[...] = jnp.zeros_like(l_ref)
        m_ref[...] = jnp.full_like(m_ref, -jnp.inf)
        block(i, True)

        def body(it, c):
            block(i - 1 - it, False)
            return c

        lax.fori_loop(0, i, body, 0)
        outs.append(acc_ref[...] / l_ref[...])
    o_ref[0] = jnp.where(lane < HEAD_DIM, outs[0], outs[1])


def _fox_attention(qn, kn, v, dcol, drow):
    b, s, w = qn.shape
    nblk = w // LANES
    tile = ATTN_TILE
    assert drow.shape == (b, s // tile, SUBLANES, tile) and w // HEAD_DIM <= SUBLANES
    return pl.pallas_call(
        functools.partial(_fox_attention_kernel, tile=tile, scale=HEAD_DIM ** -0.5),
        grid=(b, nblk, s // tile),
        in_specs=[pl.BlockSpec((1, tile, LANES), lambda bi, hp, i: (bi, i, hp)),
                  pl.BlockSpec((1, s, LANES), lambda bi, hp, i: (bi, 0, hp)),
                  pl.BlockSpec((1, s, LANES), lambda bi, hp, i: (bi, 0, hp)),
                  pl.BlockSpec((1, tile, LANES), lambda bi, hp, i: (bi, i, 0)),
                  pl.BlockSpec((1, s // tile, SUBLANES, tile), lambda bi, hp, i: (bi, 0, 0, 0))],
        out_specs=pl.BlockSpec((1, tile, LANES), lambda bi, hp, i: (bi, i, hp)),
        out_shape=jax.ShapeDtypeStruct((b, s, w), F32),
        scratch_shapes=[pltpu.VMEM((tile, LANES), F32), pltpu.VMEM((tile, 1), F32),
                        pltpu.VMEM((tile, 1), F32)],
        compiler_params=_compiler_params(("arbitrary", "arbitrary", "arbitrary")),
        name="fox_attention",
    )(qn, kn, v, dcol, drow)


def _mm(a, b):
    return jnp.dot(a, b, precision=HIGHEST, preferred_element_type=F32)


def _mm_nt(a, b):
    return lax.dot_general(a, b, (((1,), (1,)), ((), ())), precision=HIGHEST, preferred_element_type=F32)


def _mm_tn(a, b):
    return lax.dot_general(a, b, (((0,), (0,)), ((), ())), precision=HIGHEST, preferred_element_type=F32)


def _rwkv_scan_kernel(rt_ref, kt_ref, at_ref, bt_ref, v_ref, cw_ref, y_ref, st_ref,
                      *, steps_per_seq, chunks, heads):
    i = pl.program_id(0)

    @pl.when(i % steps_per_seq == 0)
    def _():
        st_ref[...] = jnp.zeros_like(st_ref)

    c = CHUNK
    ri = lax.broadcasted_iota(I32, (c, c), 0)
    ci = lax.broadcasted_iota(I32, (c, c), 1)
    strict = ci < ri
    lower = ci <= ri
    eye = (ri == ci).astype(F32)

    def chunk(cidx, carry):
        rows = pl.ds(pl.multiple_of(cidx * c, c), c)
        last = pl.ds(pl.multiple_of(cidx * c, c) + (c - SUBLANES), SUBLANES)
        ys = []
        for h in range(heads):
            cols = slice(h * HEAD_DIM, (h + 1) * HEAD_DIM)
            rt = rt_ref[rows, cols]
            kt = kt_ref[rows, cols]
            at = at_ref[rows, cols]
            bt = bt_ref[rows, cols]
            vv = v_ref[rows, cols]
            decay = jnp.exp(cw_ref[last, cols][SUBLANES - 1:SUBLANES, :])
            st = st_ref[h]
            a_ab = jnp.where(strict, _mm_nt(at, bt), 0.0)
            a_ak = jnp.where(strict, _mm_nt(at, kt), 0.0)
            a_rb = jnp.where(lower, _mm_nt(rt, bt), 0.0)
            a_rk = jnp.where(lower, _mm_nt(rt, kt), 0.0)
            inv = eye + a_ab
            power = a_ab
            for _ in range(5):
                power = _mm(power, power)
                inv = inv + _mm(inv, power)
            u = _mm(inv, _mm_nt(at, st) + _mm(a_ak, vv))
            ys.append(_mm_nt(rt, st) + _mm(a_rb, u) + _mm(a_rk, vv))
            st_ref[h] = (st + _mm_tn(u, bt) + _mm_tn(vv, kt)) * decay
        y_ref[rows, :] = jnp.concatenate(ys, axis=1)
        return carry

    lax.fori_loop(0, chunks, chunk, 0)


def _rwkv_scan(rt, kt, at, bt, v, cw, seq):
    t, w = rt.shape
    heads = w // HEAD_DIM
    ts = 2 * TOKEN_TILE
    assert seq % ts == 0
    spec = pl.BlockSpec((ts, w), lambda i: (i, 0))
    return pl.pallas_call(
        functools.partial(_rwkv_scan_kernel, steps_per_seq=seq // ts, chunks=ts // CHUNK, heads=heads),
        grid=(t // ts,),
        in_specs=[spec] * 6,
        out_specs=spec,
        out_shape=jax.ShapeDtypeStruct((t, w), F32),
        scratch_shapes=[pltpu.VMEM((heads, HEAD_DIM, HEAD_DIM), F32)],
        compiler_params=_compiler_params(("arbitrary",)),
        name="rwkv_scan",
    )(rt, kt, at, bt, v, cw)


def _odd_out_kernel(att_ref, og_ref, y_ref, bonus_ref, g_ref, hs_ref, lnw_ref, lnb_ref, wo_ref,
                    x_ref, gate_ref, o_ref, *, width):
    hs = hs_ref[...]
    inv_hd = 1.0 / HEAD_DIM
    fox = att_ref[...] * _sigmoid(og_ref[...])
    y = y_ref[...]
    mu = _dot_exact_rhs(y, hs, 2) * inv_hd
    yc = y - mu
    var = _dot_exact_rhs(yc * yc, hs, 2) * inv_hd
    yn = yc * lax.rsqrt(var + RWKV_GN_EPS) * lnw_ref[...] + lnb_ref[...]
    rw = (yn + bonus_ref[...]) * g_ref[...]
    acc = jnp.dot(fox.astype(BF16), wo_ref[:width, :], preferred_element_type=F32)
    acc += jnp.dot(rw.astype(BF16), wo_ref[width:, :], preferred_element_type=F32)
    o_ref[...] = x_ref[...] + gate_ref[0] * acc


def _odd_out(att, og, y, bonus, g, ln_w, ln_b, w_out_bf16, x2d, seq, gate):
    t, d = x2d.shape
    width = att.shape[1]
    tm = TOKEN_TILE
    tiles_per_seq = seq // tm
    hs = _head_sum_matrix(width)
    tile = lambda n: pl.BlockSpec((tm, n), lambda i: (i, 0))
    return pl.pallas_call(
        functools.partial(_odd_out_kernel, width=width),
        grid=(t // tm,),
        in_specs=[tile(width)] * 5
                 + [pl.BlockSpec(hs.shape, lambda i: (0, 0)),
                    pl.BlockSpec((1, width), lambda i: (0, 0)), pl.BlockSpec((1, width), lambda i: (0, 0)),
                    pl.BlockSpec((d, d), lambda i: (0, 0)), tile(d),
                    pl.BlockSpec((1, 1, d), lambda i: (i // tiles_per_seq, 0, 0))],
        out_specs=tile(d),
        out_shape=jax.ShapeDtypeStruct((t, d), F32),
        compiler_params=_compiler_params(("arbitrary",)),
        name="odd_out",
    )(att, og, y, bonus, g, hs, ln_w.reshape(1, width), ln_b.reshape(1, width), w_out_bf16, x2d, gate)


def _top_rows(s, k, payload=None):
    n = s.shape[0]
    rid = lax.broadcasted_iota(I32, s.shape, 0)
    vals, picks = [], []
    for _ in range(k):
        m = jnp.max(s, axis=0, keepdims=True)
        am = jnp.min(jnp.where(s == m, rid, n), axis=0, keepdims=True)
        hit = rid == am
        vals.append(m)
        if payload is None:
            picks.append(am)
        else:
            picks.append(jnp.max(jnp.where(hit, payload, -1), axis=0, keepdims=True))
        s = jnp.where(hit, -jnp.inf, s)
    return jnp.concatenate(vals, axis=0), jnp.concatenate(picks, axis=0)


def _peer_topk_kernel(q_ref, k1_ref, k2_ref, idx_ref, gate_ref):
    half = PEER_NKEYS
    nt = (((1,), (1,)), ((), ()))
    for h in range(PEER_HEADS):
        q1 = q_ref[:, (2 * h) * half:(2 * h + 1) * half]
        q2 = q_ref[:, (2 * h + 1) * half:(2 * h + 2) * half]
        s1 = lax.dot_general(k1_ref[...], q1, nt, precision=HIGHEST, preferred_element_type=F32)
        s2 = lax.dot_general(k2_ref[...], q2, nt, precision=HIGHEST, preferred_element_type=F32)
        v1, i1 = _top_rows(s1, PEER_TOPK)
        v2, i2 = _top_rows(s2, PEER_TOPK)
        cand = jnp.concatenate([v1[a:a + 1] + v2 for a in range(PEER_TOPK)], axis=0)
        cand_idx = jnp.concatenate([i1[a:a + 1] * PEER_NKEYS + i2 for a in range(PEER_TOPK)], axis=0)
        top_s, idx = _top_rows(cand, PEER_TOPK, payload=cand_idx)
        e = jnp.exp(top_s - top_s[0:1])
        gate = e / jnp.sum(e, axis=0, keepdims=True)
        idx_ref[h * PEER_TOPK:(h + 1) * PEER_TOPK, :] = idx
        gate_ref[h * PEER_TOPK:(h + 1) * PEER_TOPK, :] = gate


def _peer_topk(q2d, k1, k2):
    t, n = q2d.shape
    tt = TOKEN_TILE
    return pl.pallas_call(
        _peer_topk_kernel,
        grid=(t // tt,),
        in_specs=[pl.BlockSpec((tt, n), lambda i: (i, 0)),
                  pl.BlockSpec(k1.shape, lambda i: (0, 0)),
                  pl.BlockSpec(k2.shape, lambda i: (0, 0))],
        out_specs=[pl.BlockSpec((PEER_SLOTS, tt), lambda i: (0, i)),
                   pl.BlockSpec((PEER_SLOTS, tt), lambda i: (0, i))],
        out_shape=[jax.ShapeDtypeStruct((PEER_SLOTS, t), I32),
                   jax.ShapeDtypeStruct((PEER_SLOTS, t), F32)],
        compiler_params=_compiler_params(("arbitrary",)),
        name="peer_topk",
    )(q2d, k1, k2)


def _pack_table(table):
    e, d = table.shape
    pairs = d // (2 * LANES)
    t = table.astype(BF16).reshape(e, pairs, 2, LANES).transpose(0, 1, 3, 2)
    words = lax.bitcast_convert_type(t, jnp.uint32)
    return lax.bitcast_convert_type(words, I32).reshape(e * pairs, LANES)


def _unpack_words(w):
    lo = pltpu.bitcast(lax.shift_left(w, 16), F32)
    hi = pltpu.bitcast(lax.bitwise_and(w, jnp.int32(-65536)), F32)
    return lo, hi


def _gather_rows(idx_ref, tab_ref, tile_ref, tok, pairs):
    for m in range(PEER_SLOTS):
        e = idx_ref[m, tok]
        slab = tab_ref[pl.ds(pl.multiple_of(e * pairs, pairs), pairs), :]
        for j in range(pairs):
            tile_ref[j * TILE_PITCH + m:j * TILE_PITCH + m + 1, :] = slab[j:j + 1, :]


def _pick_row(block, r):
    sub = lax.broadcasted_iota(I32, block.shape, 0)
    return jnp.sum(jnp.where(sub == r, block, 0.0), axis=0, keepdims=True)


def _peer_act_kernel(idx_ref, tab_ref, h_ref, gate_ref, coef_ref, tile_ref, *, pairs):
    ntok = h_ref.shape[0]
    lane = lax.broadcasted_iota(I32, (PEER_SLOTS, ntok), 1)

    def token(tok, act_t):
        _gather_rows(idx_ref, tab_ref, tile_ref, tok, pairs)
        base = pl.multiple_of((tok // SUBLANES) * SUBLANES, SUBLANES)
        h_row = _pick_row(h_ref[pl.ds(base, SUBLANES), :], tok % SUBLANES)
        r = jnp.zeros((PEER_SLOTS, LANES), F32)
        for j in range(pairs):
            lo, hi = _unpack_words(tile_ref[j * TILE_PITCH:j * TILE_PITCH + PEER_SLOTS, :])
            h_lo = h_row[:, (2 * j) * LANES:(2 * j + 1) * LANES]
            h_hi = h_row[:, (2 * j + 1) * LANES:(2 * j + 2) * LANES]
            r = r + lo * h_lo + hi * h_hi
        col = jnp.sum(r, axis=1, keepdims=True)
        return jnp.where(lane == tok, col, act_t)

    act_t = lax.fori_loop(0, ntok, token, jnp.zeros((PEER_SLOTS, ntok), F32))
    gelu = 0.5 * act_t * (1.0 + lax.erf(act_t * (2.0 ** -0.5)))
    coef_ref[...] = gate_ref[...] * gelu


def _peer_act(idx_t, table_words, h2d, gate_t):
    t, d = h2d.shape
    pairs = d // (2 * LANES)
    nt = PEER_TOKENS
    return pl.pallas_call(
        functools.partial(_peer_act_kernel, pairs=pairs),
        grid=(t // nt,),
        in_specs=[pl.BlockSpec((PEER_SLOTS, nt), lambda i: (0, i), memory_space=pltpu.SMEM),
                  pl.BlockSpec(table_words.shape, lambda i: (0, 0), pipeline_mode=pl.Buffered(1)),
                  pl.BlockSpec((nt, d), lambda i: (i, 0)),
                  pl.BlockSpec((PEER_SLOTS, nt), lambda i: (0, i))],
        out_specs=pl.BlockSpec((PEER_SLOTS, nt), lambda i: (0, i)),
        out_shape=jax.ShapeDtypeStruct((PEER_SLOTS, t), F32),
        scratch_shapes=[pltpu.VMEM((pairs * TILE_PITCH, LANES), I32)],
        compiler_params=_compiler_params(("arbitrary",), vmem_mb=56),
        name="peer_act",
    )(idx_t, table_words, h2d, gate_t)


def _peer_out_kernel(idx_ref, tab_ref, coef_ref, x_ref, g_ref, o_ref, tile_ref, *, pairs):
    ntok, d = x_ref.shape
    lane = lax.broadcasted_iota(I32, (PEER_SLOTS, ntok), 1)
    sub = lax.broadcasted_iota(I32, (SUBLANES, d), 0)
    groups = PEER_SLOTS // SUBLANES

    def token(tok, rows):
        _gather_rows(idx_ref, tab_ref, tile_ref, tok, pairs)
        coef = jnp.sum(jnp.where(lane == tok, coef_ref[...], 0.0), axis=1, keepdims=True)
        pieces = []
        for j in range(pairs):
            lo, hi = _unpack_words(tile_ref[j * TILE_PITCH:j * TILE_PITCH + PEER_SLOTS, :])
            for vals in (lo, hi):
                prod = (vals * coef).reshape(groups, SUBLANES, LANES)
                pieces.append(jnp.sum(jnp.sum(prod, axis=0), axis=0, keepdims=True))
        row = jnp.concatenate(pieces, axis=1)
        return jnp.where(sub == tok % SUBLANES, row, rows)

    def group(gi, c):
        base = pl.multiple_of(gi * SUBLANES, SUBLANES)
        rows = lax.fori_loop(base, base + SUBLANES, token, jnp.zeros((SUBLANES, d), F32))
        o_ref[pl.ds(base, SUBLANES), :] = x_ref[pl.ds(base, SUBLANES), :] + g_ref[0] * rows
        return c

    lax.fori_loop(0, ntok // SUBLANES, group, 0)


def _peer_out(idx_t, table_words, coef_t, x2d, seq, gate):
    t, d = x2d.shape
    pairs = d // (2 * LANES)
    nt = PEER_TOKENS
    per_seq = seq // nt
    return pl.pallas_call(
        functools.partial(_peer_out_kernel, pairs=pairs),
        grid=(t // nt,),
        in_specs=[pl.BlockSpec((PEER_SLOTS, nt), lambda i: (0, i), memory_space=pltpu.SMEM),
                  pl.BlockSpec(table_words.shape, lambda i: (0, 0), pipeline_mode=pl.Buffered(1)),
                  pl.BlockSpec((PEER_SLOTS, nt), lambda i: (0, i)),
                  pl.BlockSpec((nt, d), lambda i: (i, 0)),
                  pl.BlockSpec((1, 1, d), lambda i: (i // per_seq, 0, 0))],
        out_specs=pl.BlockSpec((nt, d), lambda i: (i, 0)),
        out_shape=jax.ShapeDtypeStruct((t, d), F32),
        scratch_shapes=[pltpu.VMEM((pairs * TILE_PITCH, LANES), I32)],
        compiler_params=_compiler_params(("arbitrary",), vmem_mb=56),
        name="peer_out",
    )(idx_t, table_words, coef_t, x2d, gate)


def _peer_ffn(x2d, seq, gain, shift, scale, gate, wq_bf16, k1, k2, u_words, v_words):
    n = wq_bf16.shape[1]
    q, h = _norm_proj(x2d, seq, gain, shift, scale, wq_bf16, [(0, n)], [F32], emit_h=True)
    idx_t, gate_t = _peer_topk(q, k1, k2)
    coef_t = _peer_act(idx_t, u_words, h, gate_t)
    return _peer_out(idx_t, v_words, coef_t, x2d, seq, gate)


def _final_norm_kernel(x_ref, gain_ref, o_ref):
    x = x_ref[...]
    ms = jnp.mean(x * x, axis=-1, keepdims=True)
    o_ref[...] = x * lax.rsqrt(ms + NORM_EPS) * gain_ref[...]


def _final_norm(x2d, gain):
    t, d = x2d.shape
    tm = TOKEN_TILE
    return pl.pallas_call(
        _final_norm_kernel,
        grid=(t // tm,),
        in_specs=[pl.BlockSpec((tm, d), lambda i: (i, 0)), pl.BlockSpec((1, d), lambda i: (0, 0))],
        out_specs=pl.BlockSpec((tm, d), lambda i: (i, 0)),
        out_shape=jax.ShapeDtypeStruct((t, d), F32),
        compiler_params=_compiler_params(("arbitrary",)),
        name="final_norm",
    )(x2d, gain.reshape(1, d))


def _even_mixer(x2d, b, seq, gain, shift, scale, gate, w_in, w_out, pool_w, pool_scale):
    d = x2d.shape[1]
    pool_width = len(POOL_WINDOWS) * LANES
    sb_width = (w_in.shape[1] - pool_width) // 3
    qkv, pool = _norm_proj(x2d, seq, gain, shift, scale, w_in.astype(BF16),
                           [(0, 3 * sb_width), (3 * sb_width, pool_width)], [BF16, F32])
    sb = _sb_attention(qkv.reshape(b, seq, 3 * sb_width), sb_width // HEAD_DIM)
    return _even_out(sb.reshape(b * seq, sb_width), pool, seq, pool_w.astype(BF16), pool_scale,
                     w_out.astype(BF16), x2d, gate)


def _odd_mixer(x2d, b, seq, gain, shift, scale, gate, w_in, w_out, fox_qnorm, fox_knorm, fox_fbias,
               rwkv_mix, rwkv_w0, rwkv_w2, rwkv_a0, rwkv_a2, rwkv_g2, rwkv_kk, rwkv_ka, rwkv_rk,
               rwkv_ln_w, rwkv_ln_b):
    d = x2d.shape[1]
    width = rwkv_w0.shape[0]
    heads = width // HEAD_DIM
    fox_in = 4 * width + heads
    n_rw = w_in.shape[1] - fox_in
    w_cols = jnp.concatenate([w_in[:, :4 * width], w_in[:, fox_in:], w_in[:, 4 * width:fox_in],
                              jnp.zeros((d, LANES - heads), w_in.dtype)], axis=1).astype(BF16)
    qk, v, og, rw, fproj = _norm_proj(
        x2d, seq, gain, shift, scale, w_cols,
        [(0, 2 * width), (2 * width, width), (3 * width, width), (4 * width, n_rw), (4 * width + n_rw, LANES)],
        [F32, BF16, F32, F32, F32])
    lora = rwkv_w2.shape[0]
    assert 2 * lora == LANES and rwkv_a2.shape[0] == lora and rwkv_g2.shape[0] == LANES
    zeros = jnp.zeros((lora, width), F32)
    params = dict(
        qg=jnp.tile(fox_qnorm, heads), kg=jnp.tile(fox_knorm, heads),
        fb=jnp.concatenate([fox_fbias, jnp.zeros((LANES - heads,), F32)]),
        mix=rwkv_mix, w0=rwkv_w0, a0=rwkv_a0,
        w2=jnp.concatenate([rwkv_w2, zeros], axis=0).astype(BF16),
        a2=jnp.concatenate([zeros, rwkv_a2], axis=0).astype(BF16),
        g2=rwkv_g2.astype(BF16), kk=rwkv_kk, ka=rwkv_ka, rk=rwkv_rk.reshape(-1))
    qn, kn, dcol, drow, rt, kt, at, bt, vv, cw, bonus, g = _odd_prep(qk, fproj, rw, seq, params)
    r3 = lambda a: a.reshape(b, seq, -1)
    att = _fox_attention(r3(qn), r3(kn), r3(v), r3(dcol), drow)
    y = _rwkv_scan(rt, kt, at, bt, vv, cw, seq)
    return _odd_out(att.reshape(b * seq, width), og, y, bonus, g, rwkv_ln_w, rwkv_ln_b,
                    w_out.astype(BF16), x2d, seq, gate)


def kernel(x, c, ada_w, ada_b, norm_mix, norm_ffn, ev_w_in, ev_w_out, pool_w, pool_scale, od_w_in, od_w_out, fox_qnorm, fox_knorm, fox_fbias, rwkv_mix, rwkv_w0, rwkv_w2, rwkv_a0, rwkv_a2, rwkv_g2, rwkv_kk, rwkv_ka, rwkv_rk, rwkv_ln_w, rwkv_ln_b, peer_wq, peer_k1, peer_k2, peer_u, peer_v, final_norm):
    b, seq, d = x.shape
    depth = ada_w.shape[0]
    assert TOKEN_TILE == ATTN_TILE and seq % (2 * TOKEN_TILE) == 0
    mod = _adaln(c, ada_w, ada_b).reshape(depth, b, 6, 1, d)
    x2d = x.reshape(b * seq, d)
    for layer in range(depth):
        sh1, sc1, g1, sh2, sc2, g2 = (mod[layer, :, i] for i in range(6))
        j = layer // 2
        if layer % 2 == 0:
            x2d = _even_mixer(x2d, b, seq, norm_mix[layer], sh1, sc1, g1, ev_w_in[j], ev_w_out[j],
                              pool_w[j], pool_scale[j])
        else:
            x2d = _odd_mixer(x2d, b, seq, norm_mix[layer], sh1, sc1, g1, od_w_in[j], od_w_out[j],
                             fox_qnorm[j], fox_knorm[j], fox_fbias[j], rwkv_mix[j], rwkv_w0[j],
                             rwkv_w2[j], rwkv_a0[j], rwkv_a2[j], rwkv_g2[j], rwkv_kk[j], rwkv_ka[j],
                             rwkv_rk[j], rwkv_ln_w[j], rwkv_ln_b[j])
        x2d = _peer_ffn(x2d, seq, norm_ffn[layer], sh2, sc2, g2, peer_wq[layer].astype(BF16),
                        peer_k1[layer], peer_k2[layer], _pack_table(peer_u[layer]),
                        _pack_table(peer_v[layer]))
    return _final_norm(x2d, final_norm).reshape(b, seq, d)
```

```python
import functools

import jax
import jax.numpy as jnp
from jax import lax
from jax.experimental import pallas as pl
from jax.experimental.pallas import tpu as pltpu

F32 = jnp.float32
BF16 = jnp.bfloat16
I32 = jnp.int32
HIGHEST = lax.Precision.HIGHEST

LANES = 128
SUBLANES = 8
VMEM_BYTES_V7X = 64 * 1024 * 1024

HEAD_DIM = 64
HEADS_PER_LANE_BLOCK = LANES // HEAD_DIM
NORM_EPS = 1e-6
RWKV_GN_EPS = 64e-5
CHUNK = 64
POOL_WINDOWS = (2, 4, 8, 16)
PEER_TOPK = 16
PEER_NKEYS = 128
PEER_HEADS = 8
PEER_SLOTS = PEER_HEADS * PEER_TOPK

TOKEN_TILE = 256
ATTN_TILE = 256
PEER_TOKENS = 128

F32_EXP_UNDERFLOW = -104.0


def _compiler_params(semantics, vmem_mb=48):
    return pltpu.CompilerParams(dimension_semantics=semantics,
                                vmem_limit_bytes=min(vmem_mb * 1024 * 1024, VMEM_BYTES_V7X))


def _split_bf16(x, n):
    parts = []
    r = x
    for _ in range(n):
        p = r.astype(BF16)
        parts.append(p)
        r = r - p.astype(F32)
    return parts


def _dot_exact_rhs(x, m, n=3):
    acc = None
    for p in _split_bf16(x, n):
        t = jnp.dot(p, m, preferred_element_type=F32)
        acc = t if acc is None else acc + t
    return acc


def _dot_exact_lhs(m, x, n=3):
    acc = None
    for p in _split_bf16(x, n):
        t = jnp.dot(m, p, preferred_element_type=F32)
        acc = t if acc is None else acc + t
    return acc


def _log_sigmoid(z):
    return jnp.minimum(z, 0.0) - jnp.log1p(jnp.exp(-jnp.abs(z)))


def _sigmoid(z):
    return 1.0 / (1.0 + jnp.exp(-z))


def _head_sum_matrix(width):
    r = lax.broadcasted_iota(I32, (width, width), 0) // HEAD_DIM
    c = lax.broadcasted_iota(I32, (width, width), 1) // HEAD_DIM
    return (r == c).astype(BF16)


def _adaln_kernel(c_ref, w_ref, b_ref, o_ref):
    c = c_ref[...]
    c_act = c * _sigmoid(c)
    o_ref[0] = jnp.dot(c_act, w_ref[0], precision=HIGHEST, preferred_element_type=F32) + b_ref[0]


def _adaln(c, ada_w, ada_b):
    b, d = c.shape
    depth, _, n = ada_w.shape
    tn = 1536
    assert n % tn == 0 and b <= SUBLANES
    c_pad = jnp.zeros((SUBLANES, d), F32).at[:b].set(c)
    out = pl.pallas_call(
        _adaln_kernel,
        grid=(depth, n // tn),
        in_specs=[pl.BlockSpec((SUBLANES, d), lambda l, j: (0, 0)),
                  pl.BlockSpec((1, d, tn), lambda l, j: (l, 0, j)),
                  pl.BlockSpec((1, 1, tn), lambda l, j: (l, 0, j))],
        out_specs=pl.BlockSpec((1, SUBLANES, tn), lambda l, j: (l, 0, j)),
        out_shape=jax.ShapeDtypeStruct((depth, SUBLANES, n), F32),
        compiler_params=_compiler_params(("arbitrary", "arbitrary")),
        name="adaln",
    )(c_pad, ada_w, ada_b.reshape(depth, 1, n))
    return out[:, :b, :]


def _norm_proj_kernel(x_ref, gain_ref, shift_ref, scale_ref, w_ref, *out_refs, groups, emit_h):
    x = x_ref[...]
    ms = jnp.mean(x * x, axis=-1, keepdims=True)
    y = x * lax.rsqrt(ms + NORM_EPS) * gain_ref[...]
    h = y * (1.0 + scale_ref[0]) + shift_ref[0]
    hb = h.astype(BF16)
    for (start, width), o_ref in zip(groups, out_refs):
        o_ref[...] = jnp.dot(hb, w_ref[:, start:start + width],
                             preferred_element_type=F32).astype(o_ref.dtype)
    if emit_h:
        out_refs[-1][...] = h


def _norm_proj(x2d, seq, gain, shift, scale, w_bf16, groups, dtypes, emit_h=False):
    t, d = x2d.shape
    n = w_bf16.shape[1]
    tm = TOKEN_TILE
    assert t % tm == 0 and seq % tm == 0
    tiles_per_seq = seq // tm
    out_shape = [jax.ShapeDtypeStruct((t, wd), dt) for (_, wd), dt in zip(groups, dtypes)]
    out_specs = [pl.BlockSpec((tm, wd), lambda i: (i, 0)) for (_, wd) in groups]
    if emit_h:
        out_shape.append(jax.ShapeDtypeStruct((t, d), F32))
        out_specs.append(pl.BlockSpec((tm, d), lambda i: (i, 0)))
    return pl.pallas_call(
        functools.partial(_norm_proj_kernel, groups=tuple(groups), emit_h=emit_h),
        grid=(t // tm,),
        in_specs=[pl.BlockSpec((tm, d), lambda i: (i, 0)),
                  pl.BlockSpec((1, d), lambda i: (0, 0)),
                  pl.BlockSpec((1, 1, d), lambda i: (i // tiles_per_seq, 0, 0)),
                  pl.BlockSpec((1, 1, d), lambda i: (i // tiles_per_seq, 0, 0)),
                  pl.BlockSpec((d, n), lambda i: (0, 0))],
        out_specs=out_specs,
        out_shape=out_shape,
        compiler_params=_compiler_params(("arbitrary",)),
        name="norm_proj",
    )(x2d, gain.reshape(1, d), shift, scale, w_bf16)


def _sb_attention_kernel(q_ref, k_ref, v_ref, o_ref, acc_ref, carry_ref, *, tile, scale):
    i = pl.program_id(2)
    lane = lax.broadcasted_iota(I32, (1, LANES), 1)
    row = lax.broadcasted_iota(I32, (tile, tile), 0)
    col = lax.broadcasted_iota(I32, (tile, tile), 1)
    strict = col < row
    later = (row > col).astype(BF16)
    q = q_ref[0]
    outs = []
    for h in range(HEADS_PER_LANE_BLOCK):
        qh = jnp.where((lane >= HEAD_DIM * h) & (lane < HEAD_DIM * (h + 1)), q, jnp.zeros_like(q))

        def block(j, diag, qh=qh):
            start = pl.multiple_of(j * tile, tile)
            kb = k_ref[0, pl.ds(start, tile), :]
            vb = v_ref[0, pl.ds(start, tile), :]
            z = lax.dot_general(qh, kb, (((1,), (1,)), ((), ())), preferred_element_type=F32) * scale
            log_beta = _log_sigmoid(z)
            log_1m = log_beta - z
            if diag:
                log_1m = jnp.where(strict, log_1m, 0.0)
            stick = _dot_exact_rhs(log_1m, later) + carry_ref[...]
            w = jnp.exp(log_beta + stick)
            if diag:
                w = jnp.where(strict, w, 0.0)
            acc_ref[...] += jnp.dot(w.astype(BF16), vb, preferred_element_type=F32)
            carry_ref[...] += jnp.sum(log_1m, axis=1, keepdims=True)

        acc_ref[...] = jnp.zeros_like(acc_ref)
        carry_ref[...] = jnp.zeros_like(carry_ref)
        block(i, True)

        def alive():
            return jnp.max(carry_ref[...]) > F32_EXP_UNDERFLOW

        def cond(c):
            return jnp.logical_and(c[0] < i, c[1])

        def body(c):
            block(i - 1 - c[0], False)
            return c[0] + 1, alive()

        lax.while_loop(cond, body, (jnp.int32(0), alive()))
        outs.append(acc_ref[...])
    o_ref[0] = jnp.where(lane < HEAD_DIM, outs[0], outs[1]).astype(o_ref.dtype)


def _sb_attention(qkv, n_heads):
    b, s, _ = qkv.shape
    nblk = n_heads // HEADS_PER_LANE_BLOCK
    tile = ATTN_TILE
    assert s % tile == 0
    return pl.pallas_call(
        functools.partial(_sb_attention_kernel, tile=tile, scale=HEAD_DIM ** -0.5),
        grid=(b, nblk, s // tile),
        in_specs=[pl.BlockSpec((1, tile, LANES), lambda bi, hp, i: (bi, i, hp)),
                  pl.BlockSpec((1, s, LANES), lambda bi, hp, i: (bi, 0, nblk + hp)),
                  pl.BlockSpec((1, s, LANES), lambda bi, hp, i: (bi, 0, 2 * nblk + hp))],
        out_specs=pl.BlockSpec((1, tile, LANES), lambda bi, hp, i: (bi, i, hp)),
        out_shape=jax.ShapeDtypeStruct((b, s, n_heads * HEAD_DIM), BF16),
        scratch_shapes=[pltpu.VMEM((tile, LANES), F32), pltpu.VMEM((tile, 1), F32)],
        compiler_params=_compiler_params(("arbitrary", "arbitrary", "arbitrary")),
        name="sb_attention",
    )(qkv, qkv, qkv)


def _even_out_kernel(sb_ref, pool_ref, prev_ref, pw_ref, ps_ref, wo_ref, x_ref, g_ref, o_ref,
                     *, tm, tiles_per_seq):
    i = pl.program_id(0)
    t0 = (i % tiles_per_seq) * tm
    t = t0 + lax.broadcasted_iota(I32, (tm, 1), 0)
    s = t0 - tm + lax.broadcasted_iota(I32, (1, 2 * tm), 1)
    chunk_end = ((t // CHUNK) + 1) * CHUNK
    group = LANES
    acc = jnp.dot(sb_ref[...], wo_ref[0:4 * group, :], preferred_element_type=F32)
    for g, w in enumerate(POOL_WINDOWS):
        lo = jnp.maximum(t - w // 2, 0)
        hi = jnp.minimum(t + (w - w // 2), chunk_end)
        band = ((s >= lo) & (s < hi)).astype(BF16)
        cur = pool_ref[:, g * group:(g + 1) * group]
        ext = jnp.concatenate([prev_ref[:, g * group:(g + 1) * group], cur], axis=0)
        window_sum = _dot_exact_lhs(band, ext)
        pooled = window_sum / (hi - lo).astype(F32) - cur
        y = jnp.dot(pooled.astype(BF16), pw_ref[g], preferred_element_type=F32)
        y = y * ps_ref[:, g * group:(g + 1) * group]
        acc += jnp.dot(y.astype(BF16), wo_ref[(4 + g) * group:(5 + g) * group, :],
                       preferred_element_type=F32)
    o_ref[...] = x_ref[...] + g_ref[0] * acc


def _even_out(sb2d, pool2d, seq, pool_w_bf16, pool_scale, w_out_bf16, x2d, gate):
    t, d = x2d.shape
    tm = TOKEN_TILE
    tiles_per_seq = seq // tm
    assert seq % CHUNK == 0 and len(POOL_WINDOWS) * LANES == pool2d.shape[1]
    pw = pool2d.shape[1]
    return pl.pallas_call(
        functools.partial(_even_out_kernel, tm=tm, tiles_per_seq=tiles_per_seq),
        grid=(t // tm,),
        in_specs=[pl.BlockSpec((tm, sb2d.shape[1]), lambda i: (i, 0)),
                  pl.BlockSpec((tm, pw), lambda i: (i, 0)),
                  pl.BlockSpec((tm, pw), lambda i: (jnp.maximum(i - 1, 0), 0)),
                  pl.BlockSpec(pool_w_bf16.shape, lambda i: (0, 0, 0)),
                  pl.BlockSpec((1, pw), lambda i: (0, 0)),
                  pl.BlockSpec((d, d), lambda i: (0, 0)),
                  pl.BlockSpec((tm, d), lambda i: (i, 0)),
                  pl.BlockSpec((1, 1, d), lambda i: (i // tiles_per_seq, 0, 0))],
        out_specs=pl.BlockSpec((tm, d), lambda i: (i, 0)),
        out_shape=jax.ShapeDtypeStruct((t, d), F32),
        compiler_params=_compiler_params(("arbitrary",)),
        name="even_out",
    )(sb2d, pool2d, pool2d, pool_w_bf16, pool_scale.reshape(1, pw), w_out_bf16, x2d, gate)


def _odd_prep_kernel(qk_ref, f_ref, rw_ref, rwprev_ref, hs_ref, tri_ref, ctri_ref,
                     qg_ref, kg_ref, fb_ref, mix_ref, w0_ref, w2_ref, a0_ref, a2_ref, g2_ref,
                     kkp_ref, ka_ref, rk_ref,
                     qn_ref, kn_ref, dcol_ref, drow_ref, rt_ref, kt_ref, at_ref, bt_ref, v_ref,
                     cw_ref, bonus_ref, g_ref, carry_ref, *, tm, tiles_per_seq, width):
    i = pl.program_id(0)
    first = (i % tiles_per_seq) == 0
    hs = hs_ref[...]

    def head_sum(x):
        return _dot_exact_rhs(x, hs, 2)

    inv_hd = 1.0 / HEAD_DIM
    q = qk_ref[:, :width]
    k = qk_ref[:, width:]
    qn_ref[...] = (q * lax.rsqrt(head_sum(q * q) * inv_hd + NORM_EPS) * qg_ref[...]).astype(qn_ref.dtype)
    kn_ref[...] = (k * lax.rsqrt(head_sum(k * k) * inv_hd + NORM_EPS) * kg_ref[...]).astype(kn_ref.dtype)

    log_f = _log_sigmoid(f_ref[...] + fb_ref[...])

    @pl.when(first)
    def _():
        carry_ref[...] = jnp.zeros_like(carry_ref)

    dcum = _dot_exact_lhs(tri_ref[...], log_f) + carry_ref[...]
    carry_ref[...] = dcum[tm - 1:tm, :]
    dcol_ref[...] = dcum
    drow_ref[0, 0] = jnp.transpose(dcum)[:SUBLANES, :]

    pd = rw_ref[...]
    prev_row = jnp.where(first, 0.0, rwprev_ref[SUBLANES - 1:SUBLANES, :])
    row0 = lax.broadcasted_iota(I32, (tm, 1), 0) == 0
    prev = jnp.where(row0, prev_row, pltpu.roll(pd, 1, axis=0))
    pd = pd + (prev - pd) * mix_ref[...]
    r = pd[:, :width]
    kr = pd[:, width:2 * width]
    vr = pd[:, 2 * width:3 * width]
    lora_in = pd[:, 3 * width:3 * width + LANES]
    xg = pd[:, 3 * width + LANES:3 * width + 2 * LANES]
    w_log = _log_sigmoid(w0_ref[...] + jnp.dot(jnp.tanh(lora_in).astype(BF16), w2_ref[...],
                                              preferred_element_type=F32)) - 0.5
    log_w = -jnp.exp(w_log)
    a = _sigmoid(a0_ref[...] + jnp.dot(lora_in.astype(BF16), a2_ref[...], preferred_element_type=F32))
    g_ref[...] = jnp.dot(_sigmoid(xg).astype(BF16), g2_ref[...], preferred_element_type=F32)
    kk = kr * kkp_ref[...]
    kk = kk / jnp.maximum(jnp.sqrt(head_sum(kk * kk)), 1e-12)
    k2 = kr * (1.0 + (a - 1.0) * ka_ref[...])
    bonus_ref[...] = head_sum(r * k2 * rk_ref[...]) * vr
    cw = _dot_exact_lhs(ctri_ref[...], log_w)
    grow = jnp.exp(-cw)
    rt_ref[...] = r * jnp.exp(cw)
    kt_ref[...] = k2 * grow
    bt_ref[...] = kk * a * grow
    at_ref[...] = -kk * jnp.exp(cw - log_w)
    v_ref[...] = vr
    cw_ref[...] = cw


def _odd_prep(qk, fproj, rw, seq, p):
    t = qk.shape[0]
    width = qk.shape[1] // 2
    tm = TOKEN_TILE
    tiles_per_seq = seq // tm
    nrw = rw.shape[1]
    assert tm % CHUNK == 0 and seq % tm == 0 and nrw == 3 * width + 2 * LANES
    ri = lax.broadcasted_iota(I32, (tm, tm), 0)
    ci = lax.broadcasted_iota(I32, (tm, tm), 1)
    tri = (ci <= ri).astype(BF16)
    ctri = ((ci <= ri) & (ci // CHUNK == ri // CHUNK)).astype(BF16)
    hs = _head_sum_matrix(width)
    row = lambda a: a.reshape(1, -1)
    full = lambda a: pl.BlockSpec(a.shape, lambda i: (0,) * a.ndim)
    tile = lambda n: pl.BlockSpec((tm, n), lambda i: (i, 0))
    consts = [hs, tri, ctri, row(p['qg']), row(p['kg']), row(p['fb']), row(p['mix']), row(p['w0']),
              p['w2'], row(p['a0']), p['a2'], p['g2'], row(p['kk']), row(p['ka']), row(p['rk'])]
    wide = jax.ShapeDtypeStruct((t, width), F32)
    return pl.pallas_call(
        functools.partial(_odd_prep_kernel, tm=tm, tiles_per_seq=tiles_per_seq, width=width),
        grid=(t // tm,),
        in_specs=[tile(2 * width), tile(LANES), tile(nrw),
                  pl.BlockSpec((SUBLANES, nrw), lambda i: (jnp.maximum(i * (tm // SUBLANES) - 1, 0), 0))]
                 + [full(c) for c in consts],
        out_specs=[tile(width), tile(width), tile(LANES),
                   pl.BlockSpec((1, 1, SUBLANES, tm), lambda i: (i // tiles_per_seq, i % tiles_per_seq, 0, 0))]
                  + [tile(width)] * 8,
        out_shape=[jax.ShapeDtypeStruct((t, width), BF16), jax.ShapeDtypeStruct((t, width), BF16),
                   jax.ShapeDtypeStruct((t, LANES), F32),
                   jax.ShapeDtypeStruct((t // seq, tiles_per_seq, SUBLANES, tm), F32)] + [wide] * 8,
        scratch_shapes=[pltpu.VMEM((1, LANES), F32)],
        compiler_params=_compiler_params(("arbitrary",)),
        name="odd_prep",
    )(qk, fproj, rw, rw, *consts)


def _fox_attention_kernel(bound_ref, q_ref, k_ref, v_ref, dcol_ref, drow_ref, o_ref, acc_ref, m_ref,
                          l_ref, *, tile, scale):
    hp = pl.program_id(1)
    i = pl.program_id(2)
    qk_bound = bound_ref[0, 0]
    lane = lax.broadcasted_iota(I32, (1, LANES), 1)
    row = lax.broadcasted_iota(I32, (tile, tile), 0)
    col = lax.broadcasted_iota(I32, (tile, tile), 1)
    causal = col <= row
    sub = lax.broadcasted_iota(I32, (SUBLANES, tile), 0)
    q = q_ref[0]
    dq_all = dcol_ref[0]
    outs = []
    for h in range(HEADS_PER_LANE_BLOCK):
        head = hp * HEADS_PER_LANE_BLOCK + h
        qh = jnp.where((lane >= HEAD_DIM * h) & (lane < HEAD_DIM * (h + 1)), q, jnp.zeros_like(q))
        dq = jnp.sum(jnp.where(lane == head, dq_all, 0.0), axis=1, keepdims=True)

        def block(j, diag, qh=qh, dq=dq, head=head):
            start = pl.multiple_of(j * tile, tile)
            kb = k_ref[0, pl.ds(start, tile), :]
            vb = v_ref[0, pl.ds(start, tile), :]
            dk = jnp.sum(jnp.where(sub == head, drow_ref[0, j], 0.0), axis=0, keepdims=True)
            s = lax.dot_general(qh, kb, (((1,), (1,)), ((), ())), preferred_element_type=F32) * scale
            s = s + dq - dk
            if diag:
                s = jnp.where(causal, s, -jnp.inf)
            m_prev = m_ref[...]
            m_new = jnp.maximum(m_prev, jnp.max(s, axis=1, keepdims=True))
            p = jnp.exp(s - m_new)
            alpha = jnp.exp(m_prev - m_new)
            l_ref[...] = alpha * l_ref[...] + jnp.sum(p, axis=1, keepdims=True)
            acc_ref[...] = alpha * acc_ref[...] + jnp.dot(p.astype(BF16), vb, preferred_element_type=F32)
            m_ref[...] = m_new

        acc_ref[...] = jnp.zeros_like(acc_ref)
        l_ref[...] = jnp.zeros_like(l_ref)
        m_ref[...] = jnp.full_like(m_ref, -jnp.inf)
        block(i, True)

        def alive(j, dq=dq, head=head):
            dk = jnp.sum(jnp.where(sub == head, drow_ref[0, jnp.maximum(j, 0)], 0.0), axis=0, keepdims=True)
            return qk_bound + jnp.max(dq - m_ref[...]) - jnp.min(dk) > F32_EXP_UNDERFLOW

        def cond(c):
            return jnp.logical_and(c[0] < i, c[1])

        def body(c):
            j = i - 1 - c[0]
            block(j, False)
            return c[0] + 1, alive(j - 1)

        lax.while_loop(cond, body, (jnp.int32(0), alive(i - 1)))
        outs.append(acc_ref[...] / l_ref[...])
    o_ref[0] = jnp.where(lane < HEAD_DIM, outs[0], outs[1])


def _fox_attention(qn, kn, v, dcol, drow, q_gain, k_gain):
    b, s, w = qn.shape
    nblk = w // LANES
    tile = ATTN_TILE
    assert drow.shape == (b, s // tile, SUBLANES, tile) and w // HEAD_DIM <= SUBLANES
    scale = HEAD_DIM ** -0.5
    bf16_slack = (1.0 + 2.0 ** -8) ** 2
    bound = (HEAD_DIM * scale * bf16_slack) * jnp.max(jnp.abs(q_gain)) * jnp.max(jnp.abs(k_gain))
    return pl.pallas_call(
        functools.partial(_fox_attention_kernel, tile=tile, scale=scale),
        grid=(b, nblk, s // tile),
        in_specs=[pl.BlockSpec(memory_space=pltpu.SMEM),
                  pl.BlockSpec((1, tile, LANES), lambda bi, hp, i: (bi, i, hp)),
                  pl.BlockSpec((1, s, LANES), lambda bi, hp, i: (bi, 0, hp)),
                  pl.BlockSpec((1, s, LANES), lambda bi, hp, i: (bi, 0, hp)),
                  pl.BlockSpec((1, tile, LANES), lambda bi, hp, i: (bi, i, 0)),
                  pl.BlockSpec((1, s // tile, SUBLANES, tile), lambda bi, hp, i: (bi, 0, 0, 0))],
        out_specs=pl.BlockSpec((1, tile, LANES), lambda bi, hp, i: (bi, i, hp)),
        out_shape=jax.ShapeDtypeStruct((b, s, w), F32),
        scratch_shapes=[pltpu.VMEM((tile, LANES), F32), pltpu.VMEM((tile, 1), F32),
                        pltpu.VMEM((tile, 1), F32)],
        compiler_params=_compiler_params(("arbitrary", "arbitrary", "arbitrary")),
        name="fox_attention",
    )(bound.reshape(1, 1).astype(F32), qn, kn, v, dcol, drow)


_NN = (((1,), (0,)), ((), ()))
_NT = (((1,), (1,)), ((), ()))
_TN = (((0,), (0,)), ((), ()))


def _mm(a, b, dims=_NN):
    a_hi, a_lo = _split_bf16(a, 2)
    b_hi, b_lo = _split_bf16(b, 2)
    dot = lambda x, y: lax.dot_general(x, y, dims, preferred_element_type=F32)
    return dot(a_hi, b_hi) + (dot(a_hi, b_lo) + dot(a_lo, b_hi))


def _rwkv_scan_kernel(rt_ref, kt_ref, at_ref, bt_ref, v_ref, cw_ref, y_ref, st_ref,
                      *, steps_per_seq, chunks, heads):
    i = pl.program_id(0)

    @pl.when(i % steps_per_seq == 0)
    def _():
        st_ref[...] = jnp.zeros_like(st_ref)

    c = CHUNK
    ri = lax.broadcasted_iota(I32, (2 * c, 2 * c), 0)
    ci = lax.broadcasted_iota(I32, (2 * c, 2 * c), 1)
    keep = (ci % c) < jnp.where(ri < c, ri, ri - c + 1)
    eye = (lax.broadcasted_iota(I32, (c, c), 0) == lax.broadcasted_iota(I32, (c, c), 1)).astype(F32)

    def chunk(cidx, carry):
        rows = pl.ds(pl.multiple_of(cidx * c, c), c)
        last = pl.ds(pl.multiple_of(cidx * c, c) + (c - SUBLANES), SUBLANES)
        ys = []
        for h in range(heads):
            cols = slice(h * HEAD_DIM, (h + 1) * HEAD_DIM)
            ar = jnp.concatenate([at_ref[rows, cols], rt_ref[rows, cols]], axis=0)
            bk = jnp.concatenate([bt_ref[rows, cols], kt_ref[rows, cols]], axis=0)
            vv = v_ref[rows, cols]
            decay = jnp.exp(cw_ref[last, cols][SUBLANES - 1:SUBLANES, :])
            st = st_ref[h]
            a_all = jnp.where(keep, _mm(ar, bk, _NT), 0.0)
            a_ab = a_all[:c, :c]
            inv = eye + a_ab
            power = _mm(a_ab, a_ab)
            for _ in range(4):
                both = _mm(jnp.concatenate([inv, power], axis=0), power)
                inv = inv + both[:c]
                power = both[c:]
            inv = inv + _mm(inv, power)
            from_state = _mm(ar, st, _NT)
            from_v = _mm(a_all[:, c:], vv)
            u = _mm(inv, from_state[:c] + from_v[:c])
            ys.append(from_state[c:] + from_v[c:] + _mm(a_all[c:, :c], u))
            st_ref[h] = (st + _mm(jnp.concatenate([u, vv], axis=0), bk, _TN)) * decay
        y_ref[rows, :] = jnp.concatenate(ys, axis=1)
        return carry

    lax.fori_loop(0, chunks, chunk, 0)


def _rwkv_scan(rt, kt, at, bt, v, cw, seq):
    t, w = rt.shape
    heads = w // HEAD_DIM
    ts = 2 * TOKEN_TILE
    assert seq % ts == 0
    spec = pl.BlockSpec((ts, w), lambda i: (i, 0))
    return pl.pallas_call(
        functools.partial(_rwkv_scan_kernel, steps_per_seq=seq // ts, chunks=ts // CHUNK, heads=heads),
        grid=(t // ts,),
        in_specs=[spec] * 6,
        out_specs=spec,
        out_shape=jax.ShapeDtypeStruct((t, w), F32),
        scratch_shapes=[pltpu.VMEM((heads, HEAD_DIM, HEAD_DIM), F32)],
        compiler_params=_compiler_params(("arbitrary",)),
        name="rwkv_scan",
    )(rt, kt, at, bt, v, cw)


def _odd_out_kernel(att_ref, og_ref, y_ref, bonus_ref, g_ref, hs_ref, lnw_ref, lnb_ref, wo_ref,
                    x_ref, gate_ref, o_ref, *, width):
    hs = hs_ref[...]
    inv_hd = 1.0 / HEAD_DIM
    fox = att_ref[...] * _sigmoid(og_ref[...])
    y = y_ref[...]
    mu = _dot_exact_rhs(y, hs, 2) * inv_hd
    yc = y - mu
    var = _dot_exact_rhs(yc * yc, hs, 2) * inv_hd
    yn = yc * lax.rsqrt(var + RWKV_GN_EPS) * lnw_ref[...] + lnb_ref[...]
    rw = (yn + bonus_ref[...]) * g_ref[...]
    acc = jnp.dot(fox.astype(BF16), wo_ref[:width, :], preferred_element_type=F32)
    acc += jnp.dot(rw.astype(BF16), wo_ref[width:, :], preferred_element_type=F32)
    o_ref[...] = x_ref[...] + gate_ref[0] * acc


def _odd_out(att, og, y, bonus, g, ln_w, ln_b, w_out_bf16, x2d, seq, gate):
    t, d = x2d.shape
    width = att.shape[1]
    tm = TOKEN_TILE
    tiles_per_seq = seq // tm
    hs = _head_sum_matrix(width)
    tile = lambda n: pl.BlockSpec((tm, n), lambda i: (i, 0))
    return pl.pallas_call(
        functools.partial(_odd_out_kernel, width=width),
        grid=(t // tm,),
        in_specs=[tile(width)] * 5
                 + [pl.BlockSpec(hs.shape, lambda i: (0, 0)),
                    pl.BlockSpec((1, width), lambda i: (0, 0)), pl.BlockSpec((1, width), lambda i: (0, 0)),
                    pl.BlockSpec((d, d), lambda i: (0, 0)), tile(d),
                    pl.BlockSpec((1, 1, d), lambda i: (i // tiles_per_seq, 0, 0))],
        out_specs=tile(d),
        out_shape=jax.ShapeDtypeStruct((t, d), F32),
        compiler_params=_compiler_params(("arbitrary",)),
        name="odd_out",
    )(att, og, y, bonus, g, hs, ln_w.reshape(1, width), ln_b.reshape(1, width), w_out_bf16, x2d, gate)


def _top_rows(s, k, prio, payload=None):
    big = jnp.int32(2 ** 30)
    vals, picks = [], []
    for _ in range(k):
        m = jnp.max(s, axis=0, keepdims=True)
        am = jnp.min(jnp.where(s == m, prio, big), axis=0, keepdims=True)
        hit = prio == am
        vals.append(m)
        if payload is None:
            picks.append(am)
        else:
            picks.append(jnp.max(jnp.where(hit, payload, -1), axis=0, keepdims=True))
        s = jnp.where(hit, -jnp.inf, s)
    return jnp.concatenate(vals, axis=0), jnp.concatenate(picks, axis=0)


def _staircase_groups():
    k, g = PEER_TOPK, SUBLANES
    groups = []
    for b in range(k):
        count = k // (b + 1)
        if count <= 1:
            break
        for a0 in range(0, count, g):
            groups.append((a0, b, False, min(g, count - a0)))
    first_single = next(b for b in range(k) if k // (b + 1) <= 1)
    for b0 in range(first_single, k, g):
        groups.append((0, b0, True, min(g, k - b0)))
    return groups


def _peer_topk_kernel(q_ref, k1_ref, k2_ref, idx_ref, gate_ref):
    half = PEER_NKEYS
    nt = (((1,), (1,)), ((), ()))
    tt = q_ref.shape[0]
    key_id = lax.broadcasted_iota(I32, (half, tt), 0)
    sub = lax.broadcasted_iota(I32, (SUBLANES, 1), 0)
    groups = _staircase_groups()
    prio = jnp.concatenate(
        [jnp.broadcast_to((sub * 0 + a0) * PEER_TOPK + b0 + sub if along_b
                          else (a0 + sub) * PEER_TOPK + b0, (SUBLANES, tt))
         for a0, b0, along_b, _ in groups], axis=0)
    idx_rows = []
    for h in range(PEER_HEADS):
        q1 = q_ref[:, (2 * h) * half:(2 * h + 1) * half]
        q2 = q_ref[:, (2 * h + 1) * half:(2 * h + 2) * half]
        s1 = lax.dot_general(k1_ref[...], q1, nt, precision=HIGHEST, preferred_element_type=F32)
        s2 = lax.dot_general(k2_ref[...], q2, nt, precision=HIGHEST, preferred_element_type=F32)
        v1, i1 = _top_rows(s1, PEER_TOPK, key_id)
        v2, i2 = _top_rows(s2, PEER_TOPK, key_id)
        cand, cand_idx = [], []
        for a0, b0, along_b, valid in groups:
            if along_b:
                val = v1[a0:a0 + 1] + v2[b0:b0 + SUBLANES]
                eid = i1[a0:a0 + 1] * PEER_NKEYS + i2[b0:b0 + SUBLANES]
            else:
                val = v1[a0:a0 + SUBLANES] + v2[b0:b0 + 1]
                eid = i1[a0:a0 + SUBLANES] * PEER_NKEYS + i2[b0:b0 + 1]
            cand.append(jnp.where(sub < valid, val, -jnp.inf))
            cand_idx.append(eid)
        top_s, idx = _top_rows(jnp.concatenate(cand, axis=0), PEER_TOPK, prio,
                               payload=jnp.concatenate(cand_idx, axis=0))
        e = jnp.exp(top_s - top_s[0:1])
        gate_ref[h * PEER_TOPK:(h + 1) * PEER_TOPK, :] = e / jnp.sum(e, axis=0, keepdims=True)
        idx_rows.append(idx)
    idx_ref[...] = jnp.transpose(jnp.concatenate(idx_rows, axis=0))


def _peer_topk(q2d, k1, k2):
    t, n = q2d.shape
    tt = TOKEN_TILE
    assert PEER_TOPK % SUBLANES == 0
    return pl.pallas_call(
        _peer_topk_kernel,
        grid=(t // tt,),
        in_specs=[pl.BlockSpec((tt, n), lambda i: (i, 0)),
                  pl.BlockSpec(k1.shape, lambda i: (0, 0)),
                  pl.BlockSpec(k2.shape, lambda i: (0, 0))],
        out_specs=[pl.BlockSpec((tt, PEER_SLOTS), lambda i: (i, 0)),
                   pl.BlockSpec((PEER_SLOTS, tt), lambda i: (0, i))],
        out_shape=[jax.ShapeDtypeStruct((t, PEER_SLOTS), I32),
                   jax.ShapeDtypeStruct((PEER_SLOTS, t), F32)],
        compiler_params=_compiler_params(("arbitrary",)),
        name="peer_topk",
    )(q2d, k1, k2)


def _pack_table(table):
    e, d = table.shape
    pairs = d // (2 * LANES)
    t = table.astype(BF16).reshape(e, pairs, 2, LANES).transpose(0, 1, 3, 2)
    words = lax.bitcast_convert_type(t, jnp.uint32)
    return lax.bitcast_convert_type(words, I32).reshape(e * pairs, LANES)


def _unpack_words(w):
    lo = pltpu.bitcast(lax.shift_left(w, 16), F32)
    hi = pltpu.bitcast(lax.bitwise_and(w, jnp.int32(-65536)), F32)
    return lo, hi


def _gather_rows(idx_ref, tab_ref, tile_ref, tok, pairs):
    for m in range(PEER_SLOTS):
        e = idx_ref[tok, m]
        tile_ref[m * pairs:(m + 1) * pairs, :] = tab_ref[pl.ds(pl.multiple_of(e * pairs, pairs), pairs), :]


def _tile_chunk(tile_ref, j, pairs):
    return _unpack_words(tile_ref[pl.ds(j, PEER_SLOTS, stride=pairs), :])


def _pipelined_tokens(ntok, gather, compute, init):
    gather(0, 0)

    def pair(p, carry):
        tok = 2 * p
        gather(tok + 1, 1)
        carry = compute(tok, 0, carry)
        gather(jnp.minimum(tok + 2, ntok - 1), 0)
        return compute(tok + 1, 1, carry)

    return lax.fori_loop(0, ntok // 2, pair, init)


def _pick_row(block, r):
    sub = lax.broadcasted_iota(I32, block.shape, 0)
    return jnp.sum(jnp.where(sub == r, block, 0.0), axis=0, keepdims=True)


def _peer_act_kernel(idx_ref, tab_ref, h_ref, gate_ref, coef_ref, tile0_ref, tile1_ref, *, pairs):
    ntok = h_ref.shape[0]
    lane = lax.broadcasted_iota(I32, (PEER_SLOTS, ntok), 1)
    tiles = (tile0_ref, tile1_ref)

    def gather(tok, buf):
        _gather_rows(idx_ref, tab_ref, tiles[buf], tok, pairs)

    def compute(tok, buf, act_t):
        base = pl.multiple_of((tok // SUBLANES) * SUBLANES, SUBLANES)
        h_row = _pick_row(h_ref[pl.ds(base, SUBLANES), :], tok % SUBLANES)
        r = jnp.zeros((PEER_SLOTS, LANES), F32)
        for j in range(pairs):
            lo, hi = _tile_chunk(tiles[buf], j, pairs)
            h_lo = h_row[:, (2 * j) * LANES:(2 * j + 1) * LANES]
            h_hi = h_row[:, (2 * j + 1) * LANES:(2 * j + 2) * LANES]
            r = r + lo * h_lo + hi * h_hi
        col = jnp.sum(r, axis=1, keepdims=True)
        return jnp.where(lane == tok, col, act_t)

    act_t = _pipelined_tokens(ntok, gather, compute, jnp.zeros((PEER_SLOTS, ntok), F32))
    gelu = 0.5 * act_t * (1.0 + lax.erf(act_t * (2.0 ** -0.5)))
    coef_ref[...] = gate_ref[...] * gelu


def _peer_act(idx, table_words, h2d, gate_t):
    t, d = h2d.shape
    pairs = d // (2 * LANES)
    nt = PEER_TOKENS
    return pl.pallas_call(
        functools.partial(_peer_act_kernel, pairs=pairs),
        grid=(t // nt,),
        in_specs=[pl.BlockSpec((nt, PEER_SLOTS), lambda i: (i, 0), memory_space=pltpu.SMEM),
                  pl.BlockSpec(table_words.shape, lambda i: (0, 0), pipeline_mode=pl.Buffered(1)),
                  pl.BlockSpec((nt, d), lambda i: (i, 0)),
                  pl.BlockSpec((PEER_SLOTS, nt), lambda i: (0, i))],
        out_specs=pl.BlockSpec((PEER_SLOTS, nt), lambda i: (0, i)),
        out_shape=jax.ShapeDtypeStruct((PEER_SLOTS, t), F32),
        scratch_shapes=[pltpu.VMEM((pairs * PEER_SLOTS, LANES), I32)] * 2,
        compiler_params=_compiler_params(("arbitrary",), vmem_mb=56),
        name="peer_act",
    )(idx, table_words, h2d, gate_t)


def _peer_out_kernel(idx_ref, tab_ref, coef_ref, x_ref, g_ref, o_ref, tile0_ref, tile1_ref, *, pairs):
    ntok, d = x_ref.shape
    lane = lax.broadcasted_iota(I32, (PEER_SLOTS, ntok), 1)
    sub = lax.broadcasted_iota(I32, (SUBLANES, d), 0)
    groups = PEER_SLOTS // SUBLANES
    tiles = (tile0_ref, tile1_ref)

    def gather(tok, buf):
        _gather_rows(idx_ref, tab_ref, tiles[buf], tok, pairs)

    def compute(tok, buf, rows):
        coef = jnp.sum(jnp.where(lane == tok, coef_ref[...], 0.0), axis=1, keepdims=True)
        pieces = []
        for j in range(pairs):
            for vals in _tile_chunk(tiles[buf], j, pairs):
                prod = (vals * coef).reshape(groups, SUBLANES, LANES)
                pieces.append(jnp.sum(jnp.sum(prod, axis=0), axis=0, keepdims=True))
        row = jnp.concatenate(pieces, axis=1)
        rows = jnp.where(sub == tok % SUBLANES, row, rows)
        base = pl.multiple_of((tok // SUBLANES) * SUBLANES, SUBLANES)
        o_ref[pl.ds(base, SUBLANES), :] = x_ref[pl.ds(base, SUBLANES), :] + g_ref[0] * rows
        return rows

    _pipelined_tokens(ntok, gather, compute, jnp.zeros((SUBLANES, d), F32))


def _peer_out(idx, table_words, coef_t, x2d, seq, gate):
    t, d = x2d.shape
    pairs = d // (2 * LANES)
    nt = PEER_TOKENS
    per_seq = seq // nt
    return pl.pallas_call(
        functools.partial(_peer_out_kernel, pairs=pairs),
        grid=(t // nt,),
        in_specs=[pl.BlockSpec((nt, PEER_SLOTS), lambda i: (i, 0), memory_space=pltpu.SMEM),
                  pl.BlockSpec(table_words.shape, lambda i: (0, 0), pipeline_mode=pl.Buffered(1)),
                  pl.BlockSpec((PEER_SLOTS, nt), lambda i: (0, i)),
                  pl.BlockSpec((nt, d), lambda i: (i, 0)),
                  pl.BlockSpec((1, 1, d), lambda i: (i // per_seq, 0, 0))],
        out_specs=pl.BlockSpec((nt, d), lambda i: (i, 0)),
        out_shape=jax.ShapeDtypeStruct((t, d), F32),
        scratch_shapes=[pltpu.VMEM((pairs * PEER_SLOTS, LANES), I32)] * 2,
        compiler_params=_compiler_params(("arbitrary",), vmem_mb=56),
        name="peer_out",
    )(idx, table_words, coef_t, x2d, gate)


def _peer_ffn(x2d, seq, gain, shift, scale, gate, wq_bf16, k1, k2, u_words, v_words):
    n = wq_bf16.shape[1]
    q, h = _norm_proj(x2d, seq, gain, shift, scale, wq_bf16, [(0, n)], [F32], emit_h=True)
    idx, gate_t = _peer_topk(q, k1, k2)
    coef_t = _peer_act(idx, u_words, h, gate_t)
    return _peer_out(idx, v_words, coef_t, x2d, seq, gate)


def _final_norm_kernel(x_ref, gain_ref, o_ref):
    x = x_ref[...]
    ms = jnp.mean(x * x, axis=-1, keepdims=True)
    o_ref[...] = x * lax.rsqrt(ms + NORM_EPS) * gain_ref[...]


def _final_norm(x2d, gain):
    t, d = x2d.shape
    tm = TOKEN_TILE
    return pl.pallas_call(
        _final_norm_kernel,
        grid=(t // tm,),
        in_specs=[pl.BlockSpec((tm, d), lambda i: (i, 0)), pl.BlockSpec((1, d), lambda i: (0, 0))],
        out_specs=pl.BlockSpec((tm, d), lambda i: (i, 0)),
        out_shape=jax.ShapeDtypeStruct((t, d), F32),
        compiler_params=_compiler_params(("arbitrary",)),
        name="final_norm",
    )(x2d, gain.reshape(1, d))


def _even_mixer(x2d, b, seq, gain, shift, scale, gate, w_in, w_out, pool_w, pool_scale):
    d = x2d.shape[1]
    pool_width = len(POOL_WINDOWS) * LANES
    sb_width = (w_in.shape[1] - pool_width) // 3
    qkv, pool = _norm_proj(x2d, seq, gain, shift, scale, w_in.astype(BF16),
                           [(0, 3 * sb_width), (3 * sb_width, pool_width)], [BF16, F32])
    sb = _sb_attention(qkv.reshape(b, seq, 3 * sb_width), sb_width // HEAD_DIM)
    return _even_out(sb.reshape(b * seq, sb_width), pool, seq, pool_w.astype(BF16), pool_scale,
                     w_out.astype(BF16), x2d, gate)


def _odd_mixer(x2d, b, seq, gain, shift, scale, gate, w_in, w_out, fox_qnorm, fox_knorm, fox_fbias,
               rwkv_mix, rwkv_w0, rwkv_w2, rwkv_a0, rwkv_a2, rwkv_g2, rwkv_kk, rwkv_ka, rwkv_rk,
               rwkv_ln_w, rwkv_ln_b):
    d = x2d.shape[1]
    width = rwkv_w0.shape[0]
    heads = width // HEAD_DIM
    fox_in = 4 * width + heads
    n_rw = w_in.shape[1] - fox_in
    w_cols = jnp.concatenate([w_in[:, :4 * width], w_in[:, fox_in:], w_in[:, 4 * width:fox_in],
                              jnp.zeros((d, LANES - heads), w_in.dtype)], axis=1).astype(BF16)
    qk, v, og, rw, fproj = _norm_proj(
        x2d, seq, gain, shift, scale, w_cols,
        [(0, 2 * width), (2 * width, width), (3 * width, width), (4 * width, n_rw), (4 * width + n_rw, LANES)],
        [F32, BF16, F32, F32, F32])
    lora = rwkv_w2.shape[0]
    assert 2 * lora == LANES and rwkv_a2.shape[0] == lora and rwkv_g2.shape[0] == LANES
    zeros = jnp.zeros((lora, width), F32)
    params = dict(
        qg=jnp.tile(fox_qnorm, heads), kg=jnp.tile(fox_knorm, heads),
        fb=jnp.concatenate([fox_fbias, jnp.zeros((LANES - heads,), F32)]),
        mix=rwkv_mix, w0=rwkv_w0, a0=rwkv_a0,
        w2=jnp.concatenate([rwkv_w2, zeros], axis=0).astype(BF16),
        a2=jnp.concatenate([zeros, rwkv_a2], axis=0).astype(BF16),
        g2=rwkv_g2.astype(BF16), kk=rwkv_kk, ka=rwkv_ka, rk=rwkv_rk.reshape(-1))
    qn, kn, dcol, drow, rt, kt, at, bt, vv, cw, bonus, g = _odd_prep(qk, fproj, rw, seq, params)
    r3 = lambda a: a.reshape(b, seq, -1)
    att = _fox_attention(r3(qn), r3(kn), r3(v), r3(dcol), drow, fox_qnorm, fox_knorm)
    y = _rwkv_scan(rt, kt, at, bt, vv, cw, seq)
    return _odd_out(att.reshape(b * seq, width), og, y, bonus, g, rwkv_ln_w, rwkv_ln_b,
                    w_out.astype(BF16), x2d, seq, gate)


def kernel(x, c, ada_w, ada_b, norm_mix, norm_ffn, ev_w_in, ev_w_out, pool_w, pool_scale, od_w_in, od_w_out, fox_qnorm, fox_knorm, fox_fbias, rwkv_mix, rwkv_w0, rwkv_w2, rwkv_a0, rwkv_a2, rwkv_g2, rwkv_kk, rwkv_ka, rwkv_rk, rwkv_ln_w, rwkv_ln_b, peer_wq, peer_k1, peer_k2, peer_u, peer_v, final_norm):
    b, seq, d = x.shape
    depth = ada_w.shape[0]
    assert TOKEN_TILE == ATTN_TILE and seq % (2 * TOKEN_TILE) == 0
    mod = _adaln(c, ada_w, ada_b).reshape(depth, b, 6, 1, d)
    x2d = x.reshape(b * seq, d)
    for layer in range(depth):
        sh1, sc1, g1, sh2, sc2, g2 = (mod[layer, :, i] for i in range(6))
        j = layer // 2
        if layer % 2 == 0:
            x2d = _even_mixer(x2d, b, seq, norm_mix[layer], sh1, sc1, g1, ev_w_in[j], ev_w_out[j],
                              pool_w[j], pool_scale[j])
        else:
            x2d = _odd_mixer(x2d, b, seq, norm_mix[layer], sh1, sc1, g1, od_w_in[j], od_w_out[j],
                             fox_qnorm[j], fox_knorm[j], fox_fbias[j], rwkv_mix[j], rwkv_w0[j],
                             rwkv_w2[j], rwkv_a0[j], rwkv_a2[j], rwkv_g2[j], rwkv_kk[j], rwkv_ka[j],
                             rwkv_rk[j], rwkv_ln_w[j], rwkv_ln_b[j])
        x2d = _peer_ffn(x2d, seq, norm_ffn[layer], sh2, sc2, g2, peer_wq[layer].astype(BF16),
                        peer_k1[layer], peer_k2[layer], _pack_table(peer_u[layer]),
                        _pack_table(peer_v[layer]))
    return _final_norm(x2d, final_norm).reshape(b, seq, d)
```

```python
import functools

import jax
import jax.numpy as jnp
from jax import lax
from jax.experimental import pallas as pl
from jax.experimental.pallas import tpu as pltpu

F32 = jnp.float32
BF16 = jnp.bfloat16
I32 = jnp.int32
HIGHEST = lax.Precision.HIGHEST

LANES = 128
SUBLANES = 8
VMEM_BYTES_V7X = 64 * 1024 * 1024

HEAD_DIM = 64
HEADS_PER_LANE_BLOCK = LANES // HEAD_DIM
NORM_EPS = 1e-6
RWKV_GN_EPS = 64e-5
CHUNK = 64
POOL_WINDOWS = (2, 4, 8, 16)
PEER_TOPK = 16
PEER_NKEYS = 128
PEER_HEADS = 8
PEER_SLOTS = PEER_HEADS * PEER_TOPK

TOKEN_TILE = 256
ATTN_TILE = 256
PEER_TOKENS = 128

F32_EXP_UNDERFLOW = -104.0


def _compiler_params(semantics, vmem_mb=48):
    return pltpu.CompilerParams(dimension_semantics=semantics,
                                vmem_limit_bytes=min(vmem_mb * 1024 * 1024, VMEM_BYTES_V7X))


def _split_bf16(x, n):
    parts = []
    r = x
    for _ in range(n):
        p = r.astype(BF16)
        parts.append(p)
        r = r - p.astype(F32)
    return parts


def _dot_exact_rhs(x, m, n=3):
    acc = None
    for p in _split_bf16(x, n):
        t = jnp.dot(p, m, preferred_element_type=F32)
        acc = t if acc is None else acc + t
    return acc


def _dot_exact_lhs(m, x, n=3):
    acc = None
    for p in _split_bf16(x, n):
        t = jnp.dot(m, p, preferred_element_type=F32)
        acc = t if acc is None else acc + t
    return acc


def _log_sigmoid(z):
    return jnp.minimum(z, 0.0) - jnp.log1p(jnp.exp(-jnp.abs(z)))


def _sigmoid(z):
    return 1.0 / (1.0 + jnp.exp(-z))


def _head_sum_matrix(width):
    r = lax.broadcasted_iota(I32, (width, width), 0) // HEAD_DIM
    c = lax.broadcasted_iota(I32, (width, width), 1) // HEAD_DIM
    return (r == c).astype(BF16)


def _adaln_kernel(c_ref, w_ref, b_ref, o_ref):
    c = c_ref[...]
    c_act = c * _sigmoid(c)
    o_ref[0] = jnp.dot(c_act, w_ref[0], precision=HIGHEST, preferred_element_type=F32) + b_ref[0]


def _adaln(c, ada_w, ada_b):
    b, d = c.shape
    depth, _, n = ada_w.shape
    tn = 1536
    assert n % tn == 0 and b <= SUBLANES
    c_pad = jnp.zeros((SUBLANES, d), F32).at[:b].set(c)
    out = pl.pallas_call(
        _adaln_kernel,
        grid=(depth, n // tn),
        in_specs=[pl.BlockSpec((SUBLANES, d), lambda l, j: (0, 0)),
                  pl.BlockSpec((1, d, tn), lambda l, j: (l, 0, j)),
                  pl.BlockSpec((1, 1, tn), lambda l, j: (l, 0, j))],
        out_specs=pl.BlockSpec((1, SUBLANES, tn), lambda l, j: (l, 0, j)),
        out_shape=jax.ShapeDtypeStruct((depth, SUBLANES, n), F32),
        compiler_params=_compiler_params(("arbitrary", "arbitrary")),
        name="adaln",
    )(c_pad, ada_w, ada_b.reshape(depth, 1, n))
    return out[:, :b, :]


def _norm_proj_kernel(x_ref, gain_ref, shift_ref, scale_ref, w_ref, *out_refs, groups, emit_h):
    x = x_ref[...]
    ms = jnp.mean(x * x, axis=-1, keepdims=True)
    y = x * lax.rsqrt(ms + NORM_EPS) * gain_ref[...]
    h = y * (1.0 + scale_ref[0]) + shift_ref[0]
    hb = h.astype(BF16)
    for (start, width), o_ref in zip(groups, out_refs):
        o_ref[...] = jnp.dot(hb, w_ref[:, start:start + width],
                             preferred_element_type=F32).astype(o_ref.dtype)
    if emit_h:
        out_refs[-1][...] = h


def _norm_proj(x2d, seq, gain, shift, scale, w_bf16, groups, dtypes, emit_h=False):
    t, d = x2d.shape
    n = w_bf16.shape[1]
    tm = TOKEN_TILE
    assert t % tm == 0 and seq % tm == 0
    tiles_per_seq = seq // tm
    out_shape = [jax.ShapeDtypeStruct((t, wd), dt) for (_, wd), dt in zip(groups, dtypes)]
    out_specs = [pl.BlockSpec((tm, wd), lambda i: (i, 0)) for (_, wd) in groups]
    if emit_h:
        out_shape.append(jax.ShapeDtypeStruct((t, d), F32))
        out_specs.append(pl.BlockSpec((tm, d), lambda i: (i, 0)))
    return pl.pallas_call(
        functools.partial(_norm_proj_kernel, groups=tuple(groups), emit_h=emit_h),
        grid=(t // tm,),
        in_specs=[pl.BlockSpec((tm, d), lambda i: (i, 0)),
                  pl.BlockSpec((1, d), lambda i: (0, 0)),
                  pl.BlockSpec((1, 1, d), lambda i: (i // tiles_per_seq, 0, 0)),
                  pl.BlockSpec((1, 1, d), lambda i: (i // tiles_per_seq, 0, 0)),
                  pl.BlockSpec((d, n), lambda i: (0, 0))],
        out_specs=out_specs,
        out_shape=out_shape,
        compiler_params=_compiler_params(("arbitrary",)),
        name="norm_proj",
    )(x2d, gain.reshape(1, d), shift, scale, w_bf16)


def _sb_attention_kernel(q_ref, k_ref, v_ref, o_ref, acc_ref, carry_ref, *, tile, scale):
    i = pl.program_id(2)
    lane = lax.broadcasted_iota(I32, (1, LANES), 1)
    row = lax.broadcasted_iota(I32, (tile, tile), 0)
    col = lax.broadcasted_iota(I32, (tile, tile), 1)
    strict = col < row
    later = (row > col).astype(BF16)
    q = q_ref[0]
    outs = []
    for h in range(HEADS_PER_LANE_BLOCK):
        qh = jnp.where((lane >= HEAD_DIM * h) & (lane < HEAD_DIM * (h + 1)), q, jnp.zeros_like(q))

        def block(j, diag, qh=qh):
            start = pl.multiple_of(j * tile, tile)
            kb = k_ref[0, pl.ds(start, tile), :]
            vb = v_ref[0, pl.ds(start, tile), :]
            z = lax.dot_general(qh, kb, (((1,), (1,)), ((), ())), preferred_element_type=F32) * scale
            log_beta = _log_sigmoid(z)
            log_1m = log_beta - z
            if diag:
                log_1m = jnp.where(strict, log_1m, 0.0)
            stick = _dot_exact_rhs(log_1m, later) + carry_ref[...]
            w = jnp.exp(log_beta + stick)
            if diag:
                w = jnp.where(strict, w, 0.0)
            acc_ref[...] += jnp.dot(w.astype(BF16), vb, preferred_element_type=F32)
            carry_ref[...] += jnp.sum(log_1m, axis=1, keepdims=True)

        acc_ref[...] = jnp.zeros_like(acc_ref)
        carry_ref[...] = jnp.zeros_like(carry_ref)
        block(i, True)

        def alive():
            return jnp.max(carry_ref[...]) > F32_EXP_UNDERFLOW

        def cond(c):
            return jnp.logical_and(c[0] < i, c[1])

        def body(c):
            block(i - 1 - c[0], False)
            return c[0] + 1, alive()

        lax.while_loop(cond, body, (jnp.int32(0), alive()))
        outs.append(acc_ref[...])
    o_ref[0] = jnp.where(lane < HEAD_DIM, outs[0], outs[1]).astype(o_ref.dtype)


def _sb_attention(qkv, n_heads):
    b, s, _ = qkv.shape
    nblk = n_heads // HEADS_PER_LANE_BLOCK
    tile = ATTN_TILE
    assert s % tile == 0
    return pl.pallas_call(
        functools.partial(_sb_attention_kernel, tile=tile, scale=HEAD_DIM ** -0.5),
        grid=(b, nblk, s // tile),
        in_specs=[pl.BlockSpec((1, tile, LANES), lambda bi, hp, i: (bi, i, hp)),
                  pl.BlockSpec((1, s, LANES), lambda bi, hp, i: (bi, 0, nblk + hp)),
                  pl.BlockSpec((1, s, LANES), lambda bi, hp, i: (bi, 0, 2 * nblk + hp))],
        out_specs=pl.BlockSpec((1, tile, LANES), lambda bi, hp, i: (bi, i, hp)),
        out_shape=jax.ShapeDtypeStruct((b, s, n_heads * HEAD_DIM), BF16),
        scratch_shapes=[pltpu.VMEM((tile, LANES), F32), pltpu.VMEM((tile, 1), F32)],
        compiler_params=_compiler_params(("arbitrary", "arbitrary", "arbitrary")),
        name="sb_attention",
    )(qkv, qkv, qkv)


def _even_out_kernel(sb_ref, pool_ref, prev_ref, pw_ref, ps_ref, wo_ref, x_ref, g_ref, o_ref,
                     *, tm, tiles_per_seq):
    i = pl.program_id(0)
    t0 = (i % tiles_per_seq) * tm
    t = t0 + lax.broadcasted_iota(I32, (tm, 1), 0)
    s = t0 - tm + lax.broadcasted_iota(I32, (1, 2 * tm), 1)
    chunk_end = ((t // CHUNK) + 1) * CHUNK
    group = LANES
    acc = jnp.dot(sb_ref[...], wo_ref[0:4 * group, :], preferred_element_type=F32)
    for g, w in enumerate(POOL_WINDOWS):
        lo = jnp.maximum(t - w // 2, 0)
        hi = jnp.minimum(t + (w - w // 2), chunk_end)
        band = ((s >= lo) & (s < hi)).astype(BF16)
        cur = pool_ref[:, g * group:(g + 1) * group]
        ext = jnp.concatenate([prev_ref[:, g * group:(g + 1) * group], cur], axis=0)
        window_sum = _dot_exact_lhs(band, ext)
        pooled = window_sum / (hi - lo).astype(F32) - cur
        y = jnp.dot(pooled.astype(BF16), pw_ref[g], preferred_element_type=F32)
        y = y * ps_ref[:, g * group:(g + 1) * group]
        acc += jnp.dot(y.astype(BF16), wo_ref[(4 + g) * group:(5 + g) * group, :],
                       preferred_element_type=F32)
    o_ref[...] = x_ref[...] + g_ref[0] * acc


def _even_out(sb2d, pool2d, seq, pool_w_bf16, pool_scale, w_out_bf16, x2d, gate):
    t, d = x2d.shape
    tm = TOKEN_TILE
    tiles_per_seq = seq // tm
    assert seq % CHUNK == 0 and len(POOL_WINDOWS) * LANES == pool2d.shape[1]
    pw = pool2d.shape[1]
    return pl.pallas_call(
        functools.partial(_even_out_kernel, tm=tm, tiles_per_seq=tiles_per_seq),
        grid=(t // tm,),
        in_specs=[pl.BlockSpec((tm, sb2d.shape[1]), lambda i: (i, 0)),
                  pl.BlockSpec((tm, pw), lambda i: (i, 0)),
                  pl.BlockSpec((tm, pw), lambda i: (jnp.maximum(i - 1, 0), 0)),
                  pl.BlockSpec(pool_w_bf16.shape, lambda i: (0, 0, 0)),
                  pl.BlockSpec((1, pw), lambda i: (0, 0)),
                  pl.BlockSpec((d, d), lambda i: (0, 0)),
                  pl.BlockSpec((tm, d), lambda i: (i, 0)),
                  pl.BlockSpec((1, 1, d), lambda i: (i // tiles_per_seq, 0, 0))],
        out_specs=pl.BlockSpec((tm, d), lambda i: (i, 0)),
        out_shape=jax.ShapeDtypeStruct((t, d), F32),
        compiler_params=_compiler_params(("arbitrary",)),
        name="even_out",
    )(sb2d, pool2d, pool2d, pool_w_bf16, pool_scale.reshape(1, pw), w_out_bf16, x2d, gate)


def _odd_prep_kernel(qk_ref, f_ref, rw_ref, rwprev_ref, hs_ref, tri_ref, ctri_ref,
                     qg_ref, kg_ref, fb_ref, mix_ref, w0_ref, w2_ref, a0_ref, a2_ref, g2_ref,
                     kkp_ref, ka_ref, rk_ref,
                     qn_ref, kn_ref, dcol_ref, drow_ref, rt_ref, kt_ref, at_ref, bt_ref, v_ref,
                     cw_ref, bonus_ref, g_ref, carry_ref, *, tm, tiles_per_seq, width):
    i = pl.program_id(0)
    first = (i % tiles_per_seq) == 0
    hs = hs_ref[...]

    def head_sum(x):
        return _dot_exact_rhs(x, hs, 2)

    inv_hd = 1.0 / HEAD_DIM
    q = qk_ref[:, :width]
    k = qk_ref[:, width:]
    qn_ref[...] = (q * lax.rsqrt(head_sum(q * q) * inv_hd + NORM_EPS) * qg_ref[...]).astype(qn_ref.dtype)
    kn_ref[...] = (k * lax.rsqrt(head_sum(k * k) * inv_hd + NORM_EPS) * kg_ref[...]).astype(kn_ref.dtype)

    log_f = _log_sigmoid(f_ref[...] + fb_ref[...])

    @pl.when(first)
    def _():
        carry_ref[...] = jnp.zeros_like(carry_ref)

    dcum = _dot_exact_lhs(tri_ref[...], log_f) + carry_ref[...]
    carry_ref[...] = dcum[tm - 1:tm, :]
    dcol_ref[...] = dcum
    drow_ref[0, 0] = jnp.transpose(dcum)[:SUBLANES, :]

    pd = rw_ref[...]
    prev_row = jnp.where(first, 0.0, rwprev_ref[SUBLANES - 1:SUBLANES, :])
    row0 = lax.broadcasted_iota(I32, (tm, 1), 0) == 0
    prev = jnp.where(row0, prev_row, pltpu.roll(pd, 1, axis=0))
    pd = pd + (prev - pd) * mix_ref[...]
    r = pd[:, :width]
    kr = pd[:, width:2 * width]
    vr = pd[:, 2 * width:3 * width]
    lora_in = pd[:, 3 * width:3 * width + LANES]
    xg = pd[:, 3 * width + LANES:3 * width + 2 * LANES]
    w_log = _log_sigmoid(w0_ref[...] + jnp.dot(jnp.tanh(lora_in).astype(BF16), w2_ref[...],
                                              preferred_element_type=F32)) - 0.5
    log_w = -jnp.exp(w_log)
    a = _sigmoid(a0_ref[...] + jnp.dot(lora_in.astype(BF16), a2_ref[...], preferred_element_type=F32))
    g_ref[...] = jnp.dot(_sigmoid(xg).astype(BF16), g2_ref[...], preferred_element_type=F32)
    kk = kr * kkp_ref[...]
    kk = kk / jnp.maximum(jnp.sqrt(head_sum(kk * kk)), 1e-12)
    k2 = kr * (1.0 + (a - 1.0) * ka_ref[...])
    bonus_ref[...] = head_sum(r * k2 * rk_ref[...]) * vr
    cw = _dot_exact_lhs(ctri_ref[...], log_w)
    grow = jnp.exp(-cw)
    rt_ref[...] = r * jnp.exp(cw)
    kt_ref[...] = k2 * grow
    bt_ref[...] = kk * a * grow
    at_ref[...] = -kk * jnp.exp(cw - log_w)
    v_ref[...] = vr
    cw_ref[...] = cw


def _odd_prep(qk, fproj, rw, seq, p):
    t = qk.shape[0]
    width = qk.shape[1] // 2
    tm = TOKEN_TILE
    tiles_per_seq = seq // tm
    nrw = rw.shape[1]
    assert tm % CHUNK == 0 and seq % tm == 0 and nrw == 3 * width + 2 * LANES
    ri = lax.broadcasted_iota(I32, (tm, tm), 0)
    ci = lax.broadcasted_iota(I32, (tm, tm), 1)
    tri = (ci <= ri).astype(BF16)
    ctri = ((ci <= ri) & (ci // CHUNK == ri // CHUNK)).astype(BF16)
    hs = _head_sum_matrix(width)
    row = lambda a: a.reshape(1, -1)
    full = lambda a: pl.BlockSpec(a.shape, lambda i: (0,) * a.ndim)
    tile = lambda n: pl.BlockSpec((tm, n), lambda i: (i, 0))
    consts = [hs, tri, ctri, row(p['qg']), row(p['kg']), row(p['fb']), row(p['mix']), row(p['w0']),
              p['w2'], row(p['a0']), p['a2'], p['g2'], row(p['kk']), row(p['ka']), row(p['rk'])]
    wide = jax.ShapeDtypeStruct((t, width), F32)
    return pl.pallas_call(
        functools.partial(_odd_prep_kernel, tm=tm, tiles_per_seq=tiles_per_seq, width=width),
        grid=(t // tm,),
        in_specs=[tile(2 * width), tile(LANES), tile(nrw),
                  pl.BlockSpec((SUBLANES, nrw), lambda i: (jnp.maximum(i * (tm // SUBLANES) - 1, 0), 0))]
                 + [full(c) for c in consts],
        out_specs=[tile(width), tile(width), tile(LANES),
                   pl.BlockSpec((1, 1, SUBLANES, tm), lambda i: (i // tiles_per_seq, i % tiles_per_seq, 0, 0))]
                  + [tile(width)] * 8,
        out_shape=[jax.ShapeDtypeStruct((t, width), BF16), jax.ShapeDtypeStruct((t, width), BF16),
                   jax.ShapeDtypeStruct((t, LANES), F32),
                   jax.ShapeDtypeStruct((t // seq, tiles_per_seq, SUBLANES, tm), F32)] + [wide] * 8,
        scratch_shapes=[pltpu.VMEM((1, LANES), F32)],
        compiler_params=_compiler_params(("arbitrary",)),
        name="odd_prep",
    )(qk, fproj, rw, rw, *consts)


def _fox_attention_kernel(bound_ref, q_ref, k_ref, v_ref, dcol_ref, drow_ref, o_ref, acc_ref, m_ref,
                          l_ref, *, tile, scale):
    hp = pl.program_id(1)
    i = pl.program_id(2)
    qk_bound = bound_ref[0, 0]
    lane = lax.broadcasted_iota(I32, (1, LANES), 1)
    row = lax.broadcasted_iota(I32, (tile, tile), 0)
    col = lax.broadcasted_iota(I32, (tile, tile), 1)
    causal = col <= row
    sub = lax.broadcasted_iota(I32, (SUBLANES, tile), 0)
    q = q_ref[0]
    dq_all = dcol_ref[0]
    hs = range(HEADS_PER_LANE_BLOCK)
    heads = [hp * HEADS_PER_LANE_BLOCK + h for h in hs]
    qh = [jnp.where((lane >= HEAD_DIM * h) & (lane < HEAD_DIM * (h + 1)), q, jnp.zeros_like(q)) for h in hs]
    dq = [jnp.sum(jnp.where(lane == heads[h], dq_all, 0.0), axis=1, keepdims=True) for h in hs]

    def key_bias(j, h):
        return jnp.sum(jnp.where(sub == heads[h], drow_ref[0, j], 0.0), axis=0, keepdims=True)

    def block(j, diag):
        start = pl.multiple_of(j * tile, tile)
        kb = k_ref[0, pl.ds(start, tile), :]
        vb = v_ref[0, pl.ds(start, tile), :]
        s = [lax.dot_general(qh[h], kb, (((1,), (1,)), ((), ())), preferred_element_type=F32) * scale
             + dq[h] - key_bias(j, h) for h in hs]
        if diag:
            s = [jnp.where(causal, x, -jnp.inf) for x in s]
        m_prev = [m_ref[h] for h in hs]
        m_new = [jnp.maximum(m_prev[h], jnp.max(s[h], axis=1, keepdims=True)) for h in hs]
        p = [jnp.exp(s[h] - m_new[h]) for h in hs]
        alpha = [jnp.exp(m_prev[h] - m_new[h]) for h in hs]
        for h in hs:
            l_ref[h] = alpha[h] * l_ref[h] + jnp.sum(p[h], axis=1, keepdims=True)
            acc_ref[h] = alpha[h] * acc_ref[h] + jnp.dot(p[h].astype(BF16), vb, preferred_element_type=F32)
            m_ref[h] = m_new[h]

    acc_ref[...] = jnp.zeros_like(acc_ref)
    l_ref[...] = jnp.zeros_like(l_ref)
    m_ref[...] = jnp.full_like(m_ref, -jnp.inf)
    block(i, True)

    def alive(j):
        jc = jnp.maximum(j, 0)
        gaps = [qk_bound + jnp.max(dq[h] - m_ref[h]) - jnp.min(key_bias(jc, h)) for h in hs]
        return functools.reduce(jnp.maximum, gaps) > F32_EXP_UNDERFLOW

    def cond(c):
        return jnp.logical_and(c[0] < i, c[1])

    def body(c):
        j = i - 1 - c[0]
        block(j, False)
        return c[0] + 1, alive(j - 1)

    lax.while_loop(cond, body, (jnp.int32(0), alive(i - 1)))
    o_ref[0] = jnp.where(lane < HEAD_DIM, acc_ref[0] / l_ref[0], acc_ref[1] / l_ref[1])


def _fox_attention(qn, kn, v, dcol, drow, q_gain, k_gain):
    b, s, w = qn.shape
    nblk = w // LANES
    tile = ATTN_TILE
    assert drow.shape == (b, s // tile, SUBLANES, tile) and w // HEAD_DIM <= SUBLANES
    scale = HEAD_DIM ** -0.5
    bf16_slack = (1.0 + 2.0 ** -8) ** 2
    bound = (HEAD_DIM * scale * bf16_slack) * jnp.max(jnp.abs(q_gain)) * jnp.max(jnp.abs(k_gain))
    return pl.pallas_call(
        functools.partial(_fox_attention_kernel, tile=tile, scale=scale),
        grid=(b, nblk, s // tile),
        in_specs=[pl.BlockSpec(memory_space=pltpu.SMEM),
                  pl.BlockSpec((1, tile, LANES), lambda bi, hp, i: (bi, i, hp)),
                  pl.BlockSpec((1, s, LANES), lambda bi, hp, i: (bi, 0, hp)),
                  pl.BlockSpec((1, s, LANES), lambda bi, hp, i: (bi, 0, hp)),
                  pl.BlockSpec((1, tile, LANES), lambda bi, hp, i: (bi, i, 0)),
                  pl.BlockSpec((1, s // tile, SUBLANES, tile), lambda bi, hp, i: (bi, 0, 0, 0))],
        out_specs=pl.BlockSpec((1, tile, LANES), lambda bi, hp, i: (bi, i, hp)),
        out_shape=jax.ShapeDtypeStruct((b, s, w), F32),
        scratch_shapes=[pltpu.VMEM((HEADS_PER_LANE_BLOCK, tile, LANES), F32),
                        pltpu.VMEM((HEADS_PER_LANE_BLOCK, tile, 1), F32),
                        pltpu.VMEM((HEADS_PER_LANE_BLOCK, tile, 1), F32)],
        compiler_params=_compiler_params(("arbitrary", "arbitrary", "arbitrary")),
        name="fox_attention",
    )(bound.reshape(1, 1).astype(F32), qn, kn, v, dcol, drow)


_NN = (((1,), (0,)), ((), ()))
_NT = (((1,), (1,)), ((), ()))
_TN = (((0,), (0,)), ((), ()))


def _mm(a, b, dims=_NN):
    a_hi, a_lo = _split_bf16(a, 2)
    b_hi, b_lo = _split_bf16(b, 2)
    dot = lambda x, y: lax.dot_general(x, y, dims, preferred_element_type=F32)
    return dot(a_hi, b_hi) + (dot(a_hi, b_lo) + dot(a_lo, b_hi))


def _head_cols(h):
    return slice(h * HEAD_DIM, (h + 1) * HEAD_DIM)


def _rwkv_intra_kernel(rt_ref, kt_ref, at_ref, bt_ref, v_ref, inv_ref, arb_ref, fvu_ref, fvy_ref,
                       *, chunks, heads):
    c = CHUNK
    ri = lax.broadcasted_iota(I32, (2 * c, 2 * c), 0)
    ci = lax.broadcasted_iota(I32, (2 * c, 2 * c), 1)
    keep = (ci % c) < jnp.where(ri < c, ri, ri - c + 1)
    eye = (lax.broadcasted_iota(I32, (c, c), 0) == lax.broadcasted_iota(I32, (c, c), 1)).astype(F32)
    hs = range(heads)

    def chunk(cidx, carry):
        rows = pl.ds(pl.multiple_of(cidx * c, c), c)
        a_all = [jnp.where(keep,
                           _mm(jnp.concatenate([at_ref[rows, _head_cols(h)], rt_ref[rows, _head_cols(h)]], axis=0),
                               jnp.concatenate([bt_ref[rows, _head_cols(h)], kt_ref[rows, _head_cols(h)]], axis=0),
                               _NT), 0.0) for h in hs]
        a_ab = [a[:c, :c] for a in a_all]
        inv = [eye + a for a in a_ab]
        power = [_mm(a, a) for a in a_ab]
        for _ in range(4):
            both = [_mm(jnp.concatenate([inv[h], power[h]], axis=0), power[h]) for h in hs]
            inv = [inv[h] + both[h][:c] for h in hs]
            power = [both[h][c:] for h in hs]
        inv = [inv[h] + _mm(inv[h], power[h]) for h in hs]
        from_v = [_mm(a_all[h][:, c:], v_ref[rows, _head_cols(h)]) for h in hs]
        inv_ref[rows, :] = jnp.concatenate(inv, axis=1)
        arb_ref[rows, :] = jnp.concatenate([a[c:, :c] for a in a_all], axis=1)
        fvu_ref[rows, :] = jnp.concatenate([f[:c] for f in from_v], axis=1)
        fvy_ref[rows, :] = jnp.concatenate([f[c:] for f in from_v], axis=1)
        return carry

    lax.fori_loop(0, chunks, chunk, 0)


def _rwkv_scan_kernel(rt_ref, kt_ref, at_ref, bt_ref, v_ref, cw_ref, inv_ref, arb_ref, fvu_ref, fvy_ref,
                      y_ref, st_ref, *, steps_per_seq, chunks, heads):
    i = pl.program_id(0)

    @pl.when(i % steps_per_seq == 0)
    def _():
        st_ref[...] = jnp.zeros_like(st_ref)

    c = CHUNK
    hs = range(heads)

    def chunk(cidx, carry):
        rows = pl.ds(pl.multiple_of(cidx * c, c), c)
        last = pl.ds(pl.multiple_of(cidx * c, c) + (c - SUBLANES), SUBLANES)
        ar = [jnp.concatenate([at_ref[rows, _head_cols(h)], rt_ref[rows, _head_cols(h)]], axis=0) for h in hs]
        bk = [jnp.concatenate([bt_ref[rows, _head_cols(h)], kt_ref[rows, _head_cols(h)]], axis=0) for h in hs]
        st = [st_ref[h] for h in hs]
        from_state = [_mm(ar[h], st[h], _NT) for h in hs]
        u = [_mm(inv_ref[rows, _head_cols(h)], from_state[h][:c] + fvu_ref[rows, _head_cols(h)]) for h in hs]
        y = [from_state[h][c:] + fvy_ref[rows, _head_cols(h)] + _mm(arb_ref[rows, _head_cols(h)], u[h])
             for h in hs]
        for h in hs:
            decay = jnp.exp(cw_ref[last, _head_cols(h)][SUBLANES - 1:SUBLANES, :])
            uv = jnp.concatenate([u[h], v_ref[rows, _head_cols(h)]], axis=0)
            st_ref[h] = (st[h] + _mm(uv, bk[h], _TN)) * decay
        y_ref[rows, :] = jnp.concatenate(y, axis=1)
        return carry

    lax.fori_loop(0, chunks, chunk, 0)


def _rwkv_scan(rt, kt, at, bt, v, cw, seq):
    t, w = rt.shape
    heads = w // HEAD_DIM
    ts = 2 * TOKEN_TILE
    assert seq % ts == 0
    spec = pl.BlockSpec((ts, w), lambda i: (i, 0))
    wide = jax.ShapeDtypeStruct((t, w), F32)
    inv, arb, fvu, fvy = pl.pallas_call(
        functools.partial(_rwkv_intra_kernel, chunks=ts // CHUNK, heads=heads),
        grid=(t // ts,),
        in_specs=[spec] * 5,
        out_specs=[spec] * 4,
        out_shape=[wide] * 4,
        compiler_params=_compiler_params(("arbitrary",)),
        name="rwkv_intra",
    )(rt, kt, at, bt, v)
    return pl.pallas_call(
        functools.partial(_rwkv_scan_kernel, steps_per_seq=seq // ts, chunks=ts // CHUNK, heads=heads),
        grid=(t // ts,),
        in_specs=[spec] * 10,
        out_specs=spec,
        out_shape=wide,
        scratch_shapes=[pltpu.VMEM((heads, HEAD_DIM, HEAD_DIM), F32)],
        compiler_params=_compiler_params(("arbitrary",)),
        name="rwkv_scan",
    )(rt, kt, at, bt, v, cw, inv, arb, fvu, fvy)


def _odd_out_kernel(att_ref, og_ref, y_ref, bonus_ref, g_ref, hs_ref, lnw_ref, lnb_ref, wo_ref,
                    x_ref, gate_ref, o_ref, *, width):
    hs = hs_ref[...]
    inv_hd = 1.0 / HEAD_DIM
    fox = att_ref[...] * _sigmoid(og_ref[...])
    y = y_ref[...]
    mu = _dot_exact_rhs(y, hs, 2) * inv_hd
    yc = y - mu
    var = _dot_exact_rhs(yc * yc, hs, 2) * inv_hd
    yn = yc * lax.rsqrt(var + RWKV_GN_EPS) * lnw_ref[...] + lnb_ref[...]
    rw = (yn + bonus_ref[...]) * g_ref[...]
    acc = jnp.dot(fox.astype(BF16), wo_ref[:width, :], preferred_element_type=F32)
    acc += jnp.dot(rw.astype(BF16), wo_ref[width:, :], preferred_element_type=F32)
    o_ref[...] = x_ref[...] + gate_ref[0] * acc


def _odd_out(att, og, y, bonus, g, ln_w, ln_b, w_out_bf16, x2d, seq, gate):
    t, d = x2d.shape
    width = att.shape[1]
    tm = TOKEN_TILE
    tiles_per_seq = seq // tm
    hs = _head_sum_matrix(width)
    tile = lambda n: pl.BlockSpec((tm, n), lambda i: (i, 0))
    return pl.pallas_call(
        functools.partial(_odd_out_kernel, width=width),
        grid=(t // tm,),
        in_specs=[tile(width)] * 5
                 + [pl.BlockSpec(hs.shape, lambda i: (0, 0)),
                    pl.BlockSpec((1, width), lambda i: (0, 0)), pl.BlockSpec((1, width), lambda i: (0, 0)),
                    pl.BlockSpec((d, d), lambda i: (0, 0)), tile(d),
                    pl.BlockSpec((1, 1, d), lambda i: (i // tiles_per_seq, 0, 0))],
        out_specs=tile(d),
        out_shape=jax.ShapeDtypeStruct((t, d), F32),
        compiler_params=_compiler_params(("arbitrary",)),
        name="odd_out",
    )(att, og, y, bonus, g, hs, ln_w.reshape(1, width), ln_b.reshape(1, width), w_out_bf16, x2d, gate)


def _top_rows(s, k, prio, payload=None):
    big = jnp.int32(2 ** 30)
    vals, picks = [], []
    for _ in range(k):
        m = jnp.max(s, axis=0, keepdims=True)
        am = jnp.min(jnp.where(s == m, prio, big), axis=0, keepdims=True)
        hit = prio == am
        vals.append(m)
        if payload is None:
            picks.append(am)
        else:
            picks.append(jnp.max(jnp.where(hit, payload, -1), axis=0, keepdims=True))
        s = jnp.where(hit, -jnp.inf, s)
    return jnp.concatenate(vals, axis=0), jnp.concatenate(picks, axis=0)


def _staircase_groups():
    k, g = PEER_TOPK, SUBLANES
    groups = []
    for b in range(k):
        count = k // (b + 1)
        if count <= 1:
            break
        for a0 in range(0, count, g):
            groups.append((a0, b, False, min(g, count - a0)))
    first_single = next(b for b in range(k) if k // (b + 1) <= 1)
    for b0 in range(first_single, k, g):
        groups.append((0, b0, True, min(g, k - b0)))
    return groups


def _peer_topk_kernel(q_ref, k1_ref, k2_ref, idx_ref, gate_ref, *, row_scale):
    half = PEER_NKEYS
    nt = (((1,), (1,)), ((), ()))
    tt = q_ref.shape[0]
    key_id = lax.broadcasted_iota(I32, (half, tt), 0)
    sub = lax.broadcasted_iota(I32, (SUBLANES, 1), 0)
    groups = _staircase_groups()
    prio = jnp.concatenate(
        [jnp.broadcast_to((sub * 0 + a0) * PEER_TOPK + b0 + sub if along_b
                          else (a0 + sub) * PEER_TOPK + b0, (SUBLANES, tt))
         for a0, b0, along_b, _ in groups], axis=0)
    idx_rows = []
    for h in range(PEER_HEADS):
        q1 = q_ref[:, (2 * h) * half:(2 * h + 1) * half]
        q2 = q_ref[:, (2 * h + 1) * half:(2 * h + 2) * half]
        s1 = lax.dot_general(k1_ref[...], q1, nt, precision=HIGHEST, preferred_element_type=F32)
        s2 = lax.dot_general(k2_ref[...], q2, nt, precision=HIGHEST, preferred_element_type=F32)
        v1, i1 = _top_rows(s1, PEER_TOPK, key_id)
        v2, i2 = _top_rows(s2, PEER_TOPK, key_id)
        cand, cand_idx = [], []
        for a0, b0, along_b, valid in groups:
            if along_b:
                val = v1[a0:a0 + 1] + v2[b0:b0 + SUBLANES]
                eid = i1[a0:a0 + 1] * PEER_NKEYS + i2[b0:b0 + SUBLANES]
            else:
                val = v1[a0:a0 + SUBLANES] + v2[b0:b0 + 1]
                eid = i1[a0:a0 + SUBLANES] * PEER_NKEYS + i2[b0:b0 + 1]
            cand.append(jnp.where(sub < valid, val, -jnp.inf))
            cand_idx.append(eid)
        top_s, idx = _top_rows(jnp.concatenate(cand, axis=0), PEER_TOPK, prio,
                               payload=jnp.concatenate(cand_idx, axis=0))
        e = jnp.exp(top_s - top_s[0:1])
        gate_ref[h * PEER_TOPK:(h + 1) * PEER_TOPK, :] = e / jnp.sum(e, axis=0, keepdims=True)
        idx_rows.append(idx)
    idx_ref[...] = jnp.transpose(jnp.concatenate(idx_rows, axis=0)) * row_scale


def _peer_topk(q2d, k1, k2, row_scale):
    t, n = q2d.shape
    tt = TOKEN_TILE
    assert PEER_TOPK % SUBLANES == 0
    return pl.pallas_call(
        functools.partial(_peer_topk_kernel, row_scale=row_scale),
        grid=(t // tt,),
        in_specs=[pl.BlockSpec((tt, n), lambda i: (i, 0)),
                  pl.BlockSpec(k1.shape, lambda i: (0, 0)),
                  pl.BlockSpec(k2.shape, lambda i: (0, 0))],
        out_specs=[pl.BlockSpec((tt, PEER_SLOTS), lambda i: (i, 0)),
                   pl.BlockSpec((PEER_SLOTS, tt), lambda i: (0, i))],
        out_shape=[jax.ShapeDtypeStruct((t, PEER_SLOTS), I32),
                   jax.ShapeDtypeStruct((PEER_SLOTS, t), F32)],
        compiler_params=_compiler_params(("arbitrary",)),
        name="peer_topk",
    )(q2d, k1, k2)


def _pack_table_kernel(t_ref, o_ref, *, pairs):
    high = jnp.int32(-65536)
    for j in range(pairs):
        lo = t_ref[:, (2 * j) * LANES:(2 * j + 1) * LANES].astype(BF16).astype(F32)
        hi = t_ref[:, (2 * j + 1) * LANES:(2 * j + 2) * LANES].astype(BF16).astype(F32)
        lo_bits = lax.shift_right_logical(pltpu.bitcast(lo, I32), 16)
        o_ref[:, j * LANES:(j + 1) * LANES] = lax.bitwise_or(lo_bits, lax.bitwise_and(pltpu.bitcast(hi, I32), high))


def _pack_table(table):
    e, d = table.shape
    pairs = d // (2 * LANES)
    rows = 2 * TOKEN_TILE
    words = pl.pallas_call(
        functools.partial(_pack_table_kernel, pairs=pairs),
        grid=(e // rows,),
        in_specs=[pl.BlockSpec((rows, d), lambda i: (i, 0))],
        out_specs=pl.BlockSpec((rows, pairs * LANES), lambda i: (i, 0)),
        out_shape=jax.ShapeDtypeStruct((e, pairs * LANES), I32),
        compiler_params=_compiler_params(("arbitrary",)),
        name="pack_table",
    )(table)
    return words.reshape(e * pairs, LANES)


def _unpack_words(w):
    lo = pltpu.bitcast(lax.shift_left(w, 16), F32)
    hi = pltpu.bitcast(lax.bitwise_and(w, jnp.int32(-65536)), F32)
    return lo, hi


def _gather_rows(idx_ref, tab_ref, tile_ref, tok, pairs):
    for m in range(PEER_SLOTS):
        row = pl.multiple_of(idx_ref[tok, m], pairs)
        tile_ref[m * pairs:(m + 1) * pairs, :] = tab_ref[pl.ds(row, pairs), :]


def _tile_chunk(tile_ref, j, pairs):
    return _unpack_words(tile_ref[pl.ds(j, PEER_SLOTS, stride=pairs), :])


def _pipelined_tokens(ntok, gather, compute, init):
    gather(0, 0)

    def pair(p, carry):
        tok = 2 * p
        gather(tok + 1, 1)
        carry = compute(tok, 0, carry)
        gather(jnp.minimum(tok + 2, ntok - 1), 0)
        return compute(tok + 1, 1, carry)

    return lax.fori_loop(0, ntok // 2, pair, init)


def _pick_row(block, r):
    sub = lax.broadcasted_iota(I32, block.shape, 0)
    return jnp.sum(jnp.where(sub == r, block, 0.0), axis=0, keepdims=True)


def _peer_act_kernel(idx_ref, tab_ref, h_ref, gate_ref, coef_ref, tile0_ref, tile1_ref, *, pairs):
    ntok = h_ref.shape[0]
    lane = lax.broadcasted_iota(I32, (PEER_SLOTS, ntok), 1)
    tiles = (tile0_ref, tile1_ref)

    def gather(tok, buf):
        _gather_rows(idx_ref, tab_ref, tiles[buf], tok, pairs)

    def compute(tok, buf, act_t):
        base = pl.multiple_of((tok // SUBLANES) * SUBLANES, SUBLANES)
        h_row = _pick_row(h_ref[pl.ds(base, SUBLANES), :], tok % SUBLANES)
        r = jnp.zeros((PEER_SLOTS, LANES), F32)
        for j in range(pairs):
            lo, hi = _tile_chunk(tiles[buf], j, pairs)
            h_lo = h_row[:, (2 * j) * LANES:(2 * j + 1) * LANES]
            h_hi = h_row[:, (2 * j + 1) * LANES:(2 * j + 2) * LANES]
            r = r + lo * h_lo + hi * h_hi
        col = jnp.sum(r, axis=1, keepdims=True)
        return jnp.where(lane == tok, col, act_t)

    act_t = _pipelined_tokens(ntok, gather, compute, jnp.zeros((PEER_SLOTS, ntok), F32))
    gelu = 0.5 * act_t * (1.0 + lax.erf(act_t * (2.0 ** -0.5)))
    coef_ref[...] = gate_ref[...] * gelu


def _peer_act(idx, table_words, h2d, gate_t):
    t, d = h2d.shape
    pairs = d // (2 * LANES)
    nt = PEER_TOKENS
    return pl.pallas_call(
        functools.partial(_peer_act_kernel, pairs=pairs),
        grid=(t // nt,),
        in_specs=[pl.BlockSpec((nt, PEER_SLOTS), lambda i: (i, 0), memory_space=pltpu.SMEM),
                  pl.BlockSpec(table_words.shape, lambda i: (0, 0), pipeline_mode=pl.Buffered(1)),
                  pl.BlockSpec((nt, d), lambda i: (i, 0)),
                  pl.BlockSpec((PEER_SLOTS, nt), lambda i: (0, i))],
        out_specs=pl.BlockSpec((PEER_SLOTS, nt), lambda i: (0, i)),
        out_shape=jax.ShapeDtypeStruct((PEER_SLOTS, t), F32),
        scratch_shapes=[pltpu.VMEM((pairs * PEER_SLOTS, LANES), I32)] * 2,
        compiler_params=_compiler_params(("arbitrary",), vmem_mb=56),
        name="peer_act",
    )(idx, table_words, h2d, gate_t)


def _peer_out_kernel(idx_ref, tab_ref, coef_ref, x_ref, g_ref, o_ref, tile0_ref, tile1_ref, *, pairs):
    ntok, d = x_ref.shape
    lane = lax.broadcasted_iota(I32, (PEER_SLOTS, ntok), 1)
    sub = lax.broadcasted_iota(I32, (SUBLANES, d), 0)
    groups = PEER_SLOTS // SUBLANES
    tiles = (tile0_ref, tile1_ref)

    def gather(tok, buf):
        _gather_rows(idx_ref, tab_ref, tiles[buf], tok, pairs)

    def compute(tok, buf, rows):
        coef = jnp.sum(jnp.where(lane == tok, coef_ref[...], 0.0), axis=1, keepdims=True)
        pieces = []
        for j in range(pairs):
            for vals in _tile_chunk(tiles[buf], j, pairs):
                prod = (vals * coef).reshape(groups, SUBLANES, LANES)
                pieces.append(jnp.sum(jnp.sum(prod, axis=0), axis=0, keepdims=True))
        row = jnp.concatenate(pieces, axis=1)
        rows = jnp.where(sub == tok % SUBLANES, row, rows)
        base = pl.multiple_of((tok // SUBLANES) * SUBLANES, SUBLANES)
        o_ref[pl.ds(base, SUBLANES), :] = x_ref[pl.ds(base, SUBLANES), :] + g_ref[0] * rows
        return rows

    _pipelined_tokens(ntok, gather, compute, jnp.zeros((SUBLANES, d), F32))


def _peer_out(idx, table_words, coef_t, x2d, seq, gate):
    t, d = x2d.shape
    pairs = d // (2 * LANES)
    nt = PEER_TOKENS
    per_seq = seq // nt
    return pl.pallas_call(
        functools.partial(_peer_out_kernel, pairs=pairs),
        grid=(t // nt,),
        in_specs=[pl.BlockSpec((nt, PEER_SLOTS), lambda i: (i, 0), memory_space=pltpu.SMEM),
                  pl.BlockSpec(table_words.shape, lambda i: (0, 0), pipeline_mode=pl.Buffered(1)),
                  pl.BlockSpec((PEER_SLOTS, nt), lambda i: (0, i)),
                  pl.BlockSpec((nt, d), lambda i: (i, 0)),
                  pl.BlockSpec((1, 1, d), lambda i: (i // per_seq, 0, 0))],
        out_specs=pl.BlockSpec((nt, d), lambda i: (i, 0)),
        out_shape=jax.ShapeDtypeStruct((t, d), F32),
        scratch_shapes=[pltpu.VMEM((pairs * PEER_SLOTS, LANES), I32)] * 2,
        compiler_params=_compiler_params(("arbitrary",), vmem_mb=56),
        name="peer_out",
    )(idx, table_words, coef_t, x2d, gate)


def _peer_ffn(x2d, seq, gain, shift, scale, gate, wq_bf16, k1, k2, u_words, v_words):
    n = wq_bf16.shape[1]
    q, h = _norm_proj(x2d, seq, gain, shift, scale, wq_bf16, [(0, n)], [F32], emit_h=True)
    idx, gate_t = _peer_topk(q, k1, k2, x2d.shape[1] // (2 * LANES))
    coef_t = _peer_act(idx, u_words, h, gate_t)
    return _peer_out(idx, v_words, coef_t, x2d, seq, gate)


def _final_norm_kernel(x_ref, gain_ref, o_ref):
    x = x_ref[...]
    ms = jnp.mean(x * x, axis=-1, keepdims=True)
    o_ref[...] = x * lax.rsqrt(ms + NORM_EPS) * gain_ref[...]


def _final_norm(x2d, gain):
    t, d = x2d.shape
    tm = TOKEN_TILE
    return pl.pallas_call(
        _final_norm_kernel,
        grid=(t // tm,),
        in_specs=[pl.BlockSpec((tm, d), lambda i: (i, 0)), pl.BlockSpec((1, d), lambda i: (0, 0))],
        out_specs=pl.BlockSpec((tm, d), lambda i: (i, 0)),
        out_shape=jax.ShapeDtypeStruct((t, d), F32),
        compiler_params=_compiler_params(("arbitrary",)),
        name="final_norm",
    )(x2d, gain.reshape(1, d))


def _even_mixer(x2d, b, seq, gain, shift, scale, gate, w_in, w_out, pool_w, pool_scale):
    d = x2d.shape[1]
    pool_width = len(POOL_WINDOWS) * LANES
    sb_width = (w_in.shape[1] - pool_width) // 3
    qkv, pool = _norm_proj(x2d, seq, gain, shift, scale, w_in.astype(BF16),
                           [(0, 3 * sb_width), (3 * sb_width, pool_width)], [BF16, F32])
    sb = _sb_attention(qkv.reshape(b, seq, 3 * sb_width), sb_width // HEAD_DIM)
    return _even_out(sb.reshape(b * seq, sb_width), pool, seq, pool_w.astype(BF16), pool_scale,
                     w_out.astype(BF16), x2d, gate)


def _odd_mixer(x2d, b, seq, gain, shift, scale, gate, w_in, w_out, fox_qnorm, fox_knorm, fox_fbias,
               rwkv_mix, rwkv_w0, rwkv_w2, rwkv_a0, rwkv_a2, rwkv_g2, rwkv_kk, rwkv_ka, rwkv_rk,
               rwkv_ln_w, rwkv_ln_b):
    d = x2d.shape[1]
    width = rwkv_w0.shape[0]
    heads = width // HEAD_DIM
    fox_in = 4 * width + heads
    n_rw = w_in.shape[1] - fox_in
    w_cols = jnp.concatenate([w_in[:, :4 * width], w_in[:, fox_in:], w_in[:, 4 * width:fox_in],
                              jnp.zeros((d, LANES - heads), w_in.dtype)], axis=1).astype(BF16)
    qk, v, og, rw, fproj = _norm_proj(
        x2d, seq, gain, shift, scale, w_cols,
        [(0, 2 * width), (2 * width, width), (3 * width, width), (4 * width, n_rw), (4 * width + n_rw, LANES)],
        [F32, BF16, F32, F32, F32])
    lora = rwkv_w2.shape[0]
    assert 2 * lora == LANES and rwkv_a2.shape[0] == lora and rwkv_g2.shape[0] == LANES
    zeros = jnp.zeros((lora, width), F32)
    params = dict(
        qg=jnp.tile(fox_qnorm, heads), kg=jnp.tile(fox_knorm, heads),
        fb=jnp.concatenate([fox_fbias, jnp.zeros((LANES - heads,), F32)]),
        mix=rwkv_mix, w0=rwkv_w0, a0=rwkv_a0,
        w2=jnp.concatenate([rwkv_w2, zeros], axis=0).astype(BF16),
        a2=jnp.concatenate([zeros, rwkv_a2], axis=0).astype(BF16),
        g2=rwkv_g2.astype(BF16), kk=rwkv_kk, ka=rwkv_ka, rk=rwkv_rk.reshape(-1))
    qn, kn, dcol, drow, rt, kt, at, bt, vv, cw, bonus, g = _odd_prep(qk, fproj, rw, seq, params)
    r3 = lambda a: a.reshape(b, seq, -1)
    att = _fox_attention(r3(qn), r3(kn), r3(v), r3(dcol), drow, fox_qnorm, fox_knorm)
    y = _rwkv_scan(rt, kt, at, bt, vv, cw, seq)
    return _odd_out(att.reshape(b * seq, width), og, y, bonus, g, rwkv_ln_w, rwkv_ln_b,
                    w_out.astype(BF16), x2d, seq, gate)


def kernel(x, c, ada_w, ada_b, norm_mix, norm_ffn, ev_w_in, ev_w_out, pool_w, pool_scale, od_w_in, od_w_out, fox_qnorm, fox_knorm, fox_fbias, rwkv_mix, rwkv_w0, rwkv_w2, rwkv_a0, rwkv_a2, rwkv_g2, rwkv_kk, rwkv_ka, rwkv_rk, rwkv_ln_w, rwkv_ln_b, peer_wq, peer_k1, peer_k2, peer_u, peer_v, final_norm):
    b, seq, d = x.shape
    depth = ada_w.shape[0]
    assert TOKEN_TILE == ATTN_TILE and seq % (2 * TOKEN_TILE) == 0
    mod = _adaln(c, ada_w, ada_b).reshape(depth, b, 6, 1, d)
    x2d = x.reshape(b * seq, d)
    for layer in range(depth):
        sh1, sc1, g1, sh2, sc2, g2 = (mod[layer, :, i] for i in range(6))
        j = layer // 2
        if layer % 2 == 0:
            x2d = _even_mixer(x2d, b, seq, norm_mix[layer], sh1, sc1, g1, ev_w_in[j], ev_w_out[j],
                              pool_w[j], pool_scale[j])
        else:
            x2d = _odd_mixer(x2d, b, seq, norm_mix[layer], sh1, sc1, g1, od_w_in[j], od_w_out[j],
                             fox_qnorm[j], fox_knorm[j], fox_fbias[j], rwkv_mix[j], rwkv_w0[j],
                             rwkv_w2[j], rwkv_a0[j], rwkv_a2[j], rwkv_g2[j], rwkv_kk[j], rwkv_ka[j],
                             rwkv_rk[j], rwkv_ln_w[j], rwkv_ln_b[j])
        x2d = _peer_ffn(x2d, seq, norm_ffn[layer], sh2, sc2, g2, peer_wq[layer].astype(BF16),
                        peer_k1[layer], peer_k2[layer], _pack_table(peer_u[layer]),
                        _pack_table(peer_v[layer]))
    return _final_norm(x2d, final_norm).reshape(b, seq, d)
```

```python
import functools

import jax
import jax.numpy as jnp
from jax import lax
from jax.experimental import pallas as pl
from jax.experimental.pallas import tpu as pltpu

F32 = jnp.float32
BF16 = jnp.bfloat16
I32 = jnp.int32
HIGHEST = lax.Precision.HIGHEST

LANES = 128
SUBLANES = 8
VMEM_BYTES_V7X = 64 * 1024 * 1024

HEAD_DIM = 64
HEADS_PER_LANE_BLOCK = LANES // HEAD_DIM
NORM_EPS = 1e-6
RWKV_GN_EPS = 64e-5
CHUNK = 64
POOL_WINDOWS = (2, 4, 8, 16)
PEER_TOPK = 16
PEER_NKEYS = 128
PEER_HEADS = 8
PEER_SLOTS = PEER_HEADS * PEER_TOPK

TOKEN_TILE = 256
ATTN_TILE = 256
PEER_TOKENS = 128

F32_EXP_UNDERFLOW = -104.0


def _compiler_params(semantics, vmem_mb=48):
    return pltpu.CompilerParams(dimension_semantics=semantics,
                                vmem_limit_bytes=min(vmem_mb * 1024 * 1024, VMEM_BYTES_V7X))


def _split_bf16(x, n):
    parts = []
    r = x
    for _ in range(n):
        p = r.astype(BF16)
        parts.append(p)
        r = r - p.astype(F32)
    return parts


def _dot_exact_rhs(x, m, n=3):
    acc = None
    for p in _split_bf16(x, n):
        t = jnp.dot(p, m, preferred_element_type=F32)
        acc = t if acc is None else acc + t
    return acc


def _dot_exact_lhs(m, x, n=3):
    acc = None
    for p in _split_bf16(x, n):
        t = jnp.dot(m, p, preferred_element_type=F32)
        acc = t if acc is None else acc + t
    return acc


def _log_sigmoid(z):
    return jnp.minimum(z, 0.0) - jnp.log1p(jnp.exp(-jnp.abs(z)))


def _sigmoid(z):
    return 1.0 / (1.0 + jnp.exp(-z))


def _head_sum_matrix(width):
    r = lax.broadcasted_iota(I32, (width, width), 0) // HEAD_DIM
    c = lax.broadcasted_iota(I32, (width, width), 1) // HEAD_DIM
    return (r == c).astype(BF16)


def _adaln_kernel(c_ref, w_ref, b_ref, o_ref):
    c = c_ref[...]
    c_act = c * _sigmoid(c)
    o_ref[0] = jnp.dot(c_act, w_ref[0], precision=HIGHEST, preferred_element_type=F32) + b_ref[0]


def _adaln(c, ada_w, ada_b):
    b, d = c.shape
    depth, _, n = ada_w.shape
    tn = 1536
    assert n % tn == 0 and b <= SUBLANES
    c_pad = jnp.zeros((SUBLANES, d), F32).at[:b].set(c)
    out = pl.pallas_call(
        _adaln_kernel,
        grid=(depth, n // tn),
        in_specs=[pl.BlockSpec((SUBLANES, d), lambda l, j: (0, 0)),
                  pl.BlockSpec((1, d, tn), lambda l, j: (l, 0, j)),
                  pl.BlockSpec((1, 1, tn), lambda l, j: (l, 0, j))],
        out_specs=pl.BlockSpec((1, SUBLANES, tn), lambda l, j: (l, 0, j)),
        out_shape=jax.ShapeDtypeStruct((depth, SUBLANES, n), F32),
        compiler_params=_compiler_params(("arbitrary", "arbitrary")),
        name="adaln",
    )(c_pad, ada_w, ada_b.reshape(depth, 1, n))
    return out[:, :b, :]


def _norm_proj_kernel(x_ref, gain_ref, shift_ref, scale_ref, w_ref, *out_refs, groups, emit_h):
    x = x_ref[...]
    ms = jnp.mean(x * x, axis=-1, keepdims=True)
    y = x * lax.rsqrt(ms + NORM_EPS) * gain_ref[...]
    h = y * (1.0 + scale_ref[0]) + shift_ref[0]
    hb = h.astype(BF16)
    for (start, width), o_ref in zip(groups, out_refs):
        o_ref[...] = jnp.dot(hb, w_ref[:, start:start + width],
                             preferred_element_type=F32).astype(o_ref.dtype)
    if emit_h:
        out_refs[-1][...] = h


def _norm_proj(x2d, seq, gain, shift, scale, w_bf16, groups, dtypes, emit_h=False):
    t, d = x2d.shape
    n = w_bf16.shape[1]
    tm = TOKEN_TILE
    assert t % tm == 0 and seq % tm == 0
    tiles_per_seq = seq // tm
    out_shape = [jax.ShapeDtypeStruct((t, wd), dt) for (_, wd), dt in zip(groups, dtypes)]
    out_specs = [pl.BlockSpec((tm, wd), lambda i: (i, 0)) for (_, wd) in groups]
    if emit_h:
        out_shape.append(jax.ShapeDtypeStruct((t, d), F32))
        out_specs.append(pl.BlockSpec((tm, d), lambda i: (i, 0)))
    return pl.pallas_call(
        functools.partial(_norm_proj_kernel, groups=tuple(groups), emit_h=emit_h),
        grid=(t // tm,),
        in_specs=[pl.BlockSpec((tm, d), lambda i: (i, 0)),
                  pl.BlockSpec((1, d), lambda i: (0, 0)),
                  pl.BlockSpec((1, 1, d), lambda i: (i // tiles_per_seq, 0, 0)),
                  pl.BlockSpec((1, 1, d), lambda i: (i // tiles_per_seq, 0, 0)),
                  pl.BlockSpec((d, n), lambda i: (0, 0))],
        out_specs=out_specs,
        out_shape=out_shape,
        compiler_params=_compiler_params(("arbitrary",)),
        name="norm_proj",
    )(x2d, gain.reshape(1, d), shift, scale, w_bf16)


def _sb_attention_kernel(q_ref, k_ref, v_ref, o_ref, acc_ref, carry_ref, *, tile, scale):
    i = pl.program_id(2)
    lane = lax.broadcasted_iota(I32, (1, LANES), 1)
    row = lax.broadcasted_iota(I32, (tile, tile), 0)
    col = lax.broadcasted_iota(I32, (tile, tile), 1)
    strict = col < row
    later = (row > col).astype(BF16)
    q = q_ref[0]
    hs = range(HEADS_PER_LANE_BLOCK)
    qh = [jnp.where((lane >= HEAD_DIM * h) & (lane < HEAD_DIM * (h + 1)), q, jnp.zeros_like(q)) for h in hs]

    def block(j, diag):
        start = pl.multiple_of(j * tile, tile)
        kb = k_ref[0, pl.ds(start, tile), :]
        vb = v_ref[0, pl.ds(start, tile), :]
        z = [lax.dot_general(qh[h], kb, (((1,), (1,)), ((), ())), preferred_element_type=F32) * scale
             for h in hs]
        log_beta = [_log_sigmoid(x) for x in z]
        log_1m = [log_beta[h] - z[h] for h in hs]
        if diag:
            log_1m = [jnp.where(strict, x, 0.0) for x in log_1m]
        stick = [_dot_exact_rhs(log_1m[h], later, 2) + carry_ref[h] for h in hs]
        w = [jnp.exp(log_beta[h] + stick[h]) for h in hs]
        if diag:
            w = [jnp.where(strict, x, 0.0) for x in w]
        for h in hs:
            acc_ref[h] += jnp.dot(w[h].astype(BF16), vb, preferred_element_type=F32)
            carry_ref[h] += jnp.sum(log_1m[h], axis=1, keepdims=True)

    acc_ref[...] = jnp.zeros_like(acc_ref)
    carry_ref[...] = jnp.zeros_like(carry_ref)
    block(i, True)

    def alive():
        return jnp.max(carry_ref[...]) > F32_EXP_UNDERFLOW

    def cond(c):
        return jnp.logical_and(c[0] < i, c[1])

    def body(c):
        block(i - 1 - c[0], False)
        return c[0] + 1, alive()

    lax.while_loop(cond, body, (jnp.int32(0), alive()))
    o_ref[0] = jnp.where(lane < HEAD_DIM, acc_ref[0], acc_ref[1]).astype(o_ref.dtype)


def _sb_attention(qkv, n_heads):
    b, s, _ = qkv.shape
    nblk = n_heads // HEADS_PER_LANE_BLOCK
    tile = ATTN_TILE
    assert s % tile == 0
    return pl.pallas_call(
        functools.partial(_sb_attention_kernel, tile=tile, scale=HEAD_DIM ** -0.5),
        grid=(b, nblk, s // tile),
        in_specs=[pl.BlockSpec((1, tile, LANES), lambda bi, hp, i: (bi, i, hp)),
                  pl.BlockSpec((1, s, LANES), lambda bi, hp, i: (bi, 0, nblk + hp)),
                  pl.BlockSpec((1, s, LANES), lambda bi, hp, i: (bi, 0, 2 * nblk + hp))],
        out_specs=pl.BlockSpec((1, tile, LANES), lambda bi, hp, i: (bi, i, hp)),
        out_shape=jax.ShapeDtypeStruct((b, s, n_heads * HEAD_DIM), BF16),
        scratch_shapes=[pltpu.VMEM((HEADS_PER_LANE_BLOCK, tile, LANES), F32),
                        pltpu.VMEM((HEADS_PER_LANE_BLOCK, tile, 1), F32)],
        compiler_params=_compiler_params(("arbitrary", "arbitrary", "arbitrary")),
        name="sb_attention",
    )(qkv, qkv, qkv)


def _even_out_kernel(sb_ref, pool_ref, prev_ref, pw_ref, ps_ref, wo_ref, x_ref, g_ref, o_ref,
                     *, tm, tiles_per_seq):
    i = pl.program_id(0)
    t0 = (i % tiles_per_seq) * tm
    t = t0 + lax.broadcasted_iota(I32, (tm, 1), 0)
    s = t0 - tm + lax.broadcasted_iota(I32, (1, 2 * tm), 1)
    chunk_end = ((t // CHUNK) + 1) * CHUNK
    group = LANES
    acc = jnp.dot(sb_ref[...], wo_ref[0:4 * group, :], preferred_element_type=F32)
    for g, w in enumerate(POOL_WINDOWS):
        lo = jnp.maximum(t - w // 2, 0)
        hi = jnp.minimum(t + (w - w // 2), chunk_end)
        band = ((s >= lo) & (s < hi)).astype(BF16)
        cur = pool_ref[:, g * group:(g + 1) * group]
        ext = jnp.concatenate([prev_ref[:, g * group:(g + 1) * group], cur], axis=0)
        window_sum = _dot_exact_lhs(band, ext)
        pooled = window_sum / (hi - lo).astype(F32) - cur
        y = jnp.dot(pooled.astype(BF16), pw_ref[g], preferred_element_type=F32)
        y = y * ps_ref[:, g * group:(g + 1) * group]
        acc += jnp.dot(y.astype(BF16), wo_ref[(4 + g) * group:(5 + g) * group, :],
                       preferred_element_type=F32)
    o_ref[...] = x_ref[...] + g_ref[0] * acc


def _even_out(sb2d, pool2d, seq, pool_w_bf16, pool_scale, w_out_bf16, x2d, gate):
    t, d = x2d.shape
    tm = TOKEN_TILE
    tiles_per_seq = seq // tm
    assert seq % CHUNK == 0 and len(POOL_WINDOWS) * LANES == pool2d.shape[1]
    pw = pool2d.shape[1]
    return pl.pallas_call(
        functools.partial(_even_out_kernel, tm=tm, tiles_per_seq=tiles_per_seq),
        grid=(t // tm,),
        in_specs=[pl.BlockSpec((tm, sb2d.shape[1]), lambda i: (i, 0)),
                  pl.BlockSpec((tm, pw), lambda i: (i, 0)),
                  pl.BlockSpec((tm, pw), lambda i: (jnp.maximum(i - 1, 0), 0)),
                  pl.BlockSpec(pool_w_bf16.shape, lambda i: (0, 0, 0)),
                  pl.BlockSpec((1, pw), lambda i: (0, 0)),
                  pl.BlockSpec((d, d), lambda i: (0, 0)),
                  pl.BlockSpec((tm, d), lambda i: (i, 0)),
                  pl.BlockSpec((1, 1, d), lambda i: (i // tiles_per_seq, 0, 0))],
        out_specs=pl.BlockSpec((tm, d), lambda i: (i, 0)),
        out_shape=jax.ShapeDtypeStruct((t, d), F32),
        compiler_params=_compiler_params(("arbitrary",)),
        name="even_out",
    )(sb2d, pool2d, pool2d, pool_w_bf16, pool_scale.reshape(1, pw), w_out_bf16, x2d, gate)


def _odd_prep_kernel(qk_ref, f_ref, rw_ref, rwprev_ref, hs_ref, tri_ref, ctri_ref,
                     qg_ref, kg_ref, fb_ref, mix_ref, w0_ref, w2_ref, a0_ref, a2_ref, g2_ref,
                     kkp_ref, ka_ref, rk_ref,
                     qn_ref, kn_ref, dcol_ref, drow_ref, rt_ref, kt_ref, at_ref, bt_ref, v_ref,
                     cw_ref, bonus_ref, g_ref, carry_ref, *, tm, tiles_per_seq, width):
    i = pl.program_id(0)
    first = (i % tiles_per_seq) == 0
    hs = hs_ref[...]

    def head_sum(x):
        return _dot_exact_rhs(x, hs, 2)

    inv_hd = 1.0 / HEAD_DIM
    q = qk_ref[:, :width]
    k = qk_ref[:, width:]
    qn_ref[...] = (q * lax.rsqrt(head_sum(q * q) * inv_hd + NORM_EPS) * qg_ref[...]).astype(qn_ref.dtype)
    kn_ref[...] = (k * lax.rsqrt(head_sum(k * k) * inv_hd + NORM_EPS) * kg_ref[...]).astype(kn_ref.dtype)

    log_f = _log_sigmoid(f_ref[...] + fb_ref[...])

    @pl.when(first)
    def _():
        carry_ref[...] = jnp.zeros_like(carry_ref)

    dcum = _dot_exact_lhs(tri_ref[...], log_f) + carry_ref[...]
    carry_ref[...] = dcum[tm - 1:tm, :]
    dcol_ref[...] = dcum
    drow_ref[0, 0] = jnp.transpose(dcum)[:SUBLANES, :]

    pd = rw_ref[...]
    prev_row = jnp.where(first, 0.0, rwprev_ref[SUBLANES - 1:SUBLANES, :])
    row0 = lax.broadcasted_iota(I32, (tm, 1), 0) == 0
    prev = jnp.where(row0, prev_row, pltpu.roll(pd, 1, axis=0))
    pd = pd + (prev - pd) * mix_ref[...]
    r = pd[:, :width]
    kr = pd[:, width:2 * width]
    vr = pd[:, 2 * width:3 * width]
    lora_in = pd[:, 3 * width:3 * width + LANES]
    xg = pd[:, 3 * width + LANES:3 * width + 2 * LANES]
    w_log = _log_sigmoid(w0_ref[...] + jnp.dot(jnp.tanh(lora_in).astype(BF16), w2_ref[...],
                                              preferred_element_type=F32)) - 0.5
    log_w = -jnp.exp(w_log)
    a = _sigmoid(a0_ref[...] + jnp.dot(lora_in.astype(BF16), a2_ref[...], preferred_element_type=F32))
    g_ref[...] = jnp.dot(_sigmoid(xg).astype(BF16), g2_ref[...], preferred_element_type=F32)
    kk = kr * kkp_ref[...]
    kk = kk / jnp.maximum(jnp.sqrt(head_sum(kk * kk)), 1e-12)
    k2 = kr * (1.0 + (a - 1.0) * ka_ref[...])
    bonus_ref[...] = head_sum(r * k2 * rk_ref[...]) * vr
    cw = _dot_exact_lhs(ctri_ref[...], log_w)
    grow = jnp.exp(-cw)
    rt_ref[...] = r * jnp.exp(cw)
    kt_ref[...] = k2 * grow
    bt_ref[...] = kk * a * grow
    at_ref[...] = -kk * jnp.exp(cw - log_w)
    v_ref[...] = vr
    cw_ref[...] = cw


def _odd_prep(qk, fproj, rw, seq, p):
    t = qk.shape[0]
    width = qk.shape[1] // 2
    tm = TOKEN_TILE
    tiles_per_seq = seq // tm
    nrw = rw.shape[1]
    assert tm % CHUNK == 0 and seq % tm == 0 and nrw == 3 * width + 2 * LANES
    ri = lax.broadcasted_iota(I32, (tm, tm), 0)
    ci = lax.broadcasted_iota(I32, (tm, tm), 1)
    tri = (ci <= ri).astype(BF16)
    ctri = ((ci <= ri) & (ci // CHUNK == ri // CHUNK)).astype(BF16)
    hs = _head_sum_matrix(width)
    row = lambda a: a.reshape(1, -1)
    full = lambda a: pl.BlockSpec(a.shape, lambda i: (0,) * a.ndim)
    tile = lambda n: pl.BlockSpec((tm, n), lambda i: (i, 0))
    consts = [hs, tri, ctri, row(p['qg']), row(p['kg']), row(p['fb']), row(p['mix']), row(p['w0']),
              p['w2'], row(p['a0']), p['a2'], p['g2'], row(p['kk']), row(p['ka']), row(p['rk'])]
    wide = jax.ShapeDtypeStruct((t, width), F32)
    return pl.pallas_call(
        functools.partial(_odd_prep_kernel, tm=tm, tiles_per_seq=tiles_per_seq, width=width),
        grid=(t // tm,),
        in_specs=[tile(2 * width), tile(LANES), tile(nrw),
                  pl.BlockSpec((SUBLANES, nrw), lambda i: (jnp.maximum(i * (tm // SUBLANES) - 1, 0), 0))]
                 + [full(c) for c in consts],
        out_specs=[tile(width), tile(width), tile(LANES),
                   pl.BlockSpec((1, 1, SUBLANES, tm), lambda i: (i // tiles_per_seq, i % tiles_per_seq, 0, 0))]
                  + [tile(width)] * 8,
        out_shape=[jax.ShapeDtypeStruct((t, width), BF16), jax.ShapeDtypeStruct((t, width), BF16),
                   jax.ShapeDtypeStruct((t, LANES), F32),
                   jax.ShapeDtypeStruct((t // seq, tiles_per_seq, SUBLANES, tm), F32)] + [wide] * 8,
        scratch_shapes=[pltpu.VMEM((1, LANES), F32)],
        compiler_params=_compiler_params(("arbitrary",)),
        name="odd_prep",
    )(qk, fproj, rw, rw, *consts)


def _fox_attention_kernel(bound_ref, q_ref, k_ref, v_ref, dcol_ref, drow_ref, o_ref, acc_ref, m_ref,
                          l_ref, *, tile, scale):
    hp = pl.program_id(1)
    i = pl.program_id(2)
    qk_bound = bound_ref[0, 0]
    lane = lax.broadcasted_iota(I32, (1, LANES), 1)
    row = lax.broadcasted_iota(I32, (tile, tile), 0)
    col = lax.broadcasted_iota(I32, (tile, tile), 1)
    causal = col <= row
    sub = lax.broadcasted_iota(I32, (SUBLANES, tile), 0)
    q = q_ref[0]
    dq_all = dcol_ref[0]
    hs = range(HEADS_PER_LANE_BLOCK)
    heads = [hp * HEADS_PER_LANE_BLOCK + h for h in hs]
    qh = [jnp.where((lane >= HEAD_DIM * h) & (lane < HEAD_DIM * (h + 1)), q, jnp.zeros_like(q)) for h in hs]
    dq = [jnp.sum(jnp.where(lane == heads[h], dq_all, 0.0), axis=1, keepdims=True) for h in hs]

    def key_bias(j, h):
        return jnp.sum(jnp.where(sub == heads[h], drow_ref[0, j], 0.0), axis=0, keepdims=True)

    def block(j, diag):
        start = pl.multiple_of(j * tile, tile)
        kb = k_ref[0, pl.ds(start, tile), :]
        vb = v_ref[0, pl.ds(start, tile), :]
        s = [lax.dot_general(qh[h], kb, (((1,), (1,)), ((), ())), preferred_element_type=F32) * scale
             + dq[h] - key_bias(j, h) for h in hs]
        if diag:
            s = [jnp.where(causal, x, -jnp.inf) for x in s]
        m_prev = [m_ref[h] for h in hs]
        m_new = [jnp.maximum(m_prev[h], jnp.max(s[h], axis=1, keepdims=True)) for h in hs]
        p = [jnp.exp(s[h] - m_new[h]) for h in hs]
        alpha = [jnp.exp(m_prev[h] - m_new[h]) for h in hs]
        for h in hs:
            l_ref[h] = alpha[h] * l_ref[h] + jnp.sum(p[h], axis=1, keepdims=True)
            acc_ref[h] = alpha[h] * acc_ref[h] + jnp.dot(p[h].astype(BF16), vb, preferred_element_type=F32)
            m_ref[h] = m_new[h]

    acc_ref[...] = jnp.zeros_like(acc_ref)
    l_ref[...] = jnp.zeros_like(l_ref)
    m_ref[...] = jnp.full_like(m_ref, -jnp.inf)
    block(i, True)

    def alive(j):
        jc = jnp.maximum(j, 0)
        gaps = [qk_bound + jnp.max(dq[h] - m_ref[h]) - jnp.min(key_bias(jc, h)) for h in hs]
        return functools.reduce(jnp.maximum, gaps) > F32_EXP_UNDERFLOW

    def cond(c):
        return jnp.logical_and(c[0] < i, c[1])

    def body(c):
        j = i - 1 - c[0]
        block(j, False)
        return c[0] + 1, alive(j - 1)

    lax.while_loop(cond, body, (jnp.int32(0), alive(i - 1)))
    o_ref[0] = jnp.where(lane < HEAD_DIM, acc_ref[0] / l_ref[0], acc_ref[1] / l_ref[1])


def _fox_attention(qn, kn, v, dcol, drow, q_gain, k_gain):
    b, s, w = qn.shape
    nblk = w // LANES
    tile = ATTN_TILE
    assert drow.shape == (b, s // tile, SUBLANES, tile) and w // HEAD_DIM <= SUBLANES
    scale = HEAD_DIM ** -0.5
    bf16_slack = (1.0 + 2.0 ** -8) ** 2
    bound = (HEAD_DIM * scale * bf16_slack) * jnp.max(jnp.abs(q_gain)) * jnp.max(jnp.abs(k_gain))
    return pl.pallas_call(
        functools.partial(_fox_attention_kernel, tile=tile, scale=scale),
        grid=(b, nblk, s // tile),
        in_specs=[pl.BlockSpec(memory_space=pltpu.SMEM),
                  pl.BlockSpec((1, tile, LANES), lambda bi, hp, i: (bi, i, hp)),
                  pl.BlockSpec((1, s, LANES), lambda bi, hp, i: (bi, 0, hp)),
                  pl.BlockSpec((1, s, LANES), lambda bi, hp, i: (bi, 0, hp)),
                  pl.BlockSpec((1, tile, LANES), lambda bi, hp, i: (bi, i, 0)),
                  pl.BlockSpec((1, s // tile, SUBLANES, tile), lambda bi, hp, i: (bi, 0, 0, 0))],
        out_specs=pl.BlockSpec((1, tile, LANES), lambda bi, hp, i: (bi, i, hp)),
        out_shape=jax.ShapeDtypeStruct((b, s, w), F32),
        scratch_shapes=[pltpu.VMEM((HEADS_PER_LANE_BLOCK, tile, LANES), F32),
                        pltpu.VMEM((HEADS_PER_LANE_BLOCK, tile, 1), F32),
                        pltpu.VMEM((HEADS_PER_LANE_BLOCK, tile, 1), F32)],
        compiler_params=_compiler_params(("arbitrary", "arbitrary", "arbitrary")),
        name="fox_attention",
    )(bound.reshape(1, 1).astype(F32), qn, kn, v, dcol, drow)


_NN = (((1,), (0,)), ((), ()))
_NT = (((1,), (1,)), ((), ()))
_TN = (((0,), (0,)), ((), ()))


def _mm(a, b, dims=_NN):
    a_hi, a_lo = _split_bf16(a, 2)
    b_hi, b_lo = _split_bf16(b, 2)
    dot = lambda x, y: lax.dot_general(x, y, dims, preferred_element_type=F32)
    return dot(a_hi, b_hi) + (dot(a_hi, b_lo) + dot(a_lo, b_hi))


def _head_cols(h):
    return slice(h * HEAD_DIM, (h + 1) * HEAD_DIM)


def _rwkv_intra_kernel(rt_ref, kt_ref, at_ref, bt_ref, v_ref, inv_ref, arb_ref, fvu_ref, fvy_ref,
                       *, chunks, heads):
    c = CHUNK
    ri = lax.broadcasted_iota(I32, (2 * c, 2 * c), 0)
    ci = lax.broadcasted_iota(I32, (2 * c, 2 * c), 1)
    keep = (ci % c) < jnp.where(ri < c, ri, ri - c + 1)
    eye = (lax.broadcasted_iota(I32, (c, c), 0) == lax.broadcasted_iota(I32, (c, c), 1)).astype(F32)
    hs = range(heads)

    def chunk(cidx, carry):
        rows = pl.ds(pl.multiple_of(cidx * c, c), c)
        a_all = [jnp.where(keep,
                           _mm(jnp.concatenate([at_ref[rows, _head_cols(h)], rt_ref[rows, _head_cols(h)]], axis=0),
                               jnp.concatenate([bt_ref[rows, _head_cols(h)], kt_ref[rows, _head_cols(h)]], axis=0),
                               _NT), 0.0) for h in hs]
        a_ab = [a[:c, :c] for a in a_all]
        inv = [eye + a for a in a_ab]
        power = [_mm(a, a) for a in a_ab]
        for _ in range(4):
            both = [_mm(jnp.concatenate([inv[h], power[h]], axis=0), power[h]) for h in hs]
            inv = [inv[h] + both[h][:c] for h in hs]
            power = [both[h][c:] for h in hs]
        inv = [inv[h] + _mm(inv[h], power[h]) for h in hs]
        from_v = [_mm(a_all[h][:, c:], v_ref[rows, _head_cols(h)]) for h in hs]
        inv_ref[rows, :] = jnp.concatenate(inv, axis=1)
        arb_ref[rows, :] = jnp.concatenate([a[c:, :c] for a in a_all], axis=1)
        fvu_ref[rows, :] = jnp.concatenate([f[:c] for f in from_v], axis=1)
        fvy_ref[rows, :] = jnp.concatenate([f[c:] for f in from_v], axis=1)
        return carry

    lax.fori_loop(0, chunks, chunk, 0)


def _rwkv_scan_kernel(rt_ref, kt_ref, at_ref, bt_ref, v_ref, cw_ref, inv_ref, arb_ref, fvu_ref, fvy_ref,
                      y_ref, st_ref, *, steps_per_seq, chunks, heads):
    i = pl.program_id(0)

    @pl.when(i % steps_per_seq == 0)
    def _():
        st_ref[...] = jnp.zeros_like(st_ref)

    c = CHUNK
    hs = range(heads)

    def chunk(cidx, carry):
        rows = pl.ds(pl.multiple_of(cidx * c, c), c)
        last = pl.ds(pl.multiple_of(cidx * c, c) + (c - SUBLANES), SUBLANES)
        ar = [jnp.concatenate([at_ref[rows, _head_cols(h)], rt_ref[rows, _head_cols(h)]], axis=0) for h in hs]
        bk = [jnp.concatenate([bt_ref[rows, _head_cols(h)], kt_ref[rows, _head_cols(h)]], axis=0) for h in hs]
        st = [st_ref[h] for h in hs]
        from_state = [_mm(ar[h], st[h], _NT) for h in hs]
        u = [_mm(inv_ref[rows, _head_cols(h)], from_state[h][:c] + fvu_ref[rows, _head_cols(h)]) for h in hs]
        y = [from_state[h][c:] + fvy_ref[rows, _head_cols(h)] + _mm(arb_ref[rows, _head_cols(h)], u[h])
             for h in hs]
        for h in hs:
            decay = jnp.exp(cw_ref[last, _head_cols(h)][SUBLANES - 1:SUBLANES, :])
            uv = jnp.concatenate([u[h], v_ref[rows, _head_cols(h)]], axis=0)
            st_ref[h] = (st[h] + _mm(uv, bk[h], _TN)) * decay
        y_ref[rows, :] = jnp.concatenate(y, axis=1)
        return carry

    lax.fori_loop(0, chunks, chunk, 0)


def _rwkv_scan(rt, kt, at, bt, v, cw, seq):
    t, w = rt.shape
    heads = w // HEAD_DIM
    ts = 2 * TOKEN_TILE
    assert seq % ts == 0
    spec = pl.BlockSpec((ts, w), lambda i: (i, 0))
    wide = jax.ShapeDtypeStruct((t, w), F32)
    inv, arb, fvu, fvy = pl.pallas_call(
        functools.partial(_rwkv_intra_kernel, chunks=ts // CHUNK, heads=heads),
        grid=(t // ts,),
        in_specs=[spec] * 5,
        out_specs=[spec] * 4,
        out_shape=[wide] * 4,
        compiler_params=_compiler_params(("arbitrary",)),
        name="rwkv_intra",
    )(rt, kt, at, bt, v)
    return pl.pallas_call(
        functools.partial(_rwkv_scan_kernel, steps_per_seq=seq // ts, chunks=ts // CHUNK, heads=heads),
        grid=(t // ts,),
        in_specs=[spec] * 10,
        out_specs=spec,
        out_shape=wide,
        scratch_shapes=[pltpu.VMEM((heads, HEAD_DIM, HEAD_DIM), F32)],
        compiler_params=_compiler_params(("arbitrary",)),
        name="rwkv_scan",
    )(rt, kt, at, bt, v, cw, inv, arb, fvu, fvy)


def _odd_out_kernel(att_ref, og_ref, y_ref, bonus_ref, g_ref, hs_ref, lnw_ref, lnb_ref, wo_ref,
                    x_ref, gate_ref, o_ref, *, width):
    hs = hs_ref[...]
    inv_hd = 1.0 / HEAD_DIM
    fox = att_ref[...] * _sigmoid(og_ref[...])
    y = y_ref[...]
    mu = _dot_exact_rhs(y, hs, 2) * inv_hd
    yc = y - mu
    var = _dot_exact_rhs(yc * yc, hs, 2) * inv_hd
    yn = yc * lax.rsqrt(var + RWKV_GN_EPS) * lnw_ref[...] + lnb_ref[...]
    rw = (yn + bonus_ref[...]) * g_ref[...]
    acc = jnp.dot(fox.astype(BF16), wo_ref[:width, :], preferred_element_type=F32)
    acc += jnp.dot(rw.astype(BF16), wo_ref[width:, :], preferred_element_type=F32)
    o_ref[...] = x_ref[...] + gate_ref[0] * acc


def _odd_out(att, og, y, bonus, g, ln_w, ln_b, w_out_bf16, x2d, seq, gate):
    t, d = x2d.shape
    width = att.shape[1]
    tm = TOKEN_TILE
    tiles_per_seq = seq // tm
    hs = _head_sum_matrix(width)
    tile = lambda n: pl.BlockSpec((tm, n), lambda i: (i, 0))
    return pl.pallas_call(
        functools.partial(_odd_out_kernel, width=width),
        grid=(t // tm,),
        in_specs=[tile(width)] * 5
                 + [pl.BlockSpec(hs.shape, lambda i: (0, 0)),
                    pl.BlockSpec((1, width), lambda i: (0, 0)), pl.BlockSpec((1, width), lambda i: (0, 0)),
                    pl.BlockSpec((d, d), lambda i: (0, 0)), tile(d),
                    pl.BlockSpec((1, 1, d), lambda i: (i // tiles_per_seq, 0, 0))],
        out_specs=tile(d),
        out_shape=jax.ShapeDtypeStruct((t, d), F32),
        compiler_params=_compiler_params(("arbitrary",)),
        name="odd_out",
    )(att, og, y, bonus, g, hs, ln_w.reshape(1, width), ln_b.reshape(1, width), w_out_bf16, x2d, gate)


def _top_rows(s, k, prio, payload=None):
    big = jnp.int32(2 ** 30)
    vals, picks = [], []
    for _ in range(k):
        m = jnp.max(s, axis=0, keepdims=True)
        am = jnp.min(jnp.where(s == m, prio, big), axis=0, keepdims=True)
        hit = prio == am
        vals.append(m)
        if payload is None:
            picks.append(am)
        else:
            picks.append(jnp.max(jnp.where(hit, payload, -1), axis=0, keepdims=True))
        s = jnp.where(hit, -jnp.inf, s)
    return jnp.concatenate(vals, axis=0), jnp.concatenate(picks, axis=0)


def _staircase_groups():
    k, g = PEER_TOPK, SUBLANES
    groups = []
    for b in range(k):
        count = k // (b + 1)
        if count <= 1:
            break
        for a0 in range(0, count, g):
            groups.append((a0, b, False, min(g, count - a0)))
    first_single = next(b for b in range(k) if k // (b + 1) <= 1)
    for b0 in range(first_single, k, g):
        groups.append((0, b0, True, min(g, k - b0)))
    return groups


def _peer_topk_kernel(q_ref, k1_ref, k2_ref, idx_ref, gate_ref, *, row_scale):
    half = PEER_NKEYS
    nt = (((1,), (1,)), ((), ()))
    tt = q_ref.shape[0]
    key_id = lax.broadcasted_iota(I32, (half, tt), 0)
    sub = lax.broadcasted_iota(I32, (SUBLANES, 1), 0)
    groups = _staircase_groups()
    prio = jnp.concatenate(
        [jnp.broadcast_to((sub * 0 + a0) * PEER_TOPK + b0 + sub if along_b
                          else (a0 + sub) * PEER_TOPK + b0, (SUBLANES, tt))
         for a0, b0, along_b, _ in groups], axis=0)
    idx_rows = []
    for h in range(PEER_HEADS):
        q1 = q_ref[:, (2 * h) * half:(2 * h + 1) * half]
        q2 = q_ref[:, (2 * h + 1) * half:(2 * h + 2) * half]
        s1 = lax.dot_general(k1_ref[...], q1, nt, precision=HIGHEST, preferred_element_type=F32)
        s2 = lax.dot_general(k2_ref[...], q2, nt, precision=HIGHEST, preferred_element_type=F32)
        v1, i1 = _top_rows(s1, PEER_TOPK, key_id)
        v2, i2 = _top_rows(s2, PEER_TOPK, key_id)
        cand, cand_idx = [], []
        for a0, b0, along_b, valid in groups:
            if along_b:
                val = v1[a0:a0 + 1] + v2[b0:b0 + SUBLANES]
                eid = i1[a0:a0 + 1] * PEER_NKEYS + i2[b0:b0 + SUBLANES]
            else:
                val = v1[a0:a0 + SUBLANES] + v2[b0:b0 + 1]
                eid = i1[a0:a0 + SUBLANES] * PEER_NKEYS + i2[b0:b0 + 1]
            cand.append(jnp.where(sub < valid, val, -jnp.inf))
            cand_idx.append(eid)
        top_s, idx = _top_rows(jnp.concatenate(cand, axis=0), PEER_TOPK, prio,
                               payload=jnp.concatenate(cand_idx, axis=0))
        e = jnp.exp(top_s - top_s[0:1])
        gate_ref[h * PEER_TOPK:(h + 1) * PEER_TOPK, :] = e / jnp.sum(e, axis=0, keepdims=True)
        idx_rows.append(idx)
    idx_ref[...] = jnp.transpose(jnp.concatenate(idx_rows, axis=0)) * row_scale


def _peer_topk(q2d, k1, k2, row_scale):
    t, n = q2d.shape
    tt = TOKEN_TILE
    assert PEER_TOPK % SUBLANES == 0
    return pl.pallas_call(
        functools.partial(_peer_topk_kernel, row_scale=row_scale),
        grid=(t // tt,),
        in_specs=[pl.BlockSpec((tt, n), lambda i: (i, 0)),
                  pl.BlockSpec(k1.shape, lambda i: (0, 0)),
                  pl.BlockSpec(k2.shape, lambda i: (0, 0))],
        out_specs=[pl.BlockSpec((tt, PEER_SLOTS), lambda i: (i, 0)),
                   pl.BlockSpec((PEER_SLOTS, tt), lambda i: (0, i))],
        out_shape=[jax.ShapeDtypeStruct((t, PEER_SLOTS), I32),
                   jax.ShapeDtypeStruct((PEER_SLOTS, t), F32)],
        compiler_params=_compiler_params(("arbitrary",)),
        name="peer_topk",
    )(q2d, k1, k2)


def _pack_table_kernel(t_ref, o_ref, *, pairs):
    high = jnp.int32(-65536)
    for j in range(pairs):
        lo = t_ref[:, (2 * j) * LANES:(2 * j + 1) * LANES].astype(BF16).astype(F32)
        hi = t_ref[:, (2 * j + 1) * LANES:(2 * j + 2) * LANES].astype(BF16).astype(F32)
        lo_bits = lax.shift_right_logical(pltpu.bitcast(lo, I32), 16)
        o_ref[:, j * LANES:(j + 1) * LANES] = lax.bitwise_or(lo_bits, lax.bitwise_and(pltpu.bitcast(hi, I32), high))


def _pack_table(table):
    e, d = table.shape
    pairs = d // (2 * LANES)
    rows = 2 * TOKEN_TILE
    words = pl.pallas_call(
        functools.partial(_pack_table_kernel, pairs=pairs),
        grid=(e // rows,),
        in_specs=[pl.BlockSpec((rows, d), lambda i: (i, 0))],
        out_specs=pl.BlockSpec((rows, pairs * LANES), lambda i: (i, 0)),
        out_shape=jax.ShapeDtypeStruct((e, pairs * LANES), I32),
        compiler_params=_compiler_params(("arbitrary",)),
        name="pack_table",
    )(table)
    return words.reshape(e * pairs, LANES)


def _unpack_words(w):
    lo = pltpu.bitcast(lax.shift_left(w, 16), F32)
    hi = pltpu.bitcast(lax.bitwise_and(w, jnp.int32(-65536)), F32)
    return lo, hi


def _gather_rows(idx_ref, tab_ref, tile_ref, tok, pairs):
    for m in range(PEER_SLOTS):
        row = pl.multiple_of(idx_ref[tok, m], pairs)
        tile_ref[m * pairs:(m + 1) * pairs, :] = tab_ref[pl.ds(row, pairs), :]


def _tile_chunk(tile_ref, j, pairs):
    return _unpack_words(tile_ref[pl.ds(j, PEER_SLOTS, stride=pairs), :])


PEER_TILE_BUFFERS = 8


def _pipelined_tokens(ntok, gather, compute, init):
    nb = PEER_TILE_BUFFERS
    ahead = nb // 2
    assert ntok % nb == 0
    for k in range(ahead):
        gather(k, k)

    def trip(p, carry):
        tok = nb * p
        for k in range(nb):
            gather(jnp.minimum(tok + k + ahead, ntok - 1), (k + ahead) % nb)
            carry = compute(tok + k, k, carry)
        return carry

    return lax.fori_loop(0, ntok // nb, trip, init)


def _pick_row(block, r):
    sub = lax.broadcasted_iota(I32, block.shape, 0)
    return jnp.sum(jnp.where(sub == r, block, 0.0), axis=0, keepdims=True)


def _peer_act_kernel(idx_ref, tab_ref, h_ref, gate_ref, coef_ref, *tiles, pairs):
    ntok = h_ref.shape[0]
    lane = lax.broadcasted_iota(I32, (PEER_SLOTS, ntok), 1)

    def gather(tok, buf):
        _gather_rows(idx_ref, tab_ref, tiles[buf], tok, pairs)

    def compute(tok, buf, carry):
        base = pl.multiple_of((tok // SUBLANES) * SUBLANES, SUBLANES)
        h_row = _pick_row(h_ref[pl.ds(base, SUBLANES), :], tok % SUBLANES)
        terms = []
        for j in range(pairs):
            lo, hi = _tile_chunk(tiles[buf], j, pairs)
            h_lo = h_row[:, (2 * j) * LANES:(2 * j + 1) * LANES]
            h_hi = h_row[:, (2 * j + 1) * LANES:(2 * j + 2) * LANES]
            terms.append(lo * h_lo + hi * h_hi)
        while len(terms) > 1:
            terms = [a + b for a, b in zip(terms[::2], terms[1::2])]
        col = jnp.sum(terms[0], axis=1, keepdims=True)
        coef_ref[...] = jnp.where(lane == tok, col, coef_ref[...])
        return carry

    coef_ref[...] = jnp.zeros_like(coef_ref)
    _pipelined_tokens(ntok, gather, compute, 0)
    act_t = coef_ref[...]
    gelu = 0.5 * act_t * (1.0 + lax.erf(act_t * (2.0 ** -0.5)))
    coef_ref[...] = gate_ref[...] * gelu


def _peer_act(idx, table_words, h2d, gate_t):
    t, d = h2d.shape
    pairs = d // (2 * LANES)
    nt = PEER_TOKENS
    return pl.pallas_call(
        functools.partial(_peer_act_kernel, pairs=pairs),
        grid=(t // nt,),
        in_specs=[pl.BlockSpec((nt, PEER_SLOTS), lambda i: (i, 0), memory_space=pltpu.SMEM),
                  pl.BlockSpec(table_words.shape, lambda i: (0, 0), pipeline_mode=pl.Buffered(1)),
                  pl.BlockSpec((nt, d), lambda i: (i, 0)),
                  pl.BlockSpec((PEER_SLOTS, nt), lambda i: (0, i))],
        out_specs=pl.BlockSpec((PEER_SLOTS, nt), lambda i: (0, i)),
        out_shape=jax.ShapeDtypeStruct((PEER_SLOTS, t), F32),
        scratch_shapes=[pltpu.VMEM((pairs * PEER_SLOTS, LANES), I32)] * PEER_TILE_BUFFERS,
        compiler_params=_compiler_params(("arbitrary",), vmem_mb=56),
        name="peer_act",
    )(idx, table_words, h2d, gate_t)


def _peer_out_kernel(idx_ref, tab_ref, coef_ref, x_ref, g_ref, o_ref, *tiles, pairs):
    ntok, d = x_ref.shape
    lane = lax.broadcasted_iota(I32, (PEER_SLOTS, ntok), 1)
    sub = lax.broadcasted_iota(I32, (SUBLANES, d), 0)
    groups = PEER_SLOTS // SUBLANES

    def gather(tok, buf):
        _gather_rows(idx_ref, tab_ref, tiles[buf], tok, pairs)

    def compute(tok, buf, rows):
        coef = jnp.sum(jnp.where(lane == tok, coef_ref[...], 0.0), axis=1, keepdims=True)
        pieces = []
        for j in range(pairs):
            for vals in _tile_chunk(tiles[buf], j, pairs):
                prod = (vals * coef).reshape(groups, SUBLANES, LANES)
                pieces.append(jnp.sum(jnp.sum(prod, axis=0), axis=0, keepdims=True))
        row = jnp.concatenate(pieces, axis=1)
        rows = jnp.where(sub == tok % SUBLANES, row, rows)
        base = pl.multiple_of((tok // SUBLANES) * SUBLANES, SUBLANES)
        o_ref[pl.ds(base, SUBLANES), :] = x_ref[pl.ds(base, SUBLANES), :] + g_ref[0] * rows
        return rows

    _pipelined_tokens(ntok, gather, compute, jnp.zeros((SUBLANES, d), F32))


def _peer_out(idx, table_words, coef_t, x2d, seq, gate):
    t, d = x2d.shape
    pairs = d // (2 * LANES)
    nt = PEER_TOKENS
    per_seq = seq // nt
    return pl.pallas_call(
        functools.partial(_peer_out_kernel, pairs=pairs),
        grid=(t // nt,),
        in_specs=[pl.BlockSpec((nt, PEER_SLOTS), lambda i: (i, 0), memory_space=pltpu.SMEM),
                  pl.BlockSpec(table_words.shape, lambda i: (0, 0), pipeline_mode=pl.Buffered(1)),
                  pl.BlockSpec((PEER_SLOTS, nt), lambda i: (0, i)),
                  pl.BlockSpec((nt, d), lambda i: (i, 0)),
                  pl.BlockSpec((1, 1, d), lambda i: (i // per_seq, 0, 0))],
        out_specs=pl.BlockSpec((nt, d), lambda i: (i, 0)),
        out_shape=jax.ShapeDtypeStruct((t, d), F32),
        scratch_shapes=[pltpu.VMEM((pairs * PEER_SLOTS, LANES), I32)] * PEER_TILE_BUFFERS,
        compiler_params=_compiler_params(("arbitrary",), vmem_mb=56),
        name="peer_out",
    )(idx, table_words, coef_t, x2d, gate)


def _peer_ffn(x2d, seq, gain, shift, scale, gate, wq_bf16, k1, k2, u_words, v_words):
    n = wq_bf16.shape[1]
    q, h = _norm_proj(x2d, seq, gain, shift, scale, wq_bf16, [(0, n)], [F32], emit_h=True)
    idx, gate_t = _peer_topk(q, k1, k2, x2d.shape[1] // (2 * LANES))
    coef_t = _peer_act(idx, u_words, h, gate_t)
    return _peer_out(idx, v_words, coef_t, x2d, seq, gate)


def _final_norm_kernel(x_ref, gain_ref, o_ref):
    x = x_ref[...]
    ms = jnp.mean(x * x, axis=-1, keepdims=True)
    o_ref[...] = x * lax.rsqrt(ms + NORM_EPS) * gain_ref[...]


def _final_norm(x2d, gain):
    t, d = x2d.shape
    tm = TOKEN_TILE
    return pl.pallas_call(
        _final_norm_kernel,
        grid=(t // tm,),
        in_specs=[pl.BlockSpec((tm, d), lambda i: (i, 0)), pl.BlockSpec((1, d), lambda i: (0, 0))],
        out_specs=pl.BlockSpec((tm, d), lambda i: (i, 0)),
        out_shape=jax.ShapeDtypeStruct((t, d), F32),
        compiler_params=_compiler_params(("arbitrary",)),
        name="final_norm",
    )(x2d, gain.reshape(1, d))


def _even_mixer(x2d, b, seq, gain, shift, scale, gate, w_in, w_out, pool_w, pool_scale):
    d = x2d.shape[1]
    pool_width = len(POOL_WINDOWS) * LANES
    sb_width = (w_in.shape[1] - pool_width) // 3
    qkv, pool = _norm_proj(x2d, seq, gain, shift, scale, w_in.astype(BF16),
                           [(0, 3 * sb_width), (3 * sb_width, pool_width)], [BF16, F32])
    sb = _sb_attention(qkv.reshape(b, seq, 3 * sb_width), sb_width // HEAD_DIM)
    return _even_out(sb.reshape(b * seq, sb_width), pool, seq, pool_w.astype(BF16), pool_scale,
                     w_out.astype(BF16), x2d, gate)


def _odd_mixer(x2d, b, seq, gain, shift, scale, gate, w_in, w_out, fox_qnorm, fox_knorm, fox_fbias,
               rwkv_mix, rwkv_w0, rwkv_w2, rwkv_a0, rwkv_a2, rwkv_g2, rwkv_kk, rwkv_ka, rwkv_rk,
               rwkv_ln_w, rwkv_ln_b):
    d = x2d.shape[1]
    width = rwkv_w0.shape[0]
    heads = width // HEAD_DIM
    fox_in = 4 * width + heads
    n_rw = w_in.shape[1] - fox_in
    w_cols = jnp.concatenate([w_in[:, :4 * width], w_in[:, fox_in:], w_in[:, 4 * width:fox_in],
                              jnp.zeros((d, LANES - heads), w_in.dtype)], axis=1).astype(BF16)
    qk, v, og, rw, fproj = _norm_proj(
        x2d, seq, gain, shift, scale, w_cols,
        [(0, 2 * width), (2 * width, width), (3 * width, width), (4 * width, n_rw), (4 * width + n_rw, LANES)],
        [F32, BF16, F32, F32, F32])
    lora = rwkv_w2.shape[0]
    assert 2 * lora == LANES and rwkv_a2.shape[0] == lora and rwkv_g2.shape[0] == LANES
    zeros = jnp.zeros((lora, width), F32)
    params = dict(
        qg=jnp.tile(fox_qnorm, heads), kg=jnp.tile(fox_knorm, heads),
        fb=jnp.concatenate([fox_fbias, jnp.zeros((LANES - heads,), F32)]),
        mix=rwkv_mix, w0=rwkv_w0, a0=rwkv_a0,
        w2=jnp.concatenate([rwkv_w2, zeros], axis=0).astype(BF16),
        a2=jnp.concatenate([zeros, rwkv_a2], axis=0).astype(BF16),
        g2=rwkv_g2.astype(BF16), kk=rwkv_kk, ka=rwkv_ka, rk=rwkv_rk.reshape(-1))
    qn, kn, dcol, drow, rt, kt, at, bt, vv, cw, bonus, g = _odd_prep(qk, fproj, rw, seq, params)
    r3 = lambda a: a.reshape(b, seq, -1)
    att = _fox_attention(r3(qn), r3(kn), r3(v), r3(dcol), drow, fox_qnorm, fox_knorm)
    y = _rwkv_scan(rt, kt, at, bt, vv, cw, seq)
    return _odd_out(att.reshape(b * seq, width), og, y, bonus, g, rwkv_ln_w, rwkv_ln_b,
                    w_out.astype(BF16), x2d, seq, gate)


def kernel(x, c, ada_w, ada_b, norm_mix, norm_ffn, ev_w_in, ev_w_out, pool_w, pool_scale, od_w_in, od_w_out, fox_qnorm, fox_knorm, fox_fbias, rwkv_mix, rwkv_w0, rwkv_w2, rwkv_a0, rwkv_a2, rwkv_g2, rwkv_kk, rwkv_ka, rwkv_rk, rwkv_ln_w, rwkv_ln_b, peer_wq, peer_k1, peer_k2, peer_u, peer_v, final_norm):
    b, seq, d = x.shape
    depth = ada_w.shape[0]
    assert TOKEN_TILE == ATTN_TILE and seq % (2 * TOKEN_TILE) == 0
    mod = _adaln(c, ada_w, ada_b).reshape(depth, b, 6, 1, d)
    x2d = x.reshape(b * seq, d)
    for layer in range(depth):
        sh1, sc1, g1, sh2, sc2, g2 = (mod[layer, :, i] for i in range(6))
        j = layer // 2
        if layer % 2 == 0:
            x2d = _even_mixer(x2d, b, seq, norm_mix[layer], sh1, sc1, g1, ev_w_in[j], ev_w_out[j],
                              pool_w[j], pool_scale[j])
        else:
            x2d = _odd_mixer(x2d, b, seq, norm_mix[layer], sh1, sc1, g1, od_w_in[j], od_w_out[j],
                             fox_qnorm[j], fox_knorm[j], fox_fbias[j], rwkv_mix[j], rwkv_w0[j],
                             rwkv_w2[j], rwkv_a0[j], rwkv_a2[j], rwkv_g2[j], rwkv_kk[j], rwkv_ka[j],
                             rwkv_rk[j], rwkv_ln_w[j], rwkv_ln_b[j])
        x2d = _peer_ffn(x2d, seq, norm_ffn[layer], sh2, sc2, g2, peer_wq[layer].astype(BF16),
                        peer_k1[layer], peer_k2[layer], _pack_table(peer_u[layer]),
                        _pack_table(peer_v[layer]))
    return _final_norm(x2d, final_norm).reshape(b, seq, d)
```

```python
import functools

import jax
import jax.numpy as jnp
from jax import lax
from jax.experimental import pallas as pl
from jax.experimental.pallas import tpu as pltpu

F32 = jnp.float32
BF16 = jnp.bfloat16
I32 = jnp.int32
HIGHEST = lax.Precision.HIGHEST

LANES = 128
SUBLANES = 8
VMEM_BYTES_V7X = 64 * 1024 * 1024

HEAD_DIM = 64
HEADS_PER_LANE_BLOCK = LANES // HEAD_DIM
NORM_EPS = 1e-6
RWKV_GN_EPS = 64e-5
CHUNK = 64
POOL_WINDOWS = (2, 4, 8, 16)
PEER_TOPK = 16
PEER_NKEYS = 128
PEER_HEADS = 8
PEER_SLOTS = PEER_HEADS * PEER_TOPK

TOKEN_TILE = 256
ATTN_TILE = 256
PEER_TOKENS = 128

F32_EXP_UNDERFLOW = -104.0


def _compiler_params(semantics, vmem_mb=48):
    return pltpu.CompilerParams(dimension_semantics=semantics,
                                vmem_limit_bytes=min(vmem_mb * 1024 * 1024, VMEM_BYTES_V7X))


def _split_bf16(x, n):
    parts = []
    r = x
    for _ in range(n):
        p = r.astype(BF16)
        parts.append(p)
        r = r - p.astype(F32)
    return parts


def _dot_exact_rhs(x, m, n=3):
    acc = None
    for p in _split_bf16(x, n):
        t = jnp.dot(p, m, preferred_element_type=F32)
        acc = t if acc is None else acc + t
    return acc


def _dot_exact_lhs(m, x, n=3):
    acc = None
    for p in _split_bf16(x, n):
        t = jnp.dot(m, p, preferred_element_type=F32)
        acc = t if acc is None else acc + t
    return acc


def _log_sigmoid(z):
    return jnp.minimum(z, 0.0) - jnp.log1p(jnp.exp(-jnp.abs(z)))


def _sigmoid(z):
    return 1.0 / (1.0 + jnp.exp(-z))


def _head_sum_matrix(width):
    r = lax.broadcasted_iota(I32, (width, width), 0) // HEAD_DIM
    c = lax.broadcasted_iota(I32, (width, width), 1) // HEAD_DIM
    return (r == c).astype(BF16)


def _adaln_kernel(c_ref, w_ref, b_ref, o_ref):
    c = c_ref[...]
    c_act = c * _sigmoid(c)
    o_ref[0] = jnp.dot(c_act, w_ref[0], precision=HIGHEST, preferred_element_type=F32) + b_ref[0]


def _adaln(c, ada_w, ada_b):
    b, d = c.shape
    depth, _, n = ada_w.shape
    tn = 1536
    assert n % tn == 0 and b <= SUBLANES
    c_pad = jnp.zeros((SUBLANES, d), F32).at[:b].set(c)
    out = pl.pallas_call(
        _adaln_kernel,
        grid=(depth, n // tn),
        in_specs=[pl.BlockSpec((SUBLANES, d), lambda l, j: (0, 0)),
                  pl.BlockSpec((1, d, tn), lambda l, j: (l, 0, j)),
                  pl.BlockSpec((1, 1, tn), lambda l, j: (l, 0, j))],
        out_specs=pl.BlockSpec((1, SUBLANES, tn), lambda l, j: (l, 0, j)),
        out_shape=jax.ShapeDtypeStruct((depth, SUBLANES, n), F32),
        compiler_params=_compiler_params(("arbitrary", "arbitrary")),
        name="adaln",
    )(c_pad, ada_w, ada_b.reshape(depth, 1, n))
    return out[:, :b, :]


def _norm_proj_kernel(x_ref, gain_ref, shift_ref, scale_ref, w_ref, *out_refs, groups, emit_h):
    x = x_ref[...]
    ms = jnp.mean(x * x, axis=-1, keepdims=True)
    y = x * lax.rsqrt(ms + NORM_EPS) * gain_ref[...]
    h = y * (1.0 + scale_ref[0]) + shift_ref[0]
    hb = h.astype(BF16)
    for (start, width), o_ref in zip(groups, out_refs):
        o_ref[...] = jnp.dot(hb, w_ref[:, start:start + width],
                             preferred_element_type=F32).astype(o_ref.dtype)
    if emit_h:
        out_refs[-1][...] = h


def _norm_proj(x2d, seq, gain, shift, scale, w_bf16, groups, dtypes, emit_h=False):
    t, d = x2d.shape
    n = w_bf16.shape[1]
    tm = TOKEN_TILE
    assert t % tm == 0 and seq % tm == 0
    tiles_per_seq = seq // tm
    out_shape = [jax.ShapeDtypeStruct((t, wd), dt) for (_, wd), dt in zip(groups, dtypes)]
    out_specs = [pl.BlockSpec((tm, wd), lambda i: (i, 0)) for (_, wd) in groups]
    if emit_h:
        out_shape.append(jax.ShapeDtypeStruct((t, d), F32))
        out_specs.append(pl.BlockSpec((tm, d), lambda i: (i, 0)))
    return pl.pallas_call(
        functools.partial(_norm_proj_kernel, groups=tuple(groups), emit_h=emit_h),
        grid=(t // tm,),
        in_specs=[pl.BlockSpec((tm, d), lambda i: (i, 0)),
                  pl.BlockSpec((1, d), lambda i: (0, 0)),
                  pl.BlockSpec((1, 1, d), lambda i: (i // tiles_per_seq, 0, 0)),
                  pl.BlockSpec((1, 1, d), lambda i: (i // tiles_per_seq, 0, 0)),
                  pl.BlockSpec((d, n), lambda i: (0, 0))],
        out_specs=out_specs,
        out_shape=out_shape,
        compiler_params=_compiler_params(("arbitrary",)),
        name="norm_proj",
    )(x2d, gain.reshape(1, d), shift, scale, w_bf16)


def _sb_attention_kernel(q_ref, k_ref, v_ref, o_ref, acc_ref, carry_ref, *, tile, scale):
    i = pl.program_id(2)
    lane = lax.broadcasted_iota(I32, (1, LANES), 1)
    row = lax.broadcasted_iota(I32, (tile, tile), 0)
    col = lax.broadcasted_iota(I32, (tile, tile), 1)
    strict = col < row
    later = (row > col).astype(BF16)
    q = q_ref[0]
    hs = range(HEADS_PER_LANE_BLOCK)
    qh = [jnp.where((lane >= HEAD_DIM * h) & (lane < HEAD_DIM * (h + 1)), q, jnp.zeros_like(q)) for h in hs]

    def block(j, diag):
        start = pl.multiple_of(j * tile, tile)
        kb = k_ref[0, pl.ds(start, tile), :]
        vb = v_ref[0, pl.ds(start, tile), :]
        z = [lax.dot_general(qh[h], kb, (((1,), (1,)), ((), ())), preferred_element_type=F32) * scale
             for h in hs]
        log_beta = [_log_sigmoid(x) for x in z]
        log_1m = [log_beta[h] - z[h] for h in hs]
        if diag:
            log_1m = [jnp.where(strict, x, 0.0) for x in log_1m]
        stick = [_dot_exact_rhs(log_1m[h], later, 2) + carry_ref[h] for h in hs]
        w = [jnp.exp(log_beta[h] + stick[h]) for h in hs]
        if diag:
            w = [jnp.where(strict, x, 0.0) for x in w]
        for h in hs:
            acc_ref[h] += jnp.dot(w[h].astype(BF16), vb, preferred_element_type=F32)
            carry_ref[h] += jnp.sum(log_1m[h], axis=1, keepdims=True)

    acc_ref[...] = jnp.zeros_like(acc_ref)
    carry_ref[...] = jnp.zeros_like(carry_ref)
    block(i, True)

    def alive():
        return jnp.max(carry_ref[...]) > F32_EXP_UNDERFLOW

    def cond(c):
        return jnp.logical_and(c[0] < i, c[1])

    def body(c):
        block(i - 1 - c[0], False)
        return c[0] + 1, alive()

    lax.while_loop(cond, body, (jnp.int32(0), alive()))
    o_ref[0] = jnp.where(lane < HEAD_DIM, acc_ref[0], acc_ref[1]).astype(o_ref.dtype)


def _sb_attention(qkv, n_heads):
    b, s, _ = qkv.shape
    nblk = n_heads // HEADS_PER_LANE_BLOCK
    tile = ATTN_TILE
    assert s % tile == 0
    return pl.pallas_call(
        functools.partial(_sb_attention_kernel, tile=tile, scale=HEAD_DIM ** -0.5),
        grid=(b, nblk, s // tile),
        in_specs=[pl.BlockSpec((1, tile, LANES), lambda bi, hp, i: (bi, i, hp)),
                  pl.BlockSpec((1, s, LANES), lambda bi, hp, i: (bi, 0, nblk + hp)),
                  pl.BlockSpec((1, s, LANES), lambda bi, hp, i: (bi, 0, 2 * nblk + hp))],
        out_specs=pl.BlockSpec((1, tile, LANES), lambda bi, hp, i: (bi, i, hp)),
        out_shape=jax.ShapeDtypeStruct((b, s, n_heads * HEAD_DIM), BF16),
        scratch_shapes=[pltpu.VMEM((HEADS_PER_LANE_BLOCK, tile, LANES), F32),
                        pltpu.VMEM((HEADS_PER_LANE_BLOCK, tile, 1), F32)],
        compiler_params=_compiler_params(("arbitrary", "arbitrary", "arbitrary")),
        name="sb_attention",
    )(qkv, qkv, qkv)


def _even_out_kernel(sb_ref, pool_ref, prev_ref, pw_ref, ps_ref, wo_ref, x_ref, g_ref, o_ref,
                     *, tm, tiles_per_seq):
    i = pl.program_id(0)
    t0 = (i % tiles_per_seq) * tm
    t = t0 + lax.broadcasted_iota(I32, (tm, 1), 0)
    s = t0 - tm + lax.broadcasted_iota(I32, (1, 2 * tm), 1)
    chunk_end = ((t // CHUNK) + 1) * CHUNK
    group = LANES
    acc = jnp.dot(sb_ref[...], wo_ref[0:4 * group, :], preferred_element_type=F32)
    for g, w in enumerate(POOL_WINDOWS):
        lo = jnp.maximum(t - w // 2, 0)
        hi = jnp.minimum(t + (w - w // 2), chunk_end)
        band = ((s >= lo) & (s < hi)).astype(BF16)
        cur = pool_ref[:, g * group:(g + 1) * group]
        ext = jnp.concatenate([prev_ref[:, g * group:(g + 1) * group], cur], axis=0)
        window_sum = _dot_exact_lhs(band, ext)
        pooled = window_sum / (hi - lo).astype(F32) - cur
        y = jnp.dot(pooled.astype(BF16), pw_ref[g], preferred_element_type=F32)
        y = y * ps_ref[:, g * group:(g + 1) * group]
        acc += jnp.dot(y.astype(BF16), wo_ref[(4 + g) * group:(5 + g) * group, :],
                       preferred_element_type=F32)
    o_ref[...] = x_ref[...] + g_ref[0] * acc


def _even_out(sb2d, pool2d, seq, pool_w_bf16, pool_scale, w_out_bf16, x2d, gate):
    t, d = x2d.shape
    tm = TOKEN_TILE
    tiles_per_seq = seq // tm
    assert seq % CHUNK == 0 and len(POOL_WINDOWS) * LANES == pool2d.shape[1]
    pw = pool2d.shape[1]
    return pl.pallas_call(
        functools.partial(_even_out_kernel, tm=tm, tiles_per_seq=tiles_per_seq),
        grid=(t // tm,),
        in_specs=[pl.BlockSpec((tm, sb2d.shape[1]), lambda i: (i, 0)),
                  pl.BlockSpec((tm, pw), lambda i: (i, 0)),
                  pl.BlockSpec((tm, pw), lambda i: (jnp.maximum(i - 1, 0), 0)),
                  pl.BlockSpec(pool_w_bf16.shape, lambda i: (0, 0, 0)),
                  pl.BlockSpec((1, pw), lambda i: (0, 0)),
                  pl.BlockSpec((d, d), lambda i: (0, 0)),
                  pl.BlockSpec((tm, d), lambda i: (i, 0)),
                  pl.BlockSpec((1, 1, d), lambda i: (i // tiles_per_seq, 0, 0))],
        out_specs=pl.BlockSpec((tm, d), lambda i: (i, 0)),
        out_shape=jax.ShapeDtypeStruct((t, d), F32),
        compiler_params=_compiler_params(("arbitrary",)),
        name="even_out",
    )(sb2d, pool2d, pool2d, pool_w_bf16, pool_scale.reshape(1, pw), w_out_bf16, x2d, gate)


FOX_BIAS_TERMS = 3


def _fox_layout_constants(width, qk_bound):
    heads = width // HEAD_DIM
    nt = FOX_BIAS_TERMS
    wide = heads * LANES
    src = lax.broadcasted_iota(I32, (width, wide), 0)
    dst = lax.broadcasted_iota(I32, (width, wide), 1)
    place = ((dst // LANES == src // HEAD_DIM) & (dst % LANES == src % HEAD_DIM)).astype(BF16)
    src = lax.broadcasted_iota(I32, (nt * LANES, wide), 0)
    dst = lax.broadcasted_iota(I32, (nt * LANES, wide), 1)
    term, head = src // LANES, src % LANES
    place_dq = ((dst // LANES == head) & (dst % LANES == HEAD_DIM + term)).astype(BF16)
    place_dk = ((dst // LANES == head) & (dst % LANES == HEAD_DIM + nt + term)).astype(BF16)
    lane = lax.broadcasted_iota(I32, (1, wide), 1) % LANES
    bound_terms = _split_bf16(-qk_bound.astype(F32).reshape(1, 1), nt)
    q_const = jnp.where((lane >= HEAD_DIM + nt) & (lane < HEAD_DIM + 2 * nt), 1.0, 0.0)
    for k, term_k in enumerate(bound_terms):
        q_const = jnp.where(lane == HEAD_DIM + 2 * nt + k, term_k.astype(F32), q_const)
    k_const = jnp.where(((lane >= HEAD_DIM) & (lane < HEAD_DIM + nt))
                        | ((lane >= HEAD_DIM + 2 * nt) & (lane < HEAD_DIM + 3 * nt)), 1.0, 0.0)
    v_const = jnp.where(lane == HEAD_DIM, 1.0, 0.0)
    return place, place_dq, place_dk, q_const.astype(F32), k_const.astype(F32), v_const.astype(F32)


def _odd_prep_kernel(qk_ref, vin_ref, f_ref, rw_ref, rwprev_ref, hs_ref, tri_ref, ctri_ref,
                     place_ref, pdq_ref, pdk_ref, qc_ref, kc_ref, vc_ref,
                     qg_ref, kg_ref, fb_ref, mix_ref, w0_ref, w2_ref, a0_ref, a2_ref, g2_ref,
                     kkp_ref, ka_ref, rk_ref,
                     qa_ref, ka_out_ref, va_ref, drow_ref, rt_ref, kt_ref, at_ref, bt_ref, v_ref,
                     cw_ref, bonus_ref, g_ref, carry_ref, *, tm, tiles_per_seq, width):
    i = pl.program_id(0)
    first = (i % tiles_per_seq) == 0
    hs = hs_ref[...]

    def head_sum(x):
        return _dot_exact_rhs(x, hs, 2)

    inv_hd = 1.0 / HEAD_DIM
    q = qk_ref[:, :width]
    k = qk_ref[:, width:]
    qn = q * lax.rsqrt(head_sum(q * q) * inv_hd + NORM_EPS) * qg_ref[...] * (HEAD_DIM ** -0.5)
    kn = k * lax.rsqrt(head_sum(k * k) * inv_hd + NORM_EPS) * kg_ref[...]

    log_f = _log_sigmoid(f_ref[...] + fb_ref[...])

    @pl.when(first)
    def _():
        carry_ref[...] = jnp.zeros_like(carry_ref)

    dcum = _dot_exact_lhs(tri_ref[...], log_f) + carry_ref[...]
    carry_ref[...] = dcum[tm - 1:tm, :]
    drow_ref[0, 0] = jnp.transpose(dcum)[:SUBLANES, :]

    place = place_ref[...]
    d_terms = jnp.concatenate(_split_bf16(dcum, FOX_BIAS_TERMS), axis=1)
    qa_ref[...] = (jnp.dot(qn.astype(BF16), place, preferred_element_type=F32)
                   + jnp.dot(d_terms, pdq_ref[...], preferred_element_type=F32) + qc_ref[...]).astype(BF16)
    ka_out_ref[...] = (jnp.dot(kn.astype(BF16), place, preferred_element_type=F32)
                       + jnp.dot(-d_terms, pdk_ref[...], preferred_element_type=F32) + kc_ref[...]).astype(BF16)
    va_ref[...] = (jnp.dot(vin_ref[...], place, preferred_element_type=F32) + vc_ref[...]).astype(BF16)

    pd = rw_ref[...]
    prev_row = jnp.where(first, 0.0, rwprev_ref[SUBLANES - 1:SUBLANES, :])
    row0 = lax.broadcasted_iota(I32, (tm, 1), 0) == 0
    prev = jnp.where(row0, prev_row, pltpu.roll(pd, 1, axis=0))
    pd = pd + (prev - pd) * mix_ref[...]
    r = pd[:, :width]
    kr = pd[:, width:2 * width]
    vr = pd[:, 2 * width:3 * width]
    lora_in = pd[:, 3 * width:3 * width + LANES]
    xg = pd[:, 3 * width + LANES:3 * width + 2 * LANES]
    w_log = _log_sigmoid(w0_ref[...] + jnp.dot(jnp.tanh(lora_in).astype(BF16), w2_ref[...],
                                              preferred_element_type=F32)) - 0.5
    log_w = -jnp.exp(w_log)
    a = _sigmoid(a0_ref[...] + jnp.dot(lora_in.astype(BF16), a2_ref[...], preferred_element_type=F32))
    g_ref[...] = jnp.dot(_sigmoid(xg).astype(BF16), g2_ref[...], preferred_element_type=F32)
    kk = kr * kkp_ref[...]
    kk = kk / jnp.maximum(jnp.sqrt(head_sum(kk * kk)), 1e-12)
    k2 = kr * (1.0 + (a - 1.0) * ka_ref[...])
    bonus_ref[...] = head_sum(r * k2 * rk_ref[...]) * vr
    cw = _dot_exact_lhs(ctri_ref[...], log_w)
    grow = jnp.exp(-cw)
    rt_ref[...] = r * jnp.exp(cw)
    kt_ref[...] = k2 * grow
    bt_ref[...] = kk * a * grow
    at_ref[...] = -kk * jnp.exp(cw - log_w)
    v_ref[...] = vr
    cw_ref[...] = cw


def _odd_prep(qk, v, fproj, rw, seq, p, qk_bound):
    t = qk.shape[0]
    width = qk.shape[1] // 2
    tm = TOKEN_TILE
    tiles_per_seq = seq // tm
    nrw = rw.shape[1]
    assert tm % CHUNK == 0 and seq % tm == 0 and nrw == 3 * width + 2 * LANES
    assert HEAD_DIM + 3 * FOX_BIAS_TERMS <= LANES and width // HEAD_DIM <= SUBLANES
    ri = lax.broadcasted_iota(I32, (tm, tm), 0)
    ci = lax.broadcasted_iota(I32, (tm, tm), 1)
    tri = (ci <= ri).astype(BF16)
    ctri = ((ci <= ri) & (ci // CHUNK == ri // CHUNK)).astype(BF16)
    hs = _head_sum_matrix(width)
    row = lambda a: a.reshape(1, -1)
    full = lambda a: pl.BlockSpec(a.shape, lambda i: (0,) * a.ndim)
    tile = lambda n: pl.BlockSpec((tm, n), lambda i: (i, 0))
    consts = [hs, tri, ctri, *_fox_layout_constants(width, qk_bound),
              row(p['qg']), row(p['kg']), row(p['fb']), row(p['mix']), row(p['w0']),
              p['w2'], row(p['a0']), p['a2'], p['g2'], row(p['kk']), row(p['ka']), row(p['rk'])]
    wide = jax.ShapeDtypeStruct((t, width), F32)
    aug = (width // HEAD_DIM) * LANES
    return pl.pallas_call(
        functools.partial(_odd_prep_kernel, tm=tm, tiles_per_seq=tiles_per_seq, width=width),
        grid=(t // tm,),
        in_specs=[tile(2 * width), tile(width), tile(LANES), tile(nrw),
                  pl.BlockSpec((SUBLANES, nrw), lambda i: (jnp.maximum(i * (tm // SUBLANES) - 1, 0), 0))]
                 + [full(c) for c in consts],
        out_specs=[tile(aug), tile(aug), tile(aug),
                   pl.BlockSpec((1, 1, SUBLANES, tm), lambda i: (i // tiles_per_seq, i % tiles_per_seq, 0, 0))]
                  + [tile(width)] * 8,
        out_shape=[jax.ShapeDtypeStruct((t, aug), BF16)] * 3
                  + [jax.ShapeDtypeStruct((t // seq, tiles_per_seq, SUBLANES, tm), F32)] + [wide] * 8,
        scratch_shapes=[pltpu.VMEM((1, LANES), F32)],
        compiler_params=_compiler_params(("arbitrary",)),
        name="odd_prep",
    )(qk, v, fproj, rw, rw, *consts)


def _fox_attention_kernel(q_ref, k_ref, v_ref, drow_ref, o_ref, acc_ref, *, tile):
    hp = pl.program_id(1)
    i = pl.program_id(2)
    row = lax.broadcasted_iota(I32, (tile, tile), 0)
    col = lax.broadcasted_iota(I32, (tile, tile), 1)
    causal = col <= row
    sub = lax.broadcasted_iota(I32, (SUBLANES, tile), 0)
    hs = range(HEADS_PER_LANE_BLOCK)
    heads = [hp * HEADS_PER_LANE_BLOCK + h for h in hs]
    q = [q_ref[0, :, h * LANES:(h + 1) * LANES] for h in hs]

    def d_row(j, h):
        return jnp.sum(jnp.where(sub == heads[h], drow_ref[0, j], 0.0), axis=0, keepdims=True)

    def blocks(js, diag):
        work = [(pl.multiple_of(j * tile, tile), h) for j in js for h in hs]
        s = [lax.dot_general(q[h], k_ref[0, pl.ds(start, tile), h * LANES:(h + 1) * LANES],
                             (((1,), (1,)), ((), ())), preferred_element_type=F32) for start, h in work]
        p = [jnp.exp(x) for x in s]
        if diag:
            p = [jnp.where(causal, x, 0.0) for x in p]
        pv = [jnp.dot(p[n].astype(BF16), v_ref[0, pl.ds(start, tile), h * LANES:(h + 1) * LANES],
                      preferred_element_type=F32) for n, (start, h) in enumerate(work)]
        for h in hs:
            acc_ref[h] += functools.reduce(jnp.add, [pv[n] for n, (_, hh) in enumerate(work) if hh == h])

    acc_ref[...] = jnp.zeros_like(acc_ref)
    blocks([i], True)

    d_top = [jnp.max(d_row(i, h)) for h in hs]

    def alive(j):
        jc = jnp.maximum(j, 0)
        gaps = [d_top[h] - jnp.min(d_row(jc, h)) for h in hs]
        return functools.reduce(jnp.maximum, gaps) > F32_EXP_UNDERFLOW

    def cond(c):
        return jnp.logical_and(c[0] + 1 < i, c[1])

    def body(c):
        j = i - 1 - c[0]
        blocks([j, j - 1], False)
        return c[0] + 2, alive(j - 2)

    done, live = lax.while_loop(cond, body, (jnp.int32(0), alive(i - 1)))

    @pl.when(jnp.logical_and(done < i, live))
    def _():
        blocks([i - 1 - done], False)

    outs = [acc_ref[h][:, :HEAD_DIM] / acc_ref[h][:, HEAD_DIM:HEAD_DIM + 1] for h in hs]
    o_ref[0] = jnp.concatenate(outs, axis=1)


def _fox_qk_bound(q_gain, k_gain):
    bf16_slack = (1.0 + 2.0 ** -8) ** 2
    return (HEAD_DIM * HEAD_DIM ** -0.5 * bf16_slack) * jnp.max(jnp.abs(q_gain)) * jnp.max(jnp.abs(k_gain))


def _fox_attention(q_aug, k_aug, v_aug, drow):
    b, s, w = q_aug.shape
    nblk = w // (HEADS_PER_LANE_BLOCK * LANES)
    tile = ATTN_TILE
    assert drow.shape == (b, s // tile, SUBLANES, tile)
    pair = HEADS_PER_LANE_BLOCK * LANES
    return pl.pallas_call(
        functools.partial(_fox_attention_kernel, tile=tile),
        grid=(b, nblk, s // tile),
        in_specs=[pl.BlockSpec((1, tile, pair), lambda bi, hp, i: (bi, i, hp)),
                  pl.BlockSpec((1, s, pair), lambda bi, hp, i: (bi, 0, hp)),
                  pl.BlockSpec((1, s, pair), lambda bi, hp, i: (bi, 0, hp)),
                  pl.BlockSpec((1, s // tile, SUBLANES, tile), lambda bi, hp, i: (bi, 0, 0, 0))],
        out_specs=pl.BlockSpec((1, tile, LANES), lambda bi, hp, i: (bi, i, hp)),
        out_shape=jax.ShapeDtypeStruct((b, s, nblk * LANES), F32),
        scratch_shapes=[pltpu.VMEM((HEADS_PER_LANE_BLOCK, tile, LANES), F32)],
        compiler_params=_compiler_params(("arbitrary", "arbitrary", "arbitrary")),
        name="fox_attention",
    )(q_aug, k_aug, v_aug, drow)


_NN = (((1,), (0,)), ((), ()))
_NT = (((1,), (1,)), ((), ()))
_TN = (((0,), (0,)), ((), ()))


def _mm(a, b, dims=_NN):
    a_hi, a_lo = _split_bf16(a, 2)
    b_hi, b_lo = _split_bf16(b, 2)
    dot = lambda x, y: lax.dot_general(x, y, dims, preferred_element_type=F32)
    return dot(a_hi, b_hi) + (dot(a_hi, b_lo) + dot(a_lo, b_hi))


def _head_cols(h):
    return slice(h * HEAD_DIM, (h + 1) * HEAD_DIM)


def _rwkv_intra_kernel(rt_ref, kt_ref, at_ref, bt_ref, v_ref, inv_ref, arb_ref, fvu_ref, fvy_ref,
                       *, chunks, heads):
    c = CHUNK
    ri = lax.broadcasted_iota(I32, (2 * c, 2 * c), 0)
    ci = lax.broadcasted_iota(I32, (2 * c, 2 * c), 1)
    keep = (ci % c) < jnp.where(ri < c, ri, ri - c + 1)
    eye = (lax.broadcasted_iota(I32, (c, c), 0) == lax.broadcasted_iota(I32, (c, c), 1)).astype(F32)
    hs = range(heads)

    def chunk(cidx, carry):
        rows = pl.ds(pl.multiple_of(cidx * c, c), c)
        a_all = [jnp.where(keep,
                           _mm(jnp.concatenate([at_ref[rows, _head_cols(h)], rt_ref[rows, _head_cols(h)]], axis=0),
                               jnp.concatenate([bt_ref[rows, _head_cols(h)], kt_ref[rows, _head_cols(h)]], axis=0),
                               _NT), 0.0) for h in hs]
        a_ab = [a[:c, :c] for a in a_all]
        inv = [eye + a for a in a_ab]
        power = [_mm(a, a) for a in a_ab]
        for _ in range(4):
            both = [_mm(jnp.concatenate([inv[h], power[h]], axis=0), power[h]) for h in hs]
            inv = [inv[h] + both[h][:c] for h in hs]
            power = [both[h][c:] for h in hs]
        inv = [inv[h] + _mm(inv[h], power[h]) for h in hs]
        from_v = [_mm(a_all[h][:, c:], v_ref[rows, _head_cols(h)]) for h in hs]
        inv_ref[rows, :] = jnp.concatenate(inv, axis=1)
        arb_ref[rows, :] = jnp.concatenate([a[c:, :c] for a in a_all], axis=1)
        fvu_ref[rows, :] = jnp.concatenate([f[:c] for f in from_v], axis=1)
        fvy_ref[rows, :] = jnp.concatenate([f[c:] for f in from_v], axis=1)
        return carry

    lax.fori_loop(0, chunks, chunk, 0)


def _rwkv_scan_kernel(rt_ref, kt_ref, at_ref, bt_ref, v_ref, cw_ref, inv_ref, arb_ref, fvu_ref, fvy_ref,
                      y_ref, st_ref, *, steps_per_seq, chunks, heads):
    i = pl.program_id(0)

    @pl.when(i % steps_per_seq == 0)
    def _():
        st_ref[...] = jnp.zeros_like(st_ref)

    c = CHUNK
    hs = range(heads)

    def chunk(cidx, carry):
        rows = pl.ds(pl.multiple_of(cidx * c, c), c)
        last = pl.ds(pl.multiple_of(cidx * c, c) + (c - SUBLANES), SUBLANES)
        ar = [jnp.concatenate([at_ref[rows, _head_cols(h)], rt_ref[rows, _head_cols(h)]], axis=0) for h in hs]
        bk = [jnp.concatenate([bt_ref[rows, _head_cols(h)], kt_ref[rows, _head_cols(h)]], axis=0) for h in hs]
        st = [st_ref[h] for h in hs]
        from_state = [_mm(ar[h], st[h], _NT) for h in hs]
        u = [_mm(inv_ref[rows, _head_cols(h)], from_state[h][:c] + fvu_ref[rows, _head_cols(h)]) for h in hs]
        y = [from_state[h][c:] + fvy_ref[rows, _head_cols(h)] + _mm(arb_ref[rows, _head_cols(h)], u[h])
             for h in hs]
        for h in hs:
            decay = jnp.exp(cw_ref[last, _head_cols(h)][SUBLANES - 1:SUBLANES, :])
            uv = jnp.concatenate([u[h], v_ref[rows, _head_cols(h)]], axis=0)
            st_ref[h] = (st[h] + _mm(uv, bk[h], _TN)) * decay
        y_ref[rows, :] = jnp.concatenate(y, axis=1)
        return carry

    lax.fori_loop(0, chunks, chunk, 0)


def _rwkv_scan(rt, kt, at, bt, v, cw, seq):
    t, w = rt.shape
    heads = w // HEAD_DIM
    ts = 2 * TOKEN_TILE
    assert seq % ts == 0
    spec = pl.BlockSpec((ts, w), lambda i: (i, 0))
    wide = jax.ShapeDtypeStruct((t, w), F32)
    inv, arb, fvu, fvy = pl.pallas_call(
        functools.partial(_rwkv_intra_kernel, chunks=ts // CHUNK, heads=heads),
        grid=(t // ts,),
        in_specs=[spec] * 5,
        out_specs=[spec] * 4,
        out_shape=[wide] * 4,
        compiler_params=_compiler_params(("arbitrary",)),
        name="rwkv_intra",
    )(rt, kt, at, bt, v)
    return pl.pallas_call(
        functools.partial(_rwkv_scan_kernel, steps_per_seq=seq // ts, chunks=ts // CHUNK, heads=heads),
        grid=(t // ts,),
        in_specs=[spec] * 10,
        out_specs=spec,
        out_shape=wide,
        scratch_shapes=[pltpu.VMEM((heads, HEAD_DIM, HEAD_DIM), F32)],
        compiler_params=_compiler_params(("arbitrary",)),
        name="rwkv_scan",
    )(rt, kt, at, bt, v, cw, inv, arb, fvu, fvy)


def _odd_out_kernel(att_ref, og_ref, y_ref, bonus_ref, g_ref, hs_ref, lnw_ref, lnb_ref, wo_ref,
                    x_ref, gate_ref, o_ref, *, width):
    hs = hs_ref[...]
    inv_hd = 1.0 / HEAD_DIM
    fox = att_ref[...] * _sigmoid(og_ref[...])
    y = y_ref[...]
    mu = _dot_exact_rhs(y, hs, 2) * inv_hd
    yc = y - mu
    var = _dot_exact_rhs(yc * yc, hs, 2) * inv_hd
    yn = yc * lax.rsqrt(var + RWKV_GN_EPS) * lnw_ref[...] + lnb_ref[...]
    rw = (yn + bonus_ref[...]) * g_ref[...]
    acc = jnp.dot(fox.astype(BF16), wo_ref[:width, :], preferred_element_type=F32)
    acc += jnp.dot(rw.astype(BF16), wo_ref[width:, :], preferred_element_type=F32)
    o_ref[...] = x_ref[...] + gate_ref[0] * acc


def _odd_out(att, og, y, bonus, g, ln_w, ln_b, w_out_bf16, x2d, seq, gate):
    t, d = x2d.shape
    width = att.shape[1]
    tm = TOKEN_TILE
    tiles_per_seq = seq // tm
    hs = _head_sum_matrix(width)
    tile = lambda n: pl.BlockSpec((tm, n), lambda i: (i, 0))
    return pl.pallas_call(
        functools.partial(_odd_out_kernel, width=width),
        grid=(t // tm,),
        in_specs=[tile(width)] * 5
                 + [pl.BlockSpec(hs.shape, lambda i: (0, 0)),
                    pl.BlockSpec((1, width), lambda i: (0, 0)), pl.BlockSpec((1, width), lambda i: (0, 0)),
                    pl.BlockSpec((d, d), lambda i: (0, 0)), tile(d),
                    pl.BlockSpec((1, 1, d), lambda i: (i // tiles_per_seq, 0, 0))],
        out_specs=tile(d),
        out_shape=jax.ShapeDtypeStruct((t, d), F32),
        compiler_params=_compiler_params(("arbitrary",)),
        name="odd_out",
    )(att, og, y, bonus, g, hs, ln_w.reshape(1, width), ln_b.reshape(1, width), w_out_bf16, x2d, gate)


def _top_rows(s, k, prio, payload=None):
    big = jnp.int32(2 ** 30)
    vals, picks = [], []
    for _ in range(k):
        m = jnp.max(s, axis=0, keepdims=True)
        am = jnp.min(jnp.where(s == m, prio, big), axis=0, keepdims=True)
        hit = prio == am
        vals.append(m)
        if payload is None:
            picks.append(am)
        else:
            picks.append(jnp.max(jnp.where(hit, payload, -1), axis=0, keepdims=True))
        s = jnp.where(hit, -jnp.inf, s)
    return jnp.concatenate(vals, axis=0), jnp.concatenate(picks, axis=0)


def _staircase_groups():
    k, g = PEER_TOPK, SUBLANES
    groups = []
    for b in range(k):
        count = k // (b + 1)
        if count <= 1:
            break
        for a0 in range(0, count, g):
            groups.append((a0, b, False, min(g, count - a0)))
    first_single = next(b for b in range(k) if k // (b + 1) <= 1)
    for b0 in range(first_single, k, g):
        groups.append((0, b0, True, min(g, k - b0)))
    return groups


def _peer_topk_kernel(q_ref, k1_ref, k2_ref, idx_ref, gate_ref, *, row_scale):
    half = PEER_NKEYS
    nt = (((1,), (1,)), ((), ()))
    tt = q_ref.shape[0]
    key_id = lax.broadcasted_iota(I32, (half, tt), 0)
    sub = lax.broadcasted_iota(I32, (SUBLANES, 1), 0)
    groups = _staircase_groups()
    prio = jnp.concatenate(
        [jnp.broadcast_to((sub * 0 + a0) * PEER_TOPK + b0 + sub if along_b
                          else (a0 + sub) * PEER_TOPK + b0, (SUBLANES, tt))
         for a0, b0, along_b, _ in groups], axis=0)
    idx_rows = []
    for h in range(PEER_HEADS):
        q1 = q_ref[:, (2 * h) * half:(2 * h + 1) * half]
        q2 = q_ref[:, (2 * h + 1) * half:(2 * h + 2) * half]
        s1 = lax.dot_general(k1_ref[...], q1, nt, precision=HIGHEST, preferred_element_type=F32)
        s2 = lax.dot_general(k2_ref[...], q2, nt, precision=HIGHEST, preferred_element_type=F32)
        v1, i1 = _top_rows(s1, PEER_TOPK, key_id)
        v2, i2 = _top_rows(s2, PEER_TOPK, key_id)
        cand, cand_idx = [], []
        for a0, b0, along_b, valid in groups:
            if along_b:
                val = v1[a0:a0 + 1] + v2[b0:b0 + SUBLANES]
                eid = i1[a0:a0 + 1] * PEER_NKEYS + i2[b0:b0 + SUBLANES]
            else:
                val = v1[a0:a0 + SUBLANES] + v2[b0:b0 + 1]
                eid = i1[a0:a0 + SUBLANES] * PEER_NKEYS + i2[b0:b0 + 1]
            cand.append(jnp.where(sub < valid, val, -jnp.inf))
            cand_idx.append(eid)
        top_s, idx = _top_rows(jnp.concatenate(cand, axis=0), PEER_TOPK, prio,
                               payload=jnp.concatenate(cand_idx, axis=0))
        e = jnp.exp(top_s - top_s[0:1])
        gate_ref[h * PEER_TOPK:(h + 1) * PEER_TOPK, :] = e / jnp.sum(e, axis=0, keepdims=True)
        idx_rows.append(idx)
    idx_ref[...] = jnp.transpose(jnp.concatenate(idx_rows, axis=0)) * row_scale


def _peer_topk(q2d, k1, k2, row_scale):
    t, n = q2d.shape
    tt = TOKEN_TILE
    assert PEER_TOPK % SUBLANES == 0
    return pl.pallas_call(
        functools.partial(_peer_topk_kernel, row_scale=row_scale),
        grid=(t // tt,),
        in_specs=[pl.BlockSpec((tt, n), lambda i: (i, 0)),
                  pl.BlockSpec(k1.shape, lambda i: (0, 0)),
                  pl.BlockSpec(k2.shape, lambda i: (0, 0))],
        out_specs=[pl.BlockSpec((tt, PEER_SLOTS), lambda i: (i, 0)),
                   pl.BlockSpec((PEER_SLOTS, tt), lambda i: (0, i))],
        out_shape=[jax.ShapeDtypeStruct((t, PEER_SLOTS), I32),
                   jax.ShapeDtypeStruct((PEER_SLOTS, t), F32)],
        compiler_params=_compiler_params(("arbitrary",)),
        name="peer_topk",
    )(q2d, k1, k2)


def _pack_table_kernel(t_ref, o_ref, *, pairs):
    high = jnp.int32(-65536)
    for j in range(pairs):
        lo = t_ref[:, (2 * j) * LANES:(2 * j + 1) * LANES].astype(BF16).astype(F32)
        hi = t_ref[:, (2 * j + 1) * LANES:(2 * j + 2) * LANES].astype(BF16).astype(F32)
        lo_bits = lax.shift_right_logical(pltpu.bitcast(lo, I32), 16)
        words = lax.bitwise_or(lo_bits, lax.bitwise_and(pltpu.bitcast(hi, I32), high))
        o_ref[pl.ds(j, t_ref.shape[0], stride=pairs), :] = words


def _pack_table(table):
    e, d = table.shape
    pairs = d // (2 * LANES)
    rows = 2 * TOKEN_TILE
    return pl.pallas_call(
        functools.partial(_pack_table_kernel, pairs=pairs),
        grid=(e // rows,),
        in_specs=[pl.BlockSpec((rows, d), lambda i: (i, 0))],
        out_specs=pl.BlockSpec((rows * pairs, LANES), lambda i: (i, 0)),
        out_shape=jax.ShapeDtypeStruct((e * pairs, LANES), I32),
        compiler_params=_compiler_params(("arbitrary",)),
        name="pack_table",
    )(table)


def _unpack_words(w):
    lo = pltpu.bitcast(lax.shift_left(w, 16), F32)
    hi = pltpu.bitcast(lax.bitwise_and(w, jnp.int32(-65536)), F32)
    return lo, hi


def _gather_rows(idx_ref, tab_ref, tile_ref, tok, pairs):
    for m in range(PEER_SLOTS):
        row = pl.multiple_of(idx_ref[tok, m], pairs)
        tile_ref[m * pairs:(m + 1) * pairs, :] = tab_ref[pl.ds(row, pairs), :]


def _tile_chunk(tile_ref, j, pairs):
    return _unpack_words(tile_ref[pl.ds(j, PEER_SLOTS, stride=pairs), :])


PEER_TILE_BUFFERS = 8


def _pipelined_tokens(ntok, gather, compute, init):
    nb = PEER_TILE_BUFFERS
    ahead = nb // 2
    assert ntok % nb == 0
    for k in range(ahead):
        gather(k, k)

    def trip(p, carry):
        tok = nb * p
        for k in range(nb):
            gather(jnp.minimum(tok + k + ahead, ntok - 1), (k + ahead) % nb)
            carry = compute(tok + k, k, carry)
        return carry

    return lax.fori_loop(0, ntok // nb, trip, init)


def _pick_row(block, r):
    sub = lax.broadcasted_iota(I32, block.shape, 0)
    return jnp.sum(jnp.where(sub == r, block, 0.0), axis=0, keepdims=True)


def _peer_act_kernel(idx_ref, tab_ref, h_ref, gate_ref, coef_ref, *tiles, pairs):
    ntok = h_ref.shape[0]
    lane = lax.broadcasted_iota(I32, (PEER_SLOTS, ntok), 1)

    def gather(tok, buf):
        _gather_rows(idx_ref, tab_ref, tiles[buf], tok, pairs)

    def compute(tok, buf, carry):
        base = pl.multiple_of((tok // SUBLANES) * SUBLANES, SUBLANES)
        h_row = _pick_row(h_ref[pl.ds(base, SUBLANES), :], tok % SUBLANES)
        terms = []
        for j in range(pairs):
            lo, hi = _tile_chunk(tiles[buf], j, pairs)
            h_lo = h_row[:, (2 * j) * LANES:(2 * j + 1) * LANES]
            h_hi = h_row[:, (2 * j + 1) * LANES:(2 * j + 2) * LANES]
            terms.append(lo * h_lo + hi * h_hi)
        while len(terms) > 1:
            terms = [a + b for a, b in zip(terms[::2], terms[1::2])]
        col = jnp.sum(terms[0], axis=1, keepdims=True)
        coef_ref[...] = jnp.where(lane == tok, col, coef_ref[...])
        return carry

    coef_ref[...] = jnp.zeros_like(coef_ref)
    _pipelined_tokens(ntok, gather, compute, 0)
    act_t = coef_ref[...]
    gelu = 0.5 * act_t * (1.0 + lax.erf(act_t * (2.0 ** -0.5)))
    coef_ref[...] = gate_ref[...] * gelu


def _peer_act(idx, table_words, h2d, gate_t):
    t, d = h2d.shape
    pairs = d // (2 * LANES)
    nt = PEER_TOKENS
    return pl.pallas_call(
        functools.partial(_peer_act_kernel, pairs=pairs),
        grid=(t // nt,),
        in_specs=[pl.BlockSpec((nt, PEER_SLOTS), lambda i: (i, 0), memory_space=pltpu.SMEM),
                  pl.BlockSpec(table_words.shape, lambda i: (0, 0), pipeline_mode=pl.Buffered(1)),
                  pl.BlockSpec((nt, d), lambda i: (i, 0)),
                  pl.BlockSpec((PEER_SLOTS, nt), lambda i: (0, i))],
        out_specs=pl.BlockSpec((PEER_SLOTS, nt), lambda i: (0, i)),
        out_shape=jax.ShapeDtypeStruct((PEER_SLOTS, t), F32),
        scratch_shapes=[pltpu.VMEM((pairs * PEER_SLOTS, LANES), I32)] * PEER_TILE_BUFFERS,
        compiler_params=_compiler_params(("arbitrary",), vmem_mb=56),
        name="peer_act",
    )(idx, table_words, h2d, gate_t)


def _peer_out_kernel(idx_ref, tab_ref, coef_ref, x_ref, g_ref, o_ref, *tiles, pairs):
    ntok, d = x_ref.shape
    lane = lax.broadcasted_iota(I32, (PEER_SLOTS, ntok), 1)
    sub = lax.broadcasted_iota(I32, (SUBLANES, d), 0)
    groups = PEER_SLOTS // SUBLANES

    def gather(tok, buf):
        _gather_rows(idx_ref, tab_ref, tiles[buf], tok, pairs)

    def compute(tok, buf, rows):
        coef = jnp.sum(jnp.where(lane == tok, coef_ref[...], 0.0), axis=1, keepdims=True)
        pieces = []
        for j in range(pairs):
            for vals in _tile_chunk(tiles[buf], j, pairs):
                prod = (vals * coef).reshape(groups, SUBLANES, LANES)
                pieces.append(jnp.sum(jnp.sum(prod, axis=0), axis=0, keepdims=True))
        row = jnp.concatenate(pieces, axis=1)
        rows = jnp.where(sub == tok % SUBLANES, row, rows)
        base = pl.multiple_of((tok // SUBLANES) * SUBLANES, SUBLANES)
        o_ref[pl.ds(base, SUBLANES), :] = x_ref[pl.ds(base, SUBLANES), :] + g_ref[0] * rows
        return rows

    _pipelined_tokens(ntok, gather, compute, jnp.zeros((SUBLANES, d), F32))


def _peer_out(idx, table_words, coef_t, x2d, seq, gate):
    t, d = x2d.shape
    pairs = d // (2 * LANES)
    nt = PEER_TOKENS
    per_seq = seq // nt
    return pl.pallas_call(
        functools.partial(_peer_out_kernel, pairs=pairs),
        grid=(t // nt,),
        in_specs=[pl.BlockSpec((nt, PEER_SLOTS), lambda i: (i, 0), memory_space=pltpu.SMEM),
                  pl.BlockSpec(table_words.shape, lambda i: (0, 0), pipeline_mode=pl.Buffered(1)),
                  pl.BlockSpec((PEER_SLOTS, nt), lambda i: (0, i)),
                  pl.BlockSpec((nt, d), lambda i: (i, 0)),
                  pl.BlockSpec((1, 1, d), lambda i: (i // per_seq, 0, 0))],
        out_specs=pl.BlockSpec((nt, d), lambda i: (i, 0)),
        out_shape=jax.ShapeDtypeStruct((t, d), F32),
        scratch_shapes=[pltpu.VMEM((pairs * PEER_SLOTS, LANES), I32)] * PEER_TILE_BUFFERS,
        compiler_params=_compiler_params(("arbitrary",), vmem_mb=56),
        name="peer_out",
    )(idx, table_words, coef_t, x2d, gate)


def _peer_ffn(x2d, seq, gain, shift, scale, gate, wq_bf16, k1, k2, u_words, v_words):
    n = wq_bf16.shape[1]
    q, h = _norm_proj(x2d, seq, gain, shift, scale, wq_bf16, [(0, n)], [F32], emit_h=True)
    idx, gate_t = _peer_topk(q, k1, k2, x2d.shape[1] // (2 * LANES))
    coef_t = _peer_act(idx, u_words, h, gate_t)
    return _peer_out(idx, v_words, coef_t, x2d, seq, gate)


def _final_norm_kernel(x_ref, gain_ref, o_ref):
    x = x_ref[...]
    ms = jnp.mean(x * x, axis=-1, keepdims=True)
    o_ref[...] = x * lax.rsqrt(ms + NORM_EPS) * gain_ref[...]


def _final_norm(x2d, gain):
    t, d = x2d.shape
    tm = TOKEN_TILE
    return pl.pallas_call(
        _final_norm_kernel,
        grid=(t // tm,),
        in_specs=[pl.BlockSpec((tm, d), lambda i: (i, 0)), pl.BlockSpec((1, d), lambda i: (0, 0))],
        out_specs=pl.BlockSpec((tm, d), lambda i: (i, 0)),
        out_shape=jax.ShapeDtypeStruct((t, d), F32),
        compiler_params=_compiler_params(("arbitrary",)),
        name="final_norm",
    )(x2d, gain.reshape(1, d))


def _even_mixer(x2d, b, seq, gain, shift, scale, gate, w_in, w_out, pool_w, pool_scale):
    d = x2d.shape[1]
    pool_width = len(POOL_WINDOWS) * LANES
    sb_width = (w_in.shape[1] - pool_width) // 3
    qkv, pool = _norm_proj(x2d, seq, gain, shift, scale, w_in.astype(BF16),
                           [(0, 3 * sb_width), (3 * sb_width, pool_width)], [BF16, F32])
    sb = _sb_attention(qkv.reshape(b, seq, 3 * sb_width), sb_width // HEAD_DIM)
    return _even_out(sb.reshape(b * seq, sb_width), pool, seq, pool_w.astype(BF16), pool_scale,
                     w_out.astype(BF16), x2d, gate)


def _odd_mixer(x2d, b, seq, gain, shift, scale, gate, w_in, w_out, fox_qnorm, fox_knorm, fox_fbias,
               rwkv_mix, rwkv_w0, rwkv_w2, rwkv_a0, rwkv_a2, rwkv_g2, rwkv_kk, rwkv_ka, rwkv_rk,
               rwkv_ln_w, rwkv_ln_b):
    d = x2d.shape[1]
    width = rwkv_w0.shape[0]
    heads = width // HEAD_DIM
    fox_in = 4 * width + heads
    n_rw = w_in.shape[1] - fox_in
    w_cols = jnp.concatenate([w_in[:, :4 * width], w_in[:, fox_in:], w_in[:, 4 * width:fox_in],
                              jnp.zeros((d, LANES - heads), w_in.dtype)], axis=1).astype(BF16)
    qk, v, og, rw, fproj = _norm_proj(
        x2d, seq, gain, shift, scale, w_cols,
        [(0, 2 * width), (2 * width, width), (3 * width, width), (4 * width, n_rw), (4 * width + n_rw, LANES)],
        [F32, BF16, F32, F32, F32])
    lora = rwkv_w2.shape[0]
    assert 2 * lora == LANES and rwkv_a2.shape[0] == lora and rwkv_g2.shape[0] == LANES
    zeros = jnp.zeros((lora, width), F32)
    params = dict(
        qg=jnp.tile(fox_qnorm, heads), kg=jnp.tile(fox_knorm, heads),
        fb=jnp.concatenate([fox_fbias, jnp.zeros((LANES - heads,), F32)]),
        mix=rwkv_mix, w0=rwkv_w0, a0=rwkv_a0,
        w2=jnp.concatenate([rwkv_w2, zeros], axis=0).astype(BF16),
        a2=jnp.concatenate([zeros, rwkv_a2], axis=0).astype(BF16),
        g2=rwkv_g2.astype(BF16), kk=rwkv_kk, ka=rwkv_ka, rk=rwkv_rk.reshape(-1))
    q_aug, k_aug, v_aug, drow, rt, kt, at, bt, vv, cw, bonus, g = _odd_prep(
        qk, v, fproj, rw, seq, params, _fox_qk_bound(fox_qnorm, fox_knorm))
    r3 = lambda a: a.reshape(b, seq, -1)
    att = _fox_attention(r3(q_aug), r3(k_aug), r3(v_aug), drow)
    y = _rwkv_scan(rt, kt, at, bt, vv, cw, seq)
    return _odd_out(att.reshape(b * seq, width), og, y, bonus, g, rwkv_ln_w, rwkv_ln_b,
                    w_out.astype(BF16), x2d, seq, gate)


def kernel(x, c, ada_w, ada_b, norm_mix, norm_ffn, ev_w_in, ev_w_out, pool_w, pool_scale, od_w_in, od_w_out, fox_qnorm, fox_knorm, fox_fbias, rwkv_mix, rwkv_w0, rwkv_w2, rwkv_a0, rwkv_a2, rwkv_g2, rwkv_kk, rwkv_ka, rwkv_rk, rwkv_ln_w, rwkv_ln_b, peer_wq, peer_k1, peer_k2, peer_u, peer_v, final_norm):
    b, seq, d = x.shape
    depth = ada_w.shape[0]
    assert TOKEN_TILE == ATTN_TILE and seq % (2 * TOKEN_TILE) == 0
    mod = _adaln(c, ada_w, ada_b).reshape(depth, b, 6, 1, d)
    x2d = x.reshape(b * seq, d)
    for layer in range(depth):
        sh1, sc1, g1, sh2, sc2, g2 = (mod[layer, :, i] for i in range(6))
        j = layer // 2
        if layer % 2 == 0:
            x2d = _even_mixer(x2d, b, seq, norm_mix[layer], sh1, sc1, g1, ev_w_in[j], ev_w_out[j],
                              pool_w[j], pool_scale[j])
        else:
            x2d = _odd_mixer(x2d, b, seq, norm_mix[layer], sh1, sc1, g1, od_w_in[j], od_w_out[j],
                             fox_qnorm[j], fox_knorm[j], fox_fbias[j], rwkv_mix[j], rwkv_w0[j],
                             rwkv_w2[j], rwkv_a0[j], rwkv_a2[j], rwkv_g2[j], rwkv_kk[j], rwkv_ka[j],
                             rwkv_rk[j], rwkv_ln_w[j], rwkv_ln_b[j])
        x2d = _peer_ffn(x2d, seq, norm_ffn[layer], sh2, sc2, g2, peer_wq[layer].astype(BF16),
                        peer_k1[layer], peer_k2[layer], _pack_table(peer_u[layer]),
                        _pack_table(peer_v[layer]))
    return _final_norm(x2d, final_norm).reshape(b, seq, d)
```

```python
import functools

import jax
import jax.numpy as jnp
from jax import lax
from jax.experimental import pallas as pl
from jax.experimental.pallas import tpu as pltpu

F32 = jnp.float32
BF16 = jnp.bfloat16
I32 = jnp.int32
HIGHEST = lax.Precision.HIGHEST

LANES = 128
SUBLANES = 8
VMEM_BYTES_V7X = 64 * 1024 * 1024

HEAD_DIM = 64
HEADS_PER_LANE_BLOCK = LANES // HEAD_DIM
NORM_EPS = 1e-6
RWKV_GN_EPS = 64e-5
CHUNK = 64
POOL_WINDOWS = (2, 4, 8, 16)
PEER_TOPK = 16
PEER_NKEYS = 128
PEER_HEADS = 8
PEER_SLOTS = PEER_HEADS * PEER_TOPK

TOKEN_TILE = 256
ATTN_TILE = 256
PEER_TOKENS = 128

F32_EXP_UNDERFLOW = -104.0


def _compiler_params(semantics, vmem_mb=48):
    return pltpu.CompilerParams(dimension_semantics=semantics,
                                vmem_limit_bytes=min(vmem_mb * 1024 * 1024, VMEM_BYTES_V7X))


def _split_bf16(x, n):
    parts = []
    r = x
    for _ in range(n):
        p = r.astype(BF16)
        parts.append(p)
        r = r - p.astype(F32)
    return parts


def _dot_exact_rhs(x, m, n=3):
    acc = None
    for p in _split_bf16(x, n):
        t = jnp.dot(p, m, preferred_element_type=F32)
        acc = t if acc is None else acc + t
    return acc


def _dot_exact_lhs(m, x, n=3):
    acc = None
    for p in _split_bf16(x, n):
        t = jnp.dot(m, p, preferred_element_type=F32)
        acc = t if acc is None else acc + t
    return acc


def _log_sigmoid(z):
    return jnp.minimum(z, 0.0) - jnp.log1p(jnp.exp(-jnp.abs(z)))


def _sigmoid(z):
    return 1.0 / (1.0 + jnp.exp(-z))


def _head_sum_matrix(width):
    r = lax.broadcasted_iota(I32, (width, width), 0) // HEAD_DIM
    c = lax.broadcasted_iota(I32, (width, width), 1) // HEAD_DIM
    return (r == c).astype(BF16)


def _adaln_kernel(c_ref, w_ref, b_ref, o_ref):
    c = c_ref[...]
    c_act = c * _sigmoid(c)
    o_ref[0] = jnp.dot(c_act, w_ref[0], precision=HIGHEST, preferred_element_type=F32) + b_ref[0]


def _adaln(c, ada_w, ada_b):
    b, d = c.shape
    depth, _, n = ada_w.shape
    tn = 1536
    assert n % tn == 0 and b <= SUBLANES
    c_pad = jnp.zeros((SUBLANES, d), F32).at[:b].set(c)
    out = pl.pallas_call(
        _adaln_kernel,
        grid=(depth, n // tn),
        in_specs=[pl.BlockSpec((SUBLANES, d), lambda l, j: (0, 0)),
                  pl.BlockSpec((1, d, tn), lambda l, j: (l, 0, j)),
                  pl.BlockSpec((1, 1, tn), lambda l, j: (l, 0, j))],
        out_specs=pl.BlockSpec((1, SUBLANES, tn), lambda l, j: (l, 0, j)),
        out_shape=jax.ShapeDtypeStruct((depth, SUBLANES, n), F32),
        compiler_params=_compiler_params(("arbitrary", "arbitrary")),
        name="adaln",
    )(c_pad, ada_w, ada_b.reshape(depth, 1, n))
    return out[:, :b, :]


def _norm_proj_kernel(x_ref, gain_ref, shift_ref, scale_ref, w_ref, *out_refs, groups, emit_h):
    x = x_ref[...]
    ms = jnp.mean(x * x, axis=-1, keepdims=True)
    y = x * lax.rsqrt(ms + NORM_EPS) * gain_ref[...]
    h = y * (1.0 + scale_ref[0]) + shift_ref[0]
    hb = h.astype(BF16)
    for (start, width), o_ref in zip(groups, out_refs):
        o_ref[...] = jnp.dot(hb, w_ref[:, start:start + width],
                             preferred_element_type=F32).astype(o_ref.dtype)
    if emit_h:
        out_refs[-1][...] = h


def _norm_proj(x2d, seq, gain, shift, scale, w_bf16, groups, dtypes, emit_h=False):
    t, d = x2d.shape
    n = w_bf16.shape[1]
    tm = TOKEN_TILE
    assert t % tm == 0 and seq % tm == 0
    tiles_per_seq = seq // tm
    out_shape = [jax.ShapeDtypeStruct((t, wd), dt) for (_, wd), dt in zip(groups, dtypes)]
    out_specs = [pl.BlockSpec((tm, wd), lambda i: (i, 0)) for (_, wd) in groups]
    if emit_h:
        out_shape.append(jax.ShapeDtypeStruct((t, d), F32))
        out_specs.append(pl.BlockSpec((tm, d), lambda i: (i, 0)))
    return pl.pallas_call(
        functools.partial(_norm_proj_kernel, groups=tuple(groups), emit_h=emit_h),
        grid=(t // tm,),
        in_specs=[pl.BlockSpec((tm, d), lambda i: (i, 0)),
                  pl.BlockSpec((1, d), lambda i: (0, 0)),
                  pl.BlockSpec((1, 1, d), lambda i: (i // tiles_per_seq, 0, 0)),
                  pl.BlockSpec((1, 1, d), lambda i: (i // tiles_per_seq, 0, 0)),
                  pl.BlockSpec((d, n), lambda i: (0, 0))],
        out_specs=out_specs,
        out_shape=out_shape,
        compiler_params=_compiler_params(("arbitrary",)),
        name="norm_proj",
    )(x2d, gain.reshape(1, d), shift, scale, w_bf16)


def _sb_attention_kernel(q_ref, k_ref, v_ref, o_ref, acc_ref, carry_ref, *, tile, scale):
    i = pl.program_id(2)
    lane = lax.broadcasted_iota(I32, (1, LANES), 1)
    row = lax.broadcasted_iota(I32, (tile, tile), 0)
    col = lax.broadcasted_iota(I32, (tile, tile), 1)
    strict = col < row
    later = (row > col).astype(BF16)
    q = q_ref[0]
    hs = range(HEADS_PER_LANE_BLOCK)
    qh = [jnp.where((lane >= HEAD_DIM * h) & (lane < HEAD_DIM * (h + 1)), q, jnp.zeros_like(q)) for h in hs]

    def block(j, diag):
        start = pl.multiple_of(j * tile, tile)
        kb = k_ref[0, pl.ds(start, tile), :]
        vb = v_ref[0, pl.ds(start, tile), :]
        z = [lax.dot_general(qh[h], kb, (((1,), (1,)), ((), ())), preferred_element_type=F32) * scale
             for h in hs]
        log_beta = [_log_sigmoid(x) for x in z]
        log_1m = [log_beta[h] - z[h] for h in hs]
        if diag:
            log_1m = [jnp.where(strict, x, 0.0) for x in log_1m]
        stick = [_dot_exact_rhs(log_1m[h], later, 2) + carry_ref[h] for h in hs]
        w = [jnp.exp(log_beta[h] + stick[h]) for h in hs]
        if diag:
            w = [jnp.where(strict, x, 0.0) for x in w]
        for h in hs:
            acc_ref[h] += jnp.dot(w[h].astype(BF16), vb, preferred_element_type=F32)
            carry_ref[h] += jnp.sum(log_1m[h], axis=1, keepdims=True)

    acc_ref[...] = jnp.zeros_like(acc_ref)
    carry_ref[...] = jnp.zeros_like(carry_ref)
    block(i, True)

    def alive():
        return jnp.max(carry_ref[...]) > F32_EXP_UNDERFLOW

    def cond(c):
        return jnp.logical_and(c[0] < i, c[1])

    def body(c):
        block(i - 1 - c[0], False)
        return c[0] + 1, alive()

    lax.while_loop(cond, body, (jnp.int32(0), alive()))
    o_ref[0] = jnp.where(lane < HEAD_DIM, acc_ref[0], acc_ref[1]).astype(o_ref.dtype)


def _sb_attention(qkv, n_heads):
    b, s, _ = qkv.shape
    nblk = n_heads // HEADS_PER_LANE_BLOCK
    tile = ATTN_TILE
    assert s % tile == 0
    return pl.pallas_call(
        functools.partial(_sb_attention_kernel, tile=tile, scale=HEAD_DIM ** -0.5),
        grid=(b, nblk, s // tile),
        in_specs=[pl.BlockSpec((1, tile, LANES), lambda bi, hp, i: (bi, i, hp)),
                  pl.BlockSpec((1, s, LANES), lambda bi, hp, i: (bi, 0, nblk + hp)),
                  pl.BlockSpec((1, s, LANES), lambda bi, hp, i: (bi, 0, 2 * nblk + hp))],
        out_specs=pl.BlockSpec((1, tile, LANES), lambda bi, hp, i: (bi, i, hp)),
        out_shape=jax.ShapeDtypeStruct((b, s, n_heads * HEAD_DIM), BF16),
        scratch_shapes=[pltpu.VMEM((HEADS_PER_LANE_BLOCK, tile, LANES), F32),
                        pltpu.VMEM((HEADS_PER_LANE_BLOCK, tile, 1), F32)],
        compiler_params=_compiler_params(("arbitrary", "arbitrary", "arbitrary")),
        name="sb_attention",
    )(qkv, qkv, qkv)


def _even_out_kernel(sb_ref, pool_ref, prev_ref, pw_ref, ps_ref, wo_ref, x_ref, g_ref, o_ref,
                     *, tm, tiles_per_seq):
    i = pl.program_id(0)
    t0 = (i % tiles_per_seq) * tm
    t = t0 + lax.broadcasted_iota(I32, (tm, 1), 0)
    s = t0 - tm + lax.broadcasted_iota(I32, (1, 2 * tm), 1)
    chunk_end = ((t // CHUNK) + 1) * CHUNK
    group = LANES
    acc = jnp.dot(sb_ref[...], wo_ref[0:4 * group, :], preferred_element_type=F32)
    for g, w in enumerate(POOL_WINDOWS):
        lo = jnp.maximum(t - w // 2, 0)
        hi = jnp.minimum(t + (w - w // 2), chunk_end)
        band = ((s >= lo) & (s < hi)).astype(BF16)
        cur = pool_ref[:, g * group:(g + 1) * group]
        ext = jnp.concatenate([prev_ref[:, g * group:(g + 1) * group], cur], axis=0)
        window_sum = _dot_exact_lhs(band, ext)
        pooled = window_sum / (hi - lo).astype(F32) - cur
        y = jnp.dot(pooled.astype(BF16), pw_ref[g], preferred_element_type=F32)
        y = y * ps_ref[:, g * group:(g + 1) * group]
        acc += jnp.dot(y.astype(BF16), wo_ref[(4 + g) * group:(5 + g) * group, :],
                       preferred_element_type=F32)
    o_ref[...] = x_ref[...] + g_ref[0] * acc


def _even_out(sb2d, pool2d, seq, pool_w_bf16, pool_scale, w_out_bf16, x2d, gate):
    t, d = x2d.shape
    tm = TOKEN_TILE
    tiles_per_seq = seq // tm
    assert seq % CHUNK == 0 and len(POOL_WINDOWS) * LANES == pool2d.shape[1]
    pw = pool2d.shape[1]
    return pl.pallas_call(
        functools.partial(_even_out_kernel, tm=tm, tiles_per_seq=tiles_per_seq),
        grid=(t // tm,),
        in_specs=[pl.BlockSpec((tm, sb2d.shape[1]), lambda i: (i, 0)),
                  pl.BlockSpec((tm, pw), lambda i: (i, 0)),
                  pl.BlockSpec((tm, pw), lambda i: (jnp.maximum(i - 1, 0), 0)),
                  pl.BlockSpec(pool_w_bf16.shape, lambda i: (0, 0, 0)),
                  pl.BlockSpec((1, pw), lambda i: (0, 0)),
                  pl.BlockSpec((d, d), lambda i: (0, 0)),
                  pl.BlockSpec((tm, d), lambda i: (i, 0)),
                  pl.BlockSpec((1, 1, d), lambda i: (i // tiles_per_seq, 0, 0))],
        out_specs=pl.BlockSpec((tm, d), lambda i: (i, 0)),
        out_shape=jax.ShapeDtypeStruct((t, d), F32),
        compiler_params=_compiler_params(("arbitrary",)),
        name="even_out",
    )(sb2d, pool2d, pool2d, pool_w_bf16, pool_scale.reshape(1, pw), w_out_bf16, x2d, gate)


FOX_BIAS_TERMS = 3


def _fox_layout_constants(width, qk_bound):
    heads = width // HEAD_DIM
    nt = FOX_BIAS_TERMS
    wide = heads * LANES
    src = lax.broadcasted_iota(I32, (width, wide), 0)
    dst = lax.broadcasted_iota(I32, (width, wide), 1)
    place = ((dst // LANES == src // HEAD_DIM) & (dst % LANES == src % HEAD_DIM)).astype(BF16)
    src = lax.broadcasted_iota(I32, (nt * LANES, wide), 0)
    dst = lax.broadcasted_iota(I32, (nt * LANES, wide), 1)
    term, head = src // LANES, src % LANES
    place_dq = ((dst // LANES == head) & (dst % LANES == HEAD_DIM + term)).astype(BF16)
    place_dk = ((dst // LANES == head) & (dst % LANES == HEAD_DIM + nt + term)).astype(BF16)
    lane = lax.broadcasted_iota(I32, (1, wide), 1) % LANES
    bound_terms = _split_bf16(-qk_bound.astype(F32).reshape(1, 1), nt)
    q_const = jnp.where((lane >= HEAD_DIM + nt) & (lane < HEAD_DIM + 2 * nt), 1.0, 0.0)
    for k, term_k in enumerate(bound_terms):
        q_const = jnp.where(lane == HEAD_DIM + 2 * nt + k, term_k.astype(F32), q_const)
    k_const = jnp.where(((lane >= HEAD_DIM) & (lane < HEAD_DIM + nt))
                        | ((lane >= HEAD_DIM + 2 * nt) & (lane < HEAD_DIM + 3 * nt)), 1.0, 0.0)
    v_const = jnp.where(lane == HEAD_DIM, 1.0, 0.0)
    return place, place_dq, place_dk, q_const.astype(F32), k_const.astype(F32), v_const.astype(F32)


def _odd_prep_kernel(qk_ref, vin_ref, f_ref, rw_ref, rwprev_ref, hs_ref, tri_ref, ctri_ref,
                     place_ref, pdq_ref, pdk_ref, qc_ref, kc_ref, vc_ref,
                     qg_ref, kg_ref, fb_ref, mix_ref, w0_ref, w2_ref, a0_ref, a2_ref, g2_ref,
                     kkp_ref, ka_ref, rk_ref,
                     qa_ref, ka_out_ref, va_ref, drow_ref, rt_ref, kt_ref, at_ref, bt_ref, v_ref,
                     cw_ref, bonus_ref, g_ref, carry_ref, *, tm, tiles_per_seq, width):
    i = pl.program_id(0)
    first = (i % tiles_per_seq) == 0
    hs = hs_ref[...]

    def head_sum(x):
        return _dot_exact_rhs(x, hs, 2)

    inv_hd = 1.0 / HEAD_DIM
    q = qk_ref[:, :width]
    k = qk_ref[:, width:]
    qn = q * lax.rsqrt(head_sum(q * q) * inv_hd + NORM_EPS) * qg_ref[...] * (HEAD_DIM ** -0.5)
    kn = k * lax.rsqrt(head_sum(k * k) * inv_hd + NORM_EPS) * kg_ref[...]

    log_f = _log_sigmoid(f_ref[...] + fb_ref[...])

    @pl.when(first)
    def _():
        carry_ref[...] = jnp.zeros_like(carry_ref)

    dcum = _dot_exact_lhs(tri_ref[...], log_f) + carry_ref[...]
    carry_ref[...] = dcum[tm - 1:tm, :]
    drow_ref[0, 0] = jnp.transpose(dcum)[:SUBLANES, :]

    place = place_ref[...]
    d_terms = jnp.concatenate(_split_bf16(dcum, FOX_BIAS_TERMS), axis=1)
    qa_ref[...] = (jnp.dot(qn.astype(BF16), place, preferred_element_type=F32)
                   + jnp.dot(d_terms, pdq_ref[...], preferred_element_type=F32) + qc_ref[...]).astype(BF16)
    ka_out_ref[...] = (jnp.dot(kn.astype(BF16), place, preferred_element_type=F32)
                       + jnp.dot(-d_terms, pdk_ref[...], preferred_element_type=F32) + kc_ref[...]).astype(BF16)
    va_ref[...] = (jnp.dot(vin_ref[...], place, preferred_element_type=F32) + vc_ref[...]).astype(BF16)

    pd = rw_ref[...]
    prev_row = jnp.where(first, 0.0, rwprev_ref[SUBLANES - 1:SUBLANES, :])
    row0 = lax.broadcasted_iota(I32, (tm, 1), 0) == 0
    prev = jnp.where(row0, prev_row, pltpu.roll(pd, 1, axis=0))
    pd = pd + (prev - pd) * mix_ref[...]
    r = pd[:, :width]
    kr = pd[:, width:2 * width]
    vr = pd[:, 2 * width:3 * width]
    lora_in = pd[:, 3 * width:3 * width + LANES]
    xg = pd[:, 3 * width + LANES:3 * width + 2 * LANES]
    w_log = _log_sigmoid(w0_ref[...] + jnp.dot(jnp.tanh(lora_in).astype(BF16), w2_ref[...],
                                              preferred_element_type=F32)) - 0.5
    log_w = -jnp.exp(w_log)
    a = _sigmoid(a0_ref[...] + jnp.dot(lora_in.astype(BF16), a2_ref[...], preferred_element_type=F32))
    g_ref[...] = jnp.dot(_sigmoid(xg).astype(BF16), g2_ref[...], preferred_element_type=F32)
    kk = kr * kkp_ref[...]
    kk = kk / jnp.maximum(jnp.sqrt(head_sum(kk * kk)), 1e-12)
    k2 = kr * (1.0 + (a - 1.0) * ka_ref[...])
    bonus_ref[...] = head_sum(r * k2 * rk_ref[...]) * vr
    cw = _dot_exact_lhs(ctri_ref[...], log_w)
    grow = jnp.exp(-cw)
    rt_ref[...] = r * jnp.exp(cw)
    kt_ref[...] = k2 * grow
    bt_ref[...] = kk * a * grow
    at_ref[...] = -kk * jnp.exp(cw - log_w)
    v_ref[...] = vr
    cw_ref[...] = cw


def _odd_prep(qk, v, fproj, rw, seq, p, qk_bound):
    t = qk.shape[0]
    width = qk.shape[1] // 2
    tm = TOKEN_TILE
    tiles_per_seq = seq // tm
    nrw = rw.shape[1]
    assert tm % CHUNK == 0 and seq % tm == 0 and nrw == 3 * width + 2 * LANES
    assert HEAD_DIM + 3 * FOX_BIAS_TERMS <= LANES and width // HEAD_DIM <= SUBLANES
    ri = lax.broadcasted_iota(I32, (tm, tm), 0)
    ci = lax.broadcasted_iota(I32, (tm, tm), 1)
    tri = (ci <= ri).astype(BF16)
    ctri = ((ci <= ri) & (ci // CHUNK == ri // CHUNK)).astype(BF16)
    hs = _head_sum_matrix(width)
    row = lambda a: a.reshape(1, -1)
    full = lambda a: pl.BlockSpec(a.shape, lambda i: (0,) * a.ndim)
    tile = lambda n: pl.BlockSpec((tm, n), lambda i: (i, 0))
    consts = [hs, tri, ctri, *_fox_layout_constants(width, qk_bound),
              row(p['qg']), row(p['kg']), row(p['fb']), row(p['mix']), row(p['w0']),
              p['w2'], row(p['a0']), p['a2'], p['g2'], row(p['kk']), row(p['ka']), row(p['rk'])]
    wide = jax.ShapeDtypeStruct((t, width), F32)
    aug = (width // HEAD_DIM) * LANES
    return pl.pallas_call(
        functools.partial(_odd_prep_kernel, tm=tm, tiles_per_seq=tiles_per_seq, width=width),
        grid=(t // tm,),
        in_specs=[tile(2 * width), tile(width), tile(LANES), tile(nrw),
                  pl.BlockSpec((SUBLANES, nrw), lambda i: (jnp.maximum(i * (tm // SUBLANES) - 1, 0), 0))]
                 + [full(c) for c in consts],
        out_specs=[tile(aug), tile(aug), tile(aug),
                   pl.BlockSpec((1, 1, SUBLANES, tm), lambda i: (i // tiles_per_seq, i % tiles_per_seq, 0, 0))]
                  + [tile(width)] * 8,
        out_shape=[jax.ShapeDtypeStruct((t, aug), BF16)] * 3
                  + [jax.ShapeDtypeStruct((t // seq, tiles_per_seq, SUBLANES, tm), F32)] + [wide] * 8,
        scratch_shapes=[pltpu.VMEM((1, LANES), F32)],
        compiler_params=_compiler_params(("arbitrary",)),
        name="odd_prep",
    )(qk, v, fproj, rw, rw, *consts)


def _fox_attention_kernel(q_ref, k_ref, v_ref, drow_ref, o_ref, acc_ref, *, tile):
    hp = pl.program_id(1)
    i = pl.program_id(2)
    row = lax.broadcasted_iota(I32, (tile, tile), 0)
    col = lax.broadcasted_iota(I32, (tile, tile), 1)
    causal = col <= row
    sub = lax.broadcasted_iota(I32, (SUBLANES, tile), 0)
    hs = range(HEADS_PER_LANE_BLOCK)
    heads = [hp * HEADS_PER_LANE_BLOCK + h for h in hs]
    q = [q_ref[0, :, h * LANES:(h + 1) * LANES] for h in hs]

    def d_row(j, h):
        return jnp.sum(jnp.where(sub == heads[h], drow_ref[0, j], 0.0), axis=0, keepdims=True)

    def blocks(js, diag):
        work = [(pl.multiple_of(j * tile, tile), h) for j in js for h in hs]
        s = [lax.dot_general(q[h], k_ref[0, pl.ds(start, tile), h * LANES:(h + 1) * LANES],
                             (((1,), (1,)), ((), ())), preferred_element_type=F32) for start, h in work]
        p = [jnp.exp(x) for x in s]
        if diag:
            p = [jnp.where(causal, x, 0.0) for x in p]
        pv = [jnp.dot(p[n].astype(BF16), v_ref[0, pl.ds(start, tile), h * LANES:(h + 1) * LANES],
                      preferred_element_type=F32) for n, (start, h) in enumerate(work)]
        for h in hs:
            acc_ref[h] += functools.reduce(jnp.add, [pv[n] for n, (_, hh) in enumerate(work) if hh == h])

    acc_ref[...] = jnp.zeros_like(acc_ref)
    blocks([i], True)

    d_top = [jnp.max(d_row(i, h)) for h in hs]

    def alive(j):
        jc = jnp.maximum(j, 0)
        gaps = [d_top[h] - jnp.min(d_row(jc, h)) for h in hs]
        return functools.reduce(jnp.maximum, gaps) > F32_EXP_UNDERFLOW

    def cond(c):
        return jnp.logical_and(c[0] + 1 < i, c[1])

    def body(c):
        j = i - 1 - c[0]
        blocks([j, j - 1], False)
        return c[0] + 2, alive(j - 2)

    done, live = lax.while_loop(cond, body, (jnp.int32(0), alive(i - 1)))

    @pl.when(jnp.logical_and(done < i, live))
    def _():
        blocks([i - 1 - done], False)

    outs = [acc_ref[h][:, :HEAD_DIM] / acc_ref[h][:, HEAD_DIM:HEAD_DIM + 1] for h in hs]
    o_ref[0] = jnp.concatenate(outs, axis=1)


def _fox_qk_bound(q_gain, k_gain):
    bf16_slack = (1.0 + 2.0 ** -8) ** 2
    return (HEAD_DIM * HEAD_DIM ** -0.5 * bf16_slack) * jnp.max(jnp.abs(q_gain)) * jnp.max(jnp.abs(k_gain))


def _fox_attention(q_aug, k_aug, v_aug, drow):
    b, s, w = q_aug.shape
    nblk = w // (HEADS_PER_LANE_BLOCK * LANES)
    tile = ATTN_TILE
    assert drow.shape == (b, s // tile, SUBLANES, tile)
    pair = HEADS_PER_LANE_BLOCK * LANES
    return pl.pallas_call(
        functools.partial(_fox_attention_kernel, tile=tile),
        grid=(b, nblk, s // tile),
        in_specs=[pl.BlockSpec((1, tile, pair), lambda bi, hp, i: (bi, i, hp)),
                  pl.BlockSpec((1, s, pair), lambda bi, hp, i: (bi, 0, hp)),
                  pl.BlockSpec((1, s, pair), lambda bi, hp, i: (bi, 0, hp)),
                  pl.BlockSpec((1, s // tile, SUBLANES, tile), lambda bi, hp, i: (bi, 0, 0, 0))],
        out_specs=pl.BlockSpec((1, tile, LANES), lambda bi, hp, i: (bi, i, hp)),
        out_shape=jax.ShapeDtypeStruct((b, s, nblk * LANES), F32),
        scratch_shapes=[pltpu.VMEM((HEADS_PER_LANE_BLOCK, tile, LANES), F32)],
        compiler_params=_compiler_params(("arbitrary", "arbitrary", "arbitrary")),
        name="fox_attention",
    )(q_aug, k_aug, v_aug, drow)


_NN = (((1,), (0,)), ((), ()))
_NT = (((1,), (1,)), ((), ()))
_TN = (((0,), (0,)), ((), ()))


def _mm(a, b, dims=_NN):
    a_hi, a_lo = _split_bf16(a, 2)
    b_hi, b_lo = _split_bf16(b, 2)
    dot = lambda x, y: lax.dot_general(x, y, dims, preferred_element_type=F32)
    return dot(a_hi, b_hi) + (dot(a_hi, b_lo) + dot(a_lo, b_hi))


def _head_cols(h):
    return slice(h * HEAD_DIM, (h + 1) * HEAD_DIM)


def _rwkv_intra_kernel(rt_ref, kt_ref, at_ref, bt_ref, v_ref, inv_ref, arb_ref, fvu_ref, fvy_ref,
                       *, chunks, heads):
    c = CHUNK
    ri = lax.broadcasted_iota(I32, (2 * c, 2 * c), 0)
    ci = lax.broadcasted_iota(I32, (2 * c, 2 * c), 1)
    keep = (ci % c) < jnp.where(ri < c, ri, ri - c + 1)
    eye = (lax.broadcasted_iota(I32, (c, c), 0) == lax.broadcasted_iota(I32, (c, c), 1)).astype(F32)
    hs = range(heads)

    def chunk(cidx, carry):
        rows = pl.ds(pl.multiple_of(cidx * c, c), c)
        a_all = [jnp.where(keep,
                           _mm(jnp.concatenate([at_ref[rows, _head_cols(h)], rt_ref[rows, _head_cols(h)]], axis=0),
                               jnp.concatenate([bt_ref[rows, _head_cols(h)], kt_ref[rows, _head_cols(h)]], axis=0),
                               _NT), 0.0) for h in hs]
        a_ab = [a[:c, :c] for a in a_all]
        inv = [eye + a for a in a_ab]
        power = [_mm(a, a) for a in a_ab]
        for _ in range(4):
            both = [_mm(jnp.concatenate([inv[h], power[h]], axis=0), power[h]) for h in hs]
            inv = [inv[h] + both[h][:c] for h in hs]
            power = [both[h][c:] for h in hs]
        inv = [inv[h] + _mm(inv[h], power[h]) for h in hs]
        from_v = [_mm(a_all[h][:, c:], v_ref[rows, _head_cols(h)]) for h in hs]
        inv_ref[rows, :] = jnp.concatenate(inv, axis=1)
        arb_ref[rows, :] = jnp.concatenate([a[c:, :c] for a in a_all], axis=1)
        fvu_ref[rows, :] = jnp.concatenate([f[:c] for f in from_v], axis=1)
        fvy_ref[rows, :] = jnp.concatenate([f[c:] for f in from_v], axis=1)
        return carry

    lax.fori_loop(0, chunks, chunk, 0)


def _rwkv_scan_kernel(rt_ref, kt_ref, at_ref, bt_ref, v_ref, cw_ref, inv_ref, arb_ref, fvu_ref, fvy_ref,
                      y_ref, st_ref, *, steps_per_seq, chunks, heads):
    i = pl.program_id(0)

    @pl.when(i % steps_per_seq == 0)
    def _():
        st_ref[...] = jnp.zeros_like(st_ref)

    c = CHUNK
    hs = range(heads)

    def chunk(cidx, carry):
        rows = pl.ds(pl.multiple_of(cidx * c, c), c)
        last = pl.ds(pl.multiple_of(cidx * c, c) + (c - SUBLANES), SUBLANES)
        ar = [jnp.concatenate([at_ref[rows, _head_cols(h)], rt_ref[rows, _head_cols(h)]], axis=0) for h in hs]
        bk = [jnp.concatenate([bt_ref[rows, _head_cols(h)], kt_ref[rows, _head_cols(h)]], axis=0) for h in hs]
        st = [st_ref[h] for h in hs]
        from_state = [_mm(ar[h], st[h], _NT) for h in hs]
        u = [_mm(inv_ref[rows, _head_cols(h)], from_state[h][:c] + fvu_ref[rows, _head_cols(h)]) for h in hs]
        y = [from_state[h][c:] + fvy_ref[rows, _head_cols(h)] + _mm(arb_ref[rows, _head_cols(h)], u[h])
             for h in hs]
        for h in hs:
            decay = jnp.exp(cw_ref[last, _head_cols(h)][SUBLANES - 1:SUBLANES, :])
            uv = jnp.concatenate([u[h], v_ref[rows, _head_cols(h)]], axis=0)
            st_ref[h] = (st[h] + _mm(uv, bk[h], _TN)) * decay
        y_ref[rows, :] = jnp.concatenate(y, axis=1)
        return carry

    lax.fori_loop(0, chunks, chunk, 0)


def _rwkv_scan(rt, kt, at, bt, v, cw, seq):
    t, w = rt.shape
    heads = w // HEAD_DIM
    ts = 2 * TOKEN_TILE
    assert seq % ts == 0
    spec = pl.BlockSpec((ts, w), lambda i: (i, 0))
    wide = jax.ShapeDtypeStruct((t, w), F32)
    inv, arb, fvu, fvy = pl.pallas_call(
        functools.partial(_rwkv_intra_kernel, chunks=ts // CHUNK, heads=heads),
        grid=(t // ts,),
        in_specs=[spec] * 5,
        out_specs=[spec] * 4,
        out_shape=[wide] * 4,
        compiler_params=_compiler_params(("arbitrary",)),
        name="rwkv_intra",
    )(rt, kt, at, bt, v)
    return pl.pallas_call(
        functools.partial(_rwkv_scan_kernel, steps_per_seq=seq // ts, chunks=ts // CHUNK, heads=heads),
        grid=(t // ts,),
        in_specs=[spec] * 10,
        out_specs=spec,
        out_shape=wide,
        scratch_shapes=[pltpu.VMEM((heads, HEAD_DIM, HEAD_DIM), F32)],
        compiler_params=_compiler_params(("arbitrary",)),
        name="rwkv_scan",
    )(rt, kt, at, bt, v, cw, inv, arb, fvu, fvy)


def _odd_out_kernel(att_ref, og_ref, y_ref, bonus_ref, g_ref, hs_ref, lnw_ref, lnb_ref, wo_ref,
                    x_ref, gate_ref, o_ref, *, width):
    hs = hs_ref[...]
    inv_hd = 1.0 / HEAD_DIM
    fox = att_ref[...] * _sigmoid(og_ref[...])
    y = y_ref[...]
    mu = _dot_exact_rhs(y, hs, 2) * inv_hd
    yc = y - mu
    var = _dot_exact_rhs(yc * yc, hs, 2) * inv_hd
    yn = yc * lax.rsqrt(var + RWKV_GN_EPS) * lnw_ref[...] + lnb_ref[...]
    rw = (yn + bonus_ref[...]) * g_ref[...]
    acc = jnp.dot(fox.astype(BF16), wo_ref[:width, :], preferred_element_type=F32)
    acc += jnp.dot(rw.astype(BF16), wo_ref[width:, :], preferred_element_type=F32)
    o_ref[...] = x_ref[...] + gate_ref[0] * acc


def _odd_out(att, og, y, bonus, g, ln_w, ln_b, w_out_bf16, x2d, seq, gate):
    t, d = x2d.shape
    width = att.shape[1]
    tm = TOKEN_TILE
    tiles_per_seq = seq // tm
    hs = _head_sum_matrix(width)
    tile = lambda n: pl.BlockSpec((tm, n), lambda i: (i, 0))
    return pl.pallas_call(
        functools.partial(_odd_out_kernel, width=width),
        grid=(t // tm,),
        in_specs=[tile(width)] * 5
                 + [pl.BlockSpec(hs.shape, lambda i: (0, 0)),
                    pl.BlockSpec((1, width), lambda i: (0, 0)), pl.BlockSpec((1, width), lambda i: (0, 0)),
                    pl.BlockSpec((d, d), lambda i: (0, 0)), tile(d),
                    pl.BlockSpec((1, 1, d), lambda i: (i // tiles_per_seq, 0, 0))],
        out_specs=tile(d),
        out_shape=jax.ShapeDtypeStruct((t, d), F32),
        compiler_params=_compiler_params(("arbitrary",)),
        name="odd_out",
    )(att, og, y, bonus, g, hs, ln_w.reshape(1, width), ln_b.reshape(1, width), w_out_bf16, x2d, gate)


def _top_rows(s, k, prio, payload=None):
    big = jnp.int32(2 ** 30)
    vals, picks = [], []
    for _ in range(k):
        m = jnp.max(s, axis=0, keepdims=True)
        am = jnp.min(jnp.where(s == m, prio, big), axis=0, keepdims=True)
        hit = prio == am
        vals.append(m)
        if payload is None:
            picks.append(am)
        else:
            picks.append(jnp.max(jnp.where(hit, payload, -1), axis=0, keepdims=True))
        s = jnp.where(hit, -jnp.inf, s)
    return jnp.concatenate(vals, axis=0), jnp.concatenate(picks, axis=0)


def _staircase_groups():
    k, g = PEER_TOPK, SUBLANES
    groups = []
    for b in range(k):
        count = k // (b + 1)
        if count <= 1:
            break
        for a0 in range(0, count, g):
            groups.append((a0, b, False, min(g, count - a0)))
    first_single = next(b for b in range(k) if k // (b + 1) <= 1)
    for b0 in range(first_single, k, g):
        groups.append((0, b0, True, min(g, k - b0)))
    return groups


def _peer_topk_kernel(q_ref, k1_ref, k2_ref, idx_ref, gate_ref, *, row_scale):
    half = PEER_NKEYS
    nt = (((1,), (1,)), ((), ()))
    tt = q_ref.shape[0]
    key_id = lax.broadcasted_iota(I32, (half, tt), 0)
    sub = lax.broadcasted_iota(I32, (SUBLANES, 1), 0)
    groups = _staircase_groups()
    prio = jnp.concatenate(
        [jnp.broadcast_to((sub * 0 + a0) * PEER_TOPK + b0 + sub if along_b
                          else (a0 + sub) * PEER_TOPK + b0, (SUBLANES, tt))
         for a0, b0, along_b, _ in groups], axis=0)
    idx_rows = []
    for h in range(PEER_HEADS):
        q1 = q_ref[:, (2 * h) * half:(2 * h + 1) * half]
        q2 = q_ref[:, (2 * h + 1) * half:(2 * h + 2) * half]
        s1 = lax.dot_general(k1_ref[...], q1, nt, precision=HIGHEST, preferred_element_type=F32)
        s2 = lax.dot_general(k2_ref[...], q2, nt, precision=HIGHEST, preferred_element_type=F32)
        v1, i1 = _top_rows(s1, PEER_TOPK, key_id)
        v2, i2 = _top_rows(s2, PEER_TOPK, key_id)
        cand, cand_idx = [], []
        for a0, b0, along_b, valid in groups:
            if along_b:
                val = v1[a0:a0 + 1] + v2[b0:b0 + SUBLANES]
                eid = i1[a0:a0 + 1] * PEER_NKEYS + i2[b0:b0 + SUBLANES]
            else:
                val = v1[a0:a0 + SUBLANES] + v2[b0:b0 + 1]
                eid = i1[a0:a0 + SUBLANES] * PEER_NKEYS + i2[b0:b0 + 1]
            cand.append(jnp.where(sub < valid, val, -jnp.inf))
            cand_idx.append(eid)
        top_s, idx = _top_rows(jnp.concatenate(cand, axis=0), PEER_TOPK, prio,
                               payload=jnp.concatenate(cand_idx, axis=0))
        e = jnp.exp(top_s - top_s[0:1])
        gate_ref[h * PEER_TOPK:(h + 1) * PEER_TOPK, :] = e / jnp.sum(e, axis=0, keepdims=True)
        idx_rows.append(idx)
    idx_ref[...] = jnp.transpose(jnp.concatenate(idx_rows, axis=0)) * row_scale


def _peer_topk(q2d, k1, k2, row_scale):
    t, n = q2d.shape
    tt = TOKEN_TILE
    assert PEER_TOPK % SUBLANES == 0
    return pl.pallas_call(
        functools.partial(_peer_topk_kernel, row_scale=row_scale),
        grid=(t // tt,),
        in_specs=[pl.BlockSpec((tt, n), lambda i: (i, 0)),
                  pl.BlockSpec(k1.shape, lambda i: (0, 0)),
                  pl.BlockSpec(k2.shape, lambda i: (0, 0))],
        out_specs=[pl.BlockSpec((tt, PEER_SLOTS), lambda i: (i, 0)),
                   pl.BlockSpec((PEER_SLOTS, tt), lambda i: (0, i))],
        out_shape=[jax.ShapeDtypeStruct((t, PEER_SLOTS), I32),
                   jax.ShapeDtypeStruct((PEER_SLOTS, t), F32)],
        compiler_params=_compiler_params(("arbitrary",)),
        name="peer_topk",
    )(q2d, k1, k2)


def _pack_table_kernel(t_ref, o_ref, *, pairs):
    high = jnp.int32(-65536)
    for j in range(pairs):
        lo = t_ref[:, (2 * j) * LANES:(2 * j + 1) * LANES].astype(BF16).astype(F32)
        hi = t_ref[:, (2 * j + 1) * LANES:(2 * j + 2) * LANES].astype(BF16).astype(F32)
        lo_bits = lax.shift_right_logical(pltpu.bitcast(lo, I32), 16)
        words = lax.bitwise_or(lo_bits, lax.bitwise_and(pltpu.bitcast(hi, I32), high))
        o_ref[pl.ds(j, t_ref.shape[0], stride=pairs), :] = words


def _pack_table(table):
    e, d = table.shape
    pairs = d // (2 * LANES)
    rows = 2 * TOKEN_TILE
    return pl.pallas_call(
        functools.partial(_pack_table_kernel, pairs=pairs),
        grid=(e // rows,),
        in_specs=[pl.BlockSpec((rows, d), lambda i: (i, 0))],
        out_specs=pl.BlockSpec((rows * pairs, LANES), lambda i: (i, 0)),
        out_shape=jax.ShapeDtypeStruct((e * pairs, LANES), I32),
        compiler_params=_compiler_params(("arbitrary",)),
        name="pack_table",
    )(table)


def _unpack_words(w):
    lo = pltpu.bitcast(lax.shift_left(w, 16), F32)
    hi = pltpu.bitcast(lax.bitwise_and(w, jnp.int32(-65536)), F32)
    return lo, hi


def _gather_rows(idx_ref, tab_ref, tile_refs, toks, pairs):
    tok_idx = [idx_ref.at[tok] for tok in toks]
    for m in range(PEER_SLOTS):
        for rows, tile_ref in zip(tok_idx, tile_refs):
            row = pl.multiple_of(rows[m], pairs)
            tile_ref[m * pairs:(m + 1) * pairs, :] = tab_ref[pl.ds(row, pairs), :]


def _tile_chunk(tile_ref, j, pairs):
    return _unpack_words(tile_ref[pl.ds(j, PEER_SLOTS, stride=pairs), :])


PEER_TILE_BUFFERS = 8


def _pipelined_tokens(ntok, gather, compute, init):
    nb = PEER_TILE_BUFFERS
    group = nb // 2
    assert ntok % nb == 0
    gather(list(range(group)), list(range(group)))

    def trip(p, carry):
        tok = nb * p
        for half in range(2):
            first = tok + half * group
            ahead = [jnp.minimum(first + group + k, ntok - 1) for k in range(group)]
            gather(ahead, [((half + 1) % 2) * group + k for k in range(group)])
            for k in range(group):
                carry = compute(first + k, half * group + k, carry)
        return carry

    return lax.fori_loop(0, ntok // nb, trip, init)


def _pick_row(block, r):
    sub = lax.broadcasted_iota(I32, block.shape, 0)
    return jnp.sum(jnp.where(sub == r, block, 0.0), axis=0, keepdims=True)


def _peer_act_kernel(idx_ref, tab_ref, h_ref, gate_ref, coef_ref, *tiles, pairs):
    ntok = h_ref.shape[0]
    lane = lax.broadcasted_iota(I32, (PEER_SLOTS, ntok), 1)

    def gather(toks, bufs):
        _gather_rows(idx_ref, tab_ref, [tiles[b] for b in bufs], toks, pairs)

    def compute(tok, buf, carry):
        base = pl.multiple_of((tok // SUBLANES) * SUBLANES, SUBLANES)
        h_row = _pick_row(h_ref[pl.ds(base, SUBLANES), :], tok % SUBLANES)
        terms = []
        for j in range(pairs):
            lo, hi = _tile_chunk(tiles[buf], j, pairs)
            h_lo = h_row[:, (2 * j) * LANES:(2 * j + 1) * LANES]
            h_hi = h_row[:, (2 * j + 1) * LANES:(2 * j + 2) * LANES]
            terms.append(lo * h_lo + hi * h_hi)
        while len(terms) > 1:
            terms = [a + b for a, b in zip(terms[::2], terms[1::2])]
        col = jnp.sum(terms[0], axis=1, keepdims=True)
        coef_ref[...] = jnp.where(lane == tok, col, coef_ref[...])
        return carry

    coef_ref[...] = jnp.zeros_like(coef_ref)
    _pipelined_tokens(ntok, gather, compute, 0)
    act_t = coef_ref[...]
    gelu = 0.5 * act_t * (1.0 + lax.erf(act_t * (2.0 ** -0.5)))
    coef_ref[...] = gate_ref[...] * gelu


def _peer_act(idx, table_words, h2d, gate_t):
    t, d = h2d.shape
    pairs = d // (2 * LANES)
    nt = PEER_TOKENS
    return pl.pallas_call(
        functools.partial(_peer_act_kernel, pairs=pairs),
        grid=(t // nt,),
        in_specs=[pl.BlockSpec((nt, PEER_SLOTS), lambda i: (i, 0), memory_space=pltpu.SMEM),
                  pl.BlockSpec(table_words.shape, lambda i: (0, 0), pipeline_mode=pl.Buffered(1)),
                  pl.BlockSpec((nt, d), lambda i: (i, 0)),
                  pl.BlockSpec((PEER_SLOTS, nt), lambda i: (0, i))],
        out_specs=pl.BlockSpec((PEER_SLOTS, nt), lambda i: (0, i)),
        out_shape=jax.ShapeDtypeStruct((PEER_SLOTS, t), F32),
        scratch_shapes=[pltpu.VMEM((pairs * PEER_SLOTS, LANES), I32)] * PEER_TILE_BUFFERS,
        compiler_params=_compiler_params(("arbitrary",), vmem_mb=56),
        name="peer_act",
    )(idx, table_words, h2d, gate_t)


def _peer_out_kernel(idx_ref, tab_ref, coef_ref, x_ref, g_ref, o_ref, *tiles, pairs):
    ntok, d = x_ref.shape
    lane = lax.broadcasted_iota(I32, (PEER_SLOTS, ntok), 1)
    sub = lax.broadcasted_iota(I32, (SUBLANES, d), 0)
    groups = PEER_SLOTS // SUBLANES

    def gather(toks, bufs):
        _gather_rows(idx_ref, tab_ref, [tiles[b] for b in bufs], toks, pairs)

    def compute(tok, buf, rows):
        coef = jnp.sum(jnp.where(lane == tok, coef_ref[...], 0.0), axis=1, keepdims=True)
        pieces = []
        for j in range(pairs):
            for vals in _tile_chunk(tiles[buf], j, pairs):
                prod = (vals * coef).reshape(groups, SUBLANES, LANES)
                pieces.append(jnp.sum(jnp.sum(prod, axis=0), axis=0, keepdims=True))
        row = jnp.concatenate(pieces, axis=1)
        rows = jnp.where(sub == tok % SUBLANES, row, rows)
        base = pl.multiple_of((tok // SUBLANES) * SUBLANES, SUBLANES)
        o_ref[pl.ds(base, SUBLANES), :] = x_ref[pl.ds(base, SUBLANES), :] + g_ref[0] * rows
        return rows

    _pipelined_tokens(ntok, gather, compute, jnp.zeros((SUBLANES, d), F32))


def _peer_out(idx, table_words, coef_t, x2d, seq, gate):
    t, d = x2d.shape
    pairs = d // (2 * LANES)
    nt = PEER_TOKENS
    per_seq = seq // nt
    return pl.pallas_call(
        functools.partial(_peer_out_kernel, pairs=pairs),
        grid=(t // nt,),
        in_specs=[pl.BlockSpec((nt, PEER_SLOTS), lambda i: (i, 0), memory_space=pltpu.SMEM),
                  pl.BlockSpec(table_words.shape, lambda i: (0, 0), pipeline_mode=pl.Buffered(1)),
                  pl.BlockSpec((PEER_SLOTS, nt), lambda i: (0, i)),
                  pl.BlockSpec((nt, d), lambda i: (i, 0)),
                  pl.BlockSpec((1, 1, d), lambda i: (i // per_seq, 0, 0))],
        out_specs=pl.BlockSpec((nt, d), lambda i: (i, 0)),
        out_shape=jax.ShapeDtypeStruct((t, d), F32),
        scratch_shapes=[pltpu.VMEM((pairs * PEER_SLOTS, LANES), I32)] * PEER_TILE_BUFFERS,
        compiler_params=_compiler_params(("arbitrary",), vmem_mb=56),
        name="peer_out",
    )(idx, table_words, coef_t, x2d, gate)


def _peer_ffn(x2d, seq, gain, shift, scale, gate, wq_bf16, k1, k2, u_words, v_words):
    n = wq_bf16.shape[1]
    q, h = _norm_proj(x2d, seq, gain, shift, scale, wq_bf16, [(0, n)], [F32], emit_h=True)
    idx, gate_t = _peer_topk(q, k1, k2, x2d.shape[1] // (2 * LANES))
    coef_t = _peer_act(idx, u_words, h, gate_t)
    return _peer_out(idx, v_words, coef_t, x2d, seq, gate)


def _final_norm_kernel(x_ref, gain_ref, o_ref):
    x = x_ref[...]
    ms = jnp.mean(x * x, axis=-1, keepdims=True)
    o_ref[...] = x * lax.rsqrt(ms + NORM_EPS) * gain_ref[...]


def _final_norm(x2d, gain):
    t, d = x2d.shape
    tm = TOKEN_TILE
    return pl.pallas_call(
        _final_norm_kernel,
        grid=(t // tm,),
        in_specs=[pl.BlockSpec((tm, d), lambda i: (i, 0)), pl.BlockSpec((1, d), lambda i: (0, 0))],
        out_specs=pl.BlockSpec((tm, d), lambda i: (i, 0)),
        out_shape=jax.ShapeDtypeStruct((t, d), F32),
        compiler_params=_compiler_params(("arbitrary",)),
        name="final_norm",
    )(x2d, gain.reshape(1, d))


def _even_mixer(x2d, b, seq, gain, shift, scale, gate, w_in, w_out, pool_w, pool_scale):
    d = x2d.shape[1]
    pool_width = len(POOL_WINDOWS) * LANES
    sb_width = (w_in.shape[1] - pool_width) // 3
    qkv, pool = _norm_proj(x2d, seq, gain, shift, scale, w_in.astype(BF16),
                           [(0, 3 * sb_width), (3 * sb_width, pool_width)], [BF16, F32])
    sb = _sb_attention(qkv.reshape(b, seq, 3 * sb_width), sb_width // HEAD_DIM)
    return _even_out(sb.reshape(b * seq, sb_width), pool, seq, pool_w.astype(BF16), pool_scale,
                     w_out.astype(BF16), x2d, gate)


def _odd_mixer(x2d, b, seq, gain, shift, scale, gate, w_in, w_out, fox_qnorm, fox_knorm, fox_fbias,
               rwkv_mix, rwkv_w0, rwkv_w2, rwkv_a0, rwkv_a2, rwkv_g2, rwkv_kk, rwkv_ka, rwkv_rk,
               rwkv_ln_w, rwkv_ln_b):
    d = x2d.shape[1]
    width = rwkv_w0.shape[0]
    heads = width // HEAD_DIM
    fox_in = 4 * width + heads
    n_rw = w_in.shape[1] - fox_in
    w_cols = jnp.concatenate([w_in[:, :4 * width], w_in[:, fox_in:], w_in[:, 4 * width:fox_in],
                              jnp.zeros((d, LANES - heads), w_in.dtype)], axis=1).astype(BF16)
    qk, v, og, rw, fproj = _norm_proj(
        x2d, seq, gain, shift, scale, w_cols,
        [(0, 2 * width), (2 * width, width), (3 * width, width), (4 * width, n_rw), (4 * width + n_rw, LANES)],
        [F32, BF16, F32, F32, F32])
    lora = rwkv_w2.shape[0]
    assert 2 * lora == LANES and rwkv_a2.shape[0] == lora and rwkv_g2.shape[0] == LANES
    zeros = jnp.zeros((lora, width), F32)
    params = dict(
        qg=jnp.tile(fox_qnorm, heads), kg=jnp.tile(fox_knorm, heads),
        fb=jnp.concatenate([fox_fbias, jnp.zeros((LANES - heads,), F32)]),
        mix=rwkv_mix, w0=rwkv_w0, a0=rwkv_a0,
        w2=jnp.concatenate([rwkv_w2, zeros], axis=0).astype(BF16),
        a2=jnp.concatenate([zeros, rwkv_a2], axis=0).astype(BF16),
        g2=rwkv_g2.astype(BF16), kk=rwkv_kk, ka=rwkv_ka, rk=rwkv_rk.reshape(-1))
    q_aug, k_aug, v_aug, drow, rt, kt, at, bt, vv, cw, bonus, g = _odd_prep(
        qk, v, fproj, rw, seq, params, _fox_qk_bound(fox_qnorm, fox_knorm))
    r3 = lambda a: a.reshape(b, seq, -1)
    att = _fox_attention(r3(q_aug), r3(k_aug), r3(v_aug), drow)
    y = _rwkv_scan(rt, kt, at, bt, vv, cw, seq)
    return _odd_out(att.reshape(b * seq, width), og, y, bonus, g, rwkv_ln_w, rwkv_ln_b,
                    w_out.astype(BF16), x2d, seq, gate)


def kernel(x, c, ada_w, ada_b, norm_mix, norm_ffn, ev_w_in, ev_w_out, pool_w, pool_scale, od_w_in, od_w_out, fox_qnorm, fox_knorm, fox_fbias, rwkv_mix, rwkv_w0, rwkv_w2, rwkv_a0, rwkv_a2, rwkv_g2, rwkv_kk, rwkv_ka, rwkv_rk, rwkv_ln_w, rwkv_ln_b, peer_wq, peer_k1, peer_k2, peer_u, peer_v, final_norm):
    b, seq, d = x.shape
    depth = ada_w.shape[0]
    assert TOKEN_TILE == ATTN_TILE and seq % (2 * TOKEN_TILE) == 0
    mod = _adaln(c, ada_w, ada_b).reshape(depth, b, 6, 1, d)
    x2d = x.reshape(b * seq, d)
    for layer in range(depth):
        sh1, sc1, g1, sh2, sc2, g2 = (mod[layer, :, i] for i in range(6))
        j = layer // 2
        if layer % 2 == 0:
            x2d = _even_mixer(x2d, b, seq, norm_mix[layer], sh1, sc1, g1, ev_w_in[j], ev_w_out[j],
                              pool_w[j], pool_scale[j])
        else:
            x2d = _odd_mixer(x2d, b, seq, norm_mix[layer], sh1, sc1, g1, od_w_in[j], od_w_out[j],
                             fox_qnorm[j], fox_knorm[j], fox_fbias[j], rwkv_mix[j], rwkv_w0[j],
                             rwkv_w2[j], rwkv_a0[j], rwkv_a2[j], rwkv_g2[j], rwkv_kk[j], rwkv_ka[j],
                             rwkv_rk[j], rwkv_ln_w[j], rwkv_ln_b[j])
        x2d = _peer_ffn(x2d, seq, norm_ffn[layer], sh2, sc2, g2, peer_wq[layer].astype(BF16),
                        peer_k1[layer], peer_k2[layer], _pack_table(peer_u[layer]),
                        _pack_table(peer_v[layer]))
    return _final_norm(x2d, final_norm).reshape(b, seq, d)
```

```python
import functools

import jax
import jax.numpy as jnp
from jax import lax
from jax.experimental import pallas as pl
from jax.experimental.pallas import tpu as pltpu

F32 = jnp.float32
BF16 = jnp.bfloat16
I32 = jnp.int32
HIGHEST = lax.Precision.HIGHEST

LANES = 128
SUBLANES = 8
VMEM_BYTES_V7X = 64 * 1024 * 1024

HEAD_DIM = 64
HEADS_PER_LANE_BLOCK = LANES // HEAD_DIM
NORM_EPS = 1e-6
RWKV_GN_EPS = 64e-5
CHUNK = 64
POOL_WINDOWS = (2, 4, 8, 16)
PEER_TOPK = 16
PEER_NKEYS = 128
PEER_HEADS = 8
PEER_SLOTS = PEER_HEADS * PEER_TOPK

TOKEN_TILE = 256
ATTN_TILE = 256
PEER_TOKENS = 128

F32_EXP_UNDERFLOW = -104.0


def _compiler_params(semantics, vmem_mb=48):
    return pltpu.CompilerParams(dimension_semantics=semantics,
                                vmem_limit_bytes=min(vmem_mb * 1024 * 1024, VMEM_BYTES_V7X))


def _split_bf16(x, n):
    parts = []
    r = x
    for _ in range(n):
        p = r.astype(BF16)
        parts.append(p)
        r = r - p.astype(F32)
    return parts


def _dot_exact_rhs(x, m, n=3):
    acc = None
    for p in _split_bf16(x, n):
        t = jnp.dot(p, m, preferred_element_type=F32)
        acc = t if acc is None else acc + t
    return acc


def _dot_exact_lhs(m, x, n=3):
    acc = None
    for p in _split_bf16(x, n):
        t = jnp.dot(m, p, preferred_element_type=F32)
        acc = t if acc is None else acc + t
    return acc


def _log_sigmoid(z):
    return jnp.minimum(z, 0.0) - jnp.log1p(jnp.exp(-jnp.abs(z)))


def _sigmoid(z):
    return 1.0 / (1.0 + jnp.exp(-z))


def _head_sum_matrix(width):
    r = lax.broadcasted_iota(I32, (width, width), 0) // HEAD_DIM
    c = lax.broadcasted_iota(I32, (width, width), 1) // HEAD_DIM
    return (r == c).astype(BF16)


def _adaln_kernel(c_ref, w_ref, b_ref, o_ref):
    c = c_ref[...]
    c_act = c * _sigmoid(c)
    o_ref[0] = jnp.dot(c_act, w_ref[0], precision=HIGHEST, preferred_element_type=F32) + b_ref[0]


def _adaln(c, ada_w, ada_b):
    b, d = c.shape
    depth, _, n = ada_w.shape
    tn = 1536
    assert n % tn == 0 and b <= SUBLANES
    c_pad = jnp.zeros((SUBLANES, d), F32).at[:b].set(c)
    out = pl.pallas_call(
        _adaln_kernel,
        grid=(depth, n // tn),
        in_specs=[pl.BlockSpec((SUBLANES, d), lambda l, j: (0, 0)),
                  pl.BlockSpec((1, d, tn), lambda l, j: (l, 0, j)),
                  pl.BlockSpec((1, 1, tn), lambda l, j: (l, 0, j))],
        out_specs=pl.BlockSpec((1, SUBLANES, tn), lambda l, j: (l, 0, j)),
        out_shape=jax.ShapeDtypeStruct((depth, SUBLANES, n), F32),
        compiler_params=_compiler_params(("arbitrary", "arbitrary")),
        name="adaln",
    )(c_pad, ada_w, ada_b.reshape(depth, 1, n))
    return out[:, :b, :]


def _norm_proj_kernel(x_ref, gain_ref, shift_ref, scale_ref, w_ref, *out_refs, groups, emit_h):
    x = x_ref[...]
    ms = jnp.mean(x * x, axis=-1, keepdims=True)
    y = x * lax.rsqrt(ms + NORM_EPS) * gain_ref[...]
    h = y * (1.0 + scale_ref[0]) + shift_ref[0]
    hb = h.astype(BF16)
    for (start, width), o_ref in zip(groups, out_refs):
        res = jnp.dot(hb, w_ref[:, start:start + width], preferred_element_type=F32).astype(o_ref.dtype)
        if len(o_ref.shape) == 3:
            per_head = o_ref.shape[2]
            for hd in range(o_ref.shape[0]):
                o_ref[hd] = res[:, hd * per_head:(hd + 1) * per_head]
        else:
            o_ref[...] = res
    if emit_h:
        out_refs[-1][...] = h


def _norm_proj(x2d, seq, gain, shift, scale, w_bf16, groups, dtypes, emit_h=False, head_major=0):
    t, d = x2d.shape
    n = w_bf16.shape[1]
    tm = TOKEN_TILE
    assert t % tm == 0 and seq % tm == 0
    tiles_per_seq = seq // tm
    out_shape = [jax.ShapeDtypeStruct((t, wd), dt) for (_, wd), dt in zip(groups, dtypes)]
    out_specs = [pl.BlockSpec((tm, wd), lambda i: (i, 0)) for (_, wd) in groups]
    if head_major:
        (_, wd), = groups
        out_shape = [jax.ShapeDtypeStruct((head_major, t, wd // head_major), dtypes[0])]
        out_specs = [pl.BlockSpec((head_major, tm, wd // head_major), lambda i: (0, i, 0))]
    if emit_h:
        out_shape.append(jax.ShapeDtypeStruct((t, d), F32))
        out_specs.append(pl.BlockSpec((tm, d), lambda i: (i, 0)))
    return pl.pallas_call(
        functools.partial(_norm_proj_kernel, groups=tuple(groups), emit_h=emit_h),
        grid=(t // tm,),
        in_specs=[pl.BlockSpec((tm, d), lambda i: (i, 0)),
                  pl.BlockSpec((1, d), lambda i: (0, 0)),
                  pl.BlockSpec((1, 1, d), lambda i: (i // tiles_per_seq, 0, 0)),
                  pl.BlockSpec((1, 1, d), lambda i: (i // tiles_per_seq, 0, 0)),
                  pl.BlockSpec((d, n), lambda i: (0, 0))],
        out_specs=out_specs,
        out_shape=out_shape,
        compiler_params=_compiler_params(("arbitrary",)),
        name="norm_proj",
    )(x2d, gain.reshape(1, d), shift, scale, w_bf16)


def _sb_attention_kernel(q_ref, k_ref, v_ref, o_ref, acc_ref, carry_ref, *, tile, scale):
    i = pl.program_id(2)
    lane = lax.broadcasted_iota(I32, (1, LANES), 1)
    row = lax.broadcasted_iota(I32, (tile, tile), 0)
    col = lax.broadcasted_iota(I32, (tile, tile), 1)
    strict = col < row
    later = (row > col).astype(BF16)
    q = q_ref[0]
    hs = range(HEADS_PER_LANE_BLOCK)
    qh = [jnp.where((lane >= HEAD_DIM * h) & (lane < HEAD_DIM * (h + 1)), q, jnp.zeros_like(q)) for h in hs]

    def block(j, diag):
        start = pl.multiple_of(j * tile, tile)
        kb = k_ref[0, pl.ds(start, tile), :]
        vb = v_ref[0, pl.ds(start, tile), :]
        z = [lax.dot_general(qh[h], kb, (((1,), (1,)), ((), ())), preferred_element_type=F32) * scale
             for h in hs]
        log_beta = [_log_sigmoid(x) for x in z]
        log_1m = [log_beta[h] - z[h] for h in hs]
        if diag:
            log_1m = [jnp.where(strict, x, 0.0) for x in log_1m]
        stick = [_dot_exact_rhs(log_1m[h], later, 2) + carry_ref[h] for h in hs]
        w = [jnp.exp(log_beta[h] + stick[h]) for h in hs]
        if diag:
            w = [jnp.where(strict, x, 0.0) for x in w]
        for h in hs:
            acc_ref[h] += jnp.dot(w[h].astype(BF16), vb, preferred_element_type=F32)
            carry_ref[h] += jnp.sum(log_1m[h], axis=1, keepdims=True)

    acc_ref[...] = jnp.zeros_like(acc_ref)
    carry_ref[...] = jnp.zeros_like(carry_ref)
    block(i, True)

    def alive():
        return jnp.max(carry_ref[...]) > F32_EXP_UNDERFLOW

    def cond(c):
        return jnp.logical_and(c[0] < i, c[1])

    def body(c):
        block(i - 1 - c[0], False)
        return c[0] + 1, alive()

    lax.while_loop(cond, body, (jnp.int32(0), alive()))
    o_ref[0] = jnp.where(lane < HEAD_DIM, acc_ref[0], acc_ref[1]).astype(o_ref.dtype)


def _sb_attention(qkv, n_heads):
    b, s, _ = qkv.shape
    nblk = n_heads // HEADS_PER_LANE_BLOCK
    tile = ATTN_TILE
    assert s % tile == 0
    return pl.pallas_call(
        functools.partial(_sb_attention_kernel, tile=tile, scale=HEAD_DIM ** -0.5),
        grid=(b, nblk, s // tile),
        in_specs=[pl.BlockSpec((1, tile, LANES), lambda bi, hp, i: (bi, i, hp)),
                  pl.BlockSpec((1, s, LANES), lambda bi, hp, i: (bi, 0, nblk + hp)),
                  pl.BlockSpec((1, s, LANES), lambda bi, hp, i: (bi, 0, 2 * nblk + hp))],
        out_specs=pl.BlockSpec((1, tile, LANES), lambda bi, hp, i: (bi, i, hp)),
        out_shape=jax.ShapeDtypeStruct((b, s, n_heads * HEAD_DIM), BF16),
        scratch_shapes=[pltpu.VMEM((HEADS_PER_LANE_BLOCK, tile, LANES), F32),
                        pltpu.VMEM((HEADS_PER_LANE_BLOCK, tile, 1), F32)],
        compiler_params=_compiler_params(("arbitrary", "arbitrary", "arbitrary")),
        name="sb_attention",
    )(qkv, qkv, qkv)


def _even_out_kernel(sb_ref, pool_ref, prev_ref, pw_ref, ps_ref, wo_ref, x_ref, g_ref, o_ref,
                     *, tm, tiles_per_seq):
    i = pl.program_id(0)
    t0 = (i % tiles_per_seq) * tm
    t = t0 + lax.broadcasted_iota(I32, (tm, 1), 0)
    s = t0 - tm + lax.broadcasted_iota(I32, (1, 2 * tm), 1)
    chunk_end = ((t // CHUNK) + 1) * CHUNK
    group = LANES
    acc = jnp.dot(sb_ref[...], wo_ref[0:4 * group, :], preferred_element_type=F32)
    for g, w in enumerate(POOL_WINDOWS):
        lo = jnp.maximum(t - w // 2, 0)
        hi = jnp.minimum(t + (w - w // 2), chunk_end)
        band = ((s >= lo) & (s < hi)).astype(BF16)
        cur = pool_ref[:, g * group:(g + 1) * group]
        ext = jnp.concatenate([prev_ref[:, g * group:(g + 1) * group], cur], axis=0)
        window_sum = _dot_exact_lhs(band, ext)
        pooled = window_sum / (hi - lo).astype(F32) - cur
        y = jnp.dot(pooled.astype(BF16), pw_ref[g], preferred_element_type=F32)
        y = y * ps_ref[:, g * group:(g + 1) * group]
        acc += jnp.dot(y.astype(BF16), wo_ref[(4 + g) * group:(5 + g) * group, :],
                       preferred_element_type=F32)
    o_ref[...] = x_ref[...] + g_ref[0] * acc


def _even_out(sb2d, pool2d, seq, pool_w_bf16, pool_scale, w_out_bf16, x2d, gate):
    t, d = x2d.shape
    tm = TOKEN_TILE
    tiles_per_seq = seq // tm
    assert seq % CHUNK == 0 and len(POOL_WINDOWS) * LANES == pool2d.shape[1]
    pw = pool2d.shape[1]
    return pl.pallas_call(
        functools.partial(_even_out_kernel, tm=tm, tiles_per_seq=tiles_per_seq),
        grid=(t // tm,),
        in_specs=[pl.BlockSpec((tm, sb2d.shape[1]), lambda i: (i, 0)),
                  pl.BlockSpec((tm, pw), lambda i: (i, 0)),
                  pl.BlockSpec((tm, pw), lambda i: (jnp.maximum(i - 1, 0), 0)),
                  pl.BlockSpec(pool_w_bf16.shape, lambda i: (0, 0, 0)),
                  pl.BlockSpec((1, pw), lambda i: (0, 0)),
                  pl.BlockSpec((d, d), lambda i: (0, 0)),
                  pl.BlockSpec((tm, d), lambda i: (i, 0)),
                  pl.BlockSpec((1, 1, d), lambda i: (i // tiles_per_seq, 0, 0))],
        out_specs=pl.BlockSpec((tm, d), lambda i: (i, 0)),
        out_shape=jax.ShapeDtypeStruct((t, d), F32),
        compiler_params=_compiler_params(("arbitrary",)),
        name="even_out",
    )(sb2d, pool2d, pool2d, pool_w_bf16, pool_scale.reshape(1, pw), w_out_bf16, x2d, gate)


FOX_BIAS_TERMS = 3


def _fox_layout_constants(width, qk_bound):
    heads = width // HEAD_DIM
    nt = FOX_BIAS_TERMS
    wide = heads * LANES
    src = lax.broadcasted_iota(I32, (width, wide), 0)
    dst = lax.broadcasted_iota(I32, (width, wide), 1)
    place = ((dst // LANES == src // HEAD_DIM) & (dst % LANES == src % HEAD_DIM)).astype(BF16)
    src = lax.broadcasted_iota(I32, (nt * LANES, wide), 0)
    dst = lax.broadcasted_iota(I32, (nt * LANES, wide), 1)
    term, head = src // LANES, src % LANES
    place_dq = ((dst // LANES == head) & (dst % LANES == HEAD_DIM + term)).astype(BF16)
    place_dk = ((dst // LANES == head) & (dst % LANES == HEAD_DIM + nt + term)).astype(BF16)
    lane = lax.broadcasted_iota(I32, (1, wide), 1) % LANES
    bound_terms = _split_bf16(-qk_bound.astype(F32).reshape(1, 1), nt)
    q_const = jnp.where((lane >= HEAD_DIM + nt) & (lane < HEAD_DIM + 2 * nt), 1.0, 0.0)
    for k, term_k in enumerate(bound_terms):
        q_const = jnp.where(lane == HEAD_DIM + 2 * nt + k, term_k.astype(F32), q_const)
    k_const = jnp.where(((lane >= HEAD_DIM) & (lane < HEAD_DIM + nt))
                        | ((lane >= HEAD_DIM + 2 * nt) & (lane < HEAD_DIM + 3 * nt)), 1.0, 0.0)
    v_const = jnp.where(lane == HEAD_DIM, 1.0, 0.0)
    return place, place_dq, place_dk, q_const.astype(F32), k_const.astype(F32), v_const.astype(F32)


def _odd_prep_kernel(qk_ref, vin_ref, f_ref, rw_ref, rwprev_ref, hs_ref, tri_ref, ctri_ref,
                     place_ref, pdq_ref, pdk_ref, qc_ref, kc_ref, vc_ref,
                     qg_ref, kg_ref, fb_ref, mix_ref, w0_ref, w2_ref, a0_ref, a2_ref, g2_ref,
                     kkp_ref, ka_ref, rk_ref,
                     qa_ref, ka_out_ref, va_ref, drow_ref, rt_ref, kt_ref, at_ref, bt_ref, v_ref,
                     cw_ref, bonus_ref, g_ref, carry_ref, *, tm, tiles_per_seq, width):
    i = pl.program_id(0)
    first = (i % tiles_per_seq) == 0
    hs = hs_ref[...]

    def head_sum(x):
        return _dot_exact_rhs(x, hs, 2)

    inv_hd = 1.0 / HEAD_DIM
    q = qk_ref[:, :width]
    k = qk_ref[:, width:]
    qn = q * lax.rsqrt(head_sum(q * q) * inv_hd + NORM_EPS) * qg_ref[...] * (HEAD_DIM ** -0.5)
    kn = k * lax.rsqrt(head_sum(k * k) * inv_hd + NORM_EPS) * kg_ref[...]

    log_f = _log_sigmoid(f_ref[...] + fb_ref[...])

    @pl.when(first)
    def _():
        carry_ref[...] = jnp.zeros_like(carry_ref)

    dcum = _dot_exact_lhs(tri_ref[...], log_f) + carry_ref[...]
    carry_ref[...] = dcum[tm - 1:tm, :]
    drow_ref[0, 0] = jnp.transpose(dcum)[:SUBLANES, :]

    place = place_ref[...]
    d_terms = jnp.concatenate(_split_bf16(dcum, FOX_BIAS_TERMS), axis=1)
    qa_ref[...] = (jnp.dot(qn.astype(BF16), place, preferred_element_type=F32)
                   + jnp.dot(d_terms, pdq_ref[...], preferred_element_type=F32) + qc_ref[...]).astype(BF16)
    ka_out_ref[...] = (jnp.dot(kn.astype(BF16), place, preferred_element_type=F32)
                       + jnp.dot(-d_terms, pdk_ref[...], preferred_element_type=F32) + kc_ref[...]).astype(BF16)
    va_ref[...] = (jnp.dot(vin_ref[...], place, preferred_element_type=F32) + vc_ref[...]).astype(BF16)

    pd = rw_ref[...]
    prev_row = jnp.where(first, 0.0, rwprev_ref[SUBLANES - 1:SUBLANES, :])
    row0 = lax.broadcasted_iota(I32, (tm, 1), 0) == 0
    prev = jnp.where(row0, prev_row, pltpu.roll(pd, 1, axis=0))
    pd = pd + (prev - pd) * mix_ref[...]
    r = pd[:, :width]
    kr = pd[:, width:2 * width]
    vr = pd[:, 2 * width:3 * width]
    lora_in = pd[:, 3 * width:3 * width + LANES]
    xg = pd[:, 3 * width + LANES:3 * width + 2 * LANES]
    w_log = _log_sigmoid(w0_ref[...] + jnp.dot(jnp.tanh(lora_in).astype(BF16), w2_ref[...],
                                              preferred_element_type=F32)) - 0.5
    log_w = -jnp.exp(w_log)
    a = _sigmoid(a0_ref[...] + jnp.dot(lora_in.astype(BF16), a2_ref[...], preferred_element_type=F32))
    g_ref[...] = jnp.dot(_sigmoid(xg).astype(BF16), g2_ref[...], preferred_element_type=F32)
    kk = kr * kkp_ref[...]
    kk = kk / jnp.maximum(jnp.sqrt(head_sum(kk * kk)), 1e-12)
    k2 = kr * (1.0 + (a - 1.0) * ka_ref[...])
    bonus_ref[...] = head_sum(r * k2 * rk_ref[...]) * vr
    cw = _dot_exact_lhs(ctri_ref[...], log_w)
    grow = jnp.exp(-cw)
    rt_ref[...] = r * jnp.exp(cw)
    kt_ref[...] = k2 * grow
    bt_ref[...] = kk * a * grow
    at_ref[...] = -kk * jnp.exp(cw - log_w)
    v_ref[...] = vr
    cw_ref[...] = cw


def _odd_prep(qk, v, fproj, rw, seq, p, qk_bound):
    t = qk.shape[0]
    width = qk.shape[1] // 2
    tm = TOKEN_TILE
    tiles_per_seq = seq // tm
    nrw = rw.shape[1]
    assert tm % CHUNK == 0 and seq % tm == 0 and nrw == 3 * width + 2 * LANES
    assert HEAD_DIM + 3 * FOX_BIAS_TERMS <= LANES and width // HEAD_DIM <= SUBLANES
    ri = lax.broadcasted_iota(I32, (tm, tm), 0)
    ci = lax.broadcasted_iota(I32, (tm, tm), 1)
    tri = (ci <= ri).astype(BF16)
    ctri = ((ci <= ri) & (ci // CHUNK == ri // CHUNK)).astype(BF16)
    hs = _head_sum_matrix(width)
    row = lambda a: a.reshape(1, -1)
    full = lambda a: pl.BlockSpec(a.shape, lambda i: (0,) * a.ndim)
    tile = lambda n: pl.BlockSpec((tm, n), lambda i: (i, 0))
    consts = [hs, tri, ctri, *_fox_layout_constants(width, qk_bound),
              row(p['qg']), row(p['kg']), row(p['fb']), row(p['mix']), row(p['w0']),
              p['w2'], row(p['a0']), p['a2'], p['g2'], row(p['kk']), row(p['ka']), row(p['rk'])]
    wide = jax.ShapeDtypeStruct((t, width), F32)
    aug = (width // HEAD_DIM) * LANES
    return pl.pallas_call(
        functools.partial(_odd_prep_kernel, tm=tm, tiles_per_seq=tiles_per_seq, width=width),
        grid=(t // tm,),
        in_specs=[tile(2 * width), tile(width), tile(LANES), tile(nrw),
                  pl.BlockSpec((SUBLANES, nrw), lambda i: (jnp.maximum(i * (tm // SUBLANES) - 1, 0), 0))]
                 + [full(c) for c in consts],
        out_specs=[tile(aug), tile(aug), tile(aug),
                   pl.BlockSpec((1, 1, SUBLANES, tm), lambda i: (i // tiles_per_seq, i % tiles_per_seq, 0, 0))]
                  + [tile(width)] * 8,
        out_shape=[jax.ShapeDtypeStruct((t, aug), BF16)] * 3
                  + [jax.ShapeDtypeStruct((t // seq, tiles_per_seq, SUBLANES, tm), F32)] + [wide] * 8,
        scratch_shapes=[pltpu.VMEM((1, LANES), F32)],
        compiler_params=_compiler_params(("arbitrary",)),
        name="odd_prep",
    )(qk, v, fproj, rw, rw, *consts)


def _fox_attention_kernel(q_ref, k_ref, v_ref, drow_ref, o_ref, acc_ref, *, tile):
    hp = pl.program_id(1)
    i = pl.program_id(2)
    row = lax.broadcasted_iota(I32, (tile, tile), 0)
    col = lax.broadcasted_iota(I32, (tile, tile), 1)
    causal = col <= row
    sub = lax.broadcasted_iota(I32, (SUBLANES, tile), 0)
    hs = range(HEADS_PER_LANE_BLOCK)
    heads = [hp * HEADS_PER_LANE_BLOCK + h for h in hs]
    q = [q_ref[0, :, h * LANES:(h + 1) * LANES] for h in hs]

    def d_row(j, h):
        return jnp.sum(jnp.where(sub == heads[h], drow_ref[0, j], 0.0), axis=0, keepdims=True)

    def blocks(js, diag):
        work = [(pl.multiple_of(j * tile, tile), h) for j in js for h in hs]
        s = [lax.dot_general(q[h], k_ref[0, pl.ds(start, tile), h * LANES:(h + 1) * LANES],
                             (((1,), (1,)), ((), ())), preferred_element_type=F32) for start, h in work]
        p = [jnp.exp(x) for x in s]
        if diag:
            p = [jnp.where(causal, x, 0.0) for x in p]
        pv = [jnp.dot(p[n].astype(BF16), v_ref[0, pl.ds(start, tile), h * LANES:(h + 1) * LANES],
                      preferred_element_type=F32) for n, (start, h) in enumerate(work)]
        for h in hs:
            acc_ref[h] += functools.reduce(jnp.add, [pv[n] for n, (_, hh) in enumerate(work) if hh == h])

    acc_ref[...] = jnp.zeros_like(acc_ref)
    blocks([i], True)

    d_top = [jnp.max(d_row(i, h)) for h in hs]

    def alive(j):
        jc = jnp.maximum(j, 0)
        gaps = [d_top[h] - jnp.min(d_row(jc, h)) for h in hs]
        return functools.reduce(jnp.maximum, gaps) > F32_EXP_UNDERFLOW

    def cond(c):
        return jnp.logical_and(c[0] + 1 < i, c[1])

    def body(c):
        j = i - 1 - c[0]
        blocks([j, j - 1], False)
        return c[0] + 2, alive(j - 2)

    done, live = lax.while_loop(cond, body, (jnp.int32(0), alive(i - 1)))

    @pl.when(jnp.logical_and(done < i, live))
    def _():
        blocks([i - 1 - done], False)

    outs = [acc_ref[h][:, :HEAD_DIM] / acc_ref[h][:, HEAD_DIM:HEAD_DIM + 1] for h in hs]
    o_ref[0] = jnp.concatenate(outs, axis=1)


def _fox_qk_bound(q_gain, k_gain):
    bf16_slack = (1.0 + 2.0 ** -8) ** 2
    return (HEAD_DIM * HEAD_DIM ** -0.5 * bf16_slack) * jnp.max(jnp.abs(q_gain)) * jnp.max(jnp.abs(k_gain))


def _fox_attention(q_aug, k_aug, v_aug, drow):
    b, s, w = q_aug.shape
    nblk = w // (HEADS_PER_LANE_BLOCK * LANES)
    tile = ATTN_TILE
    assert drow.shape == (b, s // tile, SUBLANES, tile)
    pair = HEADS_PER_LANE_BLOCK * LANES
    return pl.pallas_call(
        functools.partial(_fox_attention_kernel, tile=tile),
        grid=(b, nblk, s // tile),
        in_specs=[pl.BlockSpec((1, tile, pair), lambda bi, hp, i: (bi, i, hp)),
                  pl.BlockSpec((1, s, pair), lambda bi, hp, i: (bi, 0, hp)),
                  pl.BlockSpec((1, s, pair), lambda bi, hp, i: (bi, 0, hp)),
                  pl.BlockSpec((1, s // tile, SUBLANES, tile), lambda bi, hp, i: (bi, 0, 0, 0))],
        out_specs=pl.BlockSpec((1, tile, LANES), lambda bi, hp, i: (bi, i, hp)),
        out_shape=jax.ShapeDtypeStruct((b, s, nblk * LANES), F32),
        scratch_shapes=[pltpu.VMEM((HEADS_PER_LANE_BLOCK, tile, LANES), F32)],
        compiler_params=_compiler_params(("arbitrary", "arbitrary", "arbitrary")),
        name="fox_attention",
    )(q_aug, k_aug, v_aug, drow)


_NN = (((1,), (0,)), ((), ()))
_NT = (((1,), (1,)), ((), ()))
_TN = (((0,), (0,)), ((), ()))


def _mm(a, b, dims=_NN):
    a_hi, a_lo = _split_bf16(a, 2)
    b_hi, b_lo = _split_bf16(b, 2)
    dot = lambda x, y: lax.dot_general(x, y, dims, preferred_element_type=F32)
    return dot(a_hi, b_hi) + (dot(a_hi, b_lo) + dot(a_lo, b_hi))


def _head_cols(h):
    return slice(h * HEAD_DIM, (h + 1) * HEAD_DIM)


def _rwkv_intra_kernel(rt_ref, kt_ref, at_ref, bt_ref, v_ref, inv_ref, arb_ref, fvu_ref, fvy_ref,
                       *, chunks, heads):
    c = CHUNK
    ri = lax.broadcasted_iota(I32, (2 * c, 2 * c), 0)
    ci = lax.broadcasted_iota(I32, (2 * c, 2 * c), 1)
    keep = (ci % c) < jnp.where(ri < c, ri, ri - c + 1)
    eye = (lax.broadcasted_iota(I32, (c, c), 0) == lax.broadcasted_iota(I32, (c, c), 1)).astype(F32)
    hs = range(heads)

    def chunk(cidx, carry):
        rows = pl.ds(pl.multiple_of(cidx * c, c), c)
        a_all = [jnp.where(keep,
                           _mm(jnp.concatenate([at_ref[rows, _head_cols(h)], rt_ref[rows, _head_cols(h)]], axis=0),
                               jnp.concatenate([bt_ref[rows, _head_cols(h)], kt_ref[rows, _head_cols(h)]], axis=0),
                               _NT), 0.0) for h in hs]
        a_ab = [a[:c, :c] for a in a_all]
        inv = [eye + a for a in a_ab]
        power = [_mm(a, a) for a in a_ab]
        for _ in range(4):
            both = [_mm(jnp.concatenate([inv[h], power[h]], axis=0), power[h]) for h in hs]
            inv = [inv[h] + both[h][:c] for h in hs]
            power = [both[h][c:] for h in hs]
        inv = [inv[h] + _mm(inv[h], power[h]) for h in hs]
        from_v = [_mm(a_all[h][:, c:], v_ref[rows, _head_cols(h)]) for h in hs]
        inv_ref[rows, :] = jnp.concatenate(inv, axis=1)
        arb_ref[rows, :] = jnp.concatenate([a[c:, :c] for a in a_all], axis=1)
        fvu_ref[rows, :] = jnp.concatenate([f[:c] for f in from_v], axis=1)
        fvy_ref[rows, :] = jnp.concatenate([f[c:] for f in from_v], axis=1)
        return carry

    lax.fori_loop(0, chunks, chunk, 0)


def _rwkv_scan_kernel(rt_ref, kt_ref, at_ref, bt_ref, v_ref, cw_ref, inv_ref, arb_ref, fvu_ref, fvy_ref,
                      y_ref, st_ref, *, steps_per_seq, chunks, heads):
    i = pl.program_id(0)

    @pl.when(i % steps_per_seq == 0)
    def _():
        st_ref[...] = jnp.zeros_like(st_ref)

    c = CHUNK
    hs = range(heads)

    def chunk(cidx, carry):
        rows = pl.ds(pl.multiple_of(cidx * c, c), c)
        last = pl.ds(pl.multiple_of(cidx * c, c) + (c - SUBLANES), SUBLANES)
        ar = [jnp.concatenate([at_ref[rows, _head_cols(h)], rt_ref[rows, _head_cols(h)]], axis=0) for h in hs]
        bk = [jnp.concatenate([bt_ref[rows, _head_cols(h)], kt_ref[rows, _head_cols(h)]], axis=0) for h in hs]
        st = [st_ref[h] for h in hs]
        from_state = [_mm(ar[h], st[h], _NT) for h in hs]
        u = [_mm(inv_ref[rows, _head_cols(h)], from_state[h][:c] + fvu_ref[rows, _head_cols(h)]) for h in hs]
        y = [from_state[h][c:] + fvy_ref[rows, _head_cols(h)] + _mm(arb_ref[rows, _head_cols(h)], u[h])
             for h in hs]
        for h in hs:
            decay = jnp.exp(cw_ref[last, _head_cols(h)][SUBLANES - 1:SUBLANES, :])
            uv = jnp.concatenate([u[h], v_ref[rows, _head_cols(h)]], axis=0)
            st_ref[h] = (st[h] + _mm(uv, bk[h], _TN)) * decay
        y_ref[rows, :] = jnp.concatenate(y, axis=1)
        return carry

    lax.fori_loop(0, chunks, chunk, 0)


def _rwkv_scan(rt, kt, at, bt, v, cw, seq):
    t, w = rt.shape
    heads = w // HEAD_DIM
    ts = 2 * TOKEN_TILE
    assert seq % ts == 0
    spec = pl.BlockSpec((ts, w), lambda i: (i, 0))
    wide = jax.ShapeDtypeStruct((t, w), F32)
    inv, arb, fvu, fvy = pl.pallas_call(
        functools.partial(_rwkv_intra_kernel, chunks=ts // CHUNK, heads=heads),
        grid=(t // ts,),
        in_specs=[spec] * 5,
        out_specs=[spec] * 4,
        out_shape=[wide] * 4,
        compiler_params=_compiler_params(("arbitrary",)),
        name="rwkv_intra",
    )(rt, kt, at, bt, v)
    return pl.pallas_call(
        functools.partial(_rwkv_scan_kernel, steps_per_seq=seq // ts, chunks=ts // CHUNK, heads=heads),
        grid=(t // ts,),
        in_specs=[spec] * 10,
        out_specs=spec,
        out_shape=wide,
        scratch_shapes=[pltpu.VMEM((heads, HEAD_DIM, HEAD_DIM), F32)],
        compiler_params=_compiler_params(("arbitrary",)),
        name="rwkv_scan",
    )(rt, kt, at, bt, v, cw, inv, arb, fvu, fvy)


def _odd_out_kernel(att_ref, og_ref, y_ref, bonus_ref, g_ref, hs_ref, lnw_ref, lnb_ref, wo_ref,
                    x_ref, gate_ref, o_ref, *, width):
    hs = hs_ref[...]
    inv_hd = 1.0 / HEAD_DIM
    fox = att_ref[...] * _sigmoid(og_ref[...])
    y = y_ref[...]
    mu = _dot_exact_rhs(y, hs, 2) * inv_hd
    yc = y - mu
    var = _dot_exact_rhs(yc * yc, hs, 2) * inv_hd
    yn = yc * lax.rsqrt(var + RWKV_GN_EPS) * lnw_ref[...] + lnb_ref[...]
    rw = (yn + bonus_ref[...]) * g_ref[...]
    acc = jnp.dot(fox.astype(BF16), wo_ref[:width, :], preferred_element_type=F32)
    acc += jnp.dot(rw.astype(BF16), wo_ref[width:, :], preferred_element_type=F32)
    o_ref[...] = x_ref[...] + gate_ref[0] * acc


def _odd_out(att, og, y, bonus, g, ln_w, ln_b, w_out_bf16, x2d, seq, gate):
    t, d = x2d.shape
    width = att.shape[1]
    tm = TOKEN_TILE
    tiles_per_seq = seq // tm
    hs = _head_sum_matrix(width)
    tile = lambda n: pl.BlockSpec((tm, n), lambda i: (i, 0))
    return pl.pallas_call(
        functools.partial(_odd_out_kernel, width=width),
        grid=(t // tm,),
        in_specs=[tile(width)] * 5
                 + [pl.BlockSpec(hs.shape, lambda i: (0, 0)),
                    pl.BlockSpec((1, width), lambda i: (0, 0)), pl.BlockSpec((1, width), lambda i: (0, 0)),
                    pl.BlockSpec((d, d), lambda i: (0, 0)), tile(d),
                    pl.BlockSpec((1, 1, d), lambda i: (i // tiles_per_seq, 0, 0))],
        out_specs=tile(d),
        out_shape=jax.ShapeDtypeStruct((t, d), F32),
        compiler_params=_compiler_params(("arbitrary",)),
        name="odd_out",
    )(att, og, y, bonus, g, hs, ln_w.reshape(1, width), ln_b.reshape(1, width), w_out_bf16, x2d, gate)


def _top_rows(s, k, prio, payload=None):
    big = jnp.int32(2 ** 30)
    vals, picks = [], []
    for _ in range(k):
        m = jnp.max(s, axis=0, keepdims=True)
        am = jnp.min(jnp.where(s == m, prio, big), axis=0, keepdims=True)
        hit = prio == am
        vals.append(m)
        if payload is None:
            picks.append(am)
        else:
            picks.append(jnp.max(jnp.where(hit, payload, -1), axis=0, keepdims=True))
        s = jnp.where(hit, -jnp.inf, s)
    return jnp.concatenate(vals, axis=0), jnp.concatenate(picks, axis=0)


def _staircase_groups():
    k, g = PEER_TOPK, SUBLANES
    groups = []
    for b in range(k):
        count = k // (b + 1)
        if count <= 1:
            break
        for a0 in range(0, count, g):
            groups.append((a0, b, False, min(g, count - a0)))
    first_single = next(b for b in range(k) if k // (b + 1) <= 1)
    for b0 in range(first_single, k, g):
        groups.append((0, b0, True, min(g, k - b0)))
    return groups


def _topk_constants(tt):
    sub = lax.broadcasted_iota(I32, (SUBLANES, 1), 0)
    groups = _staircase_groups()
    prio = jnp.concatenate(
        [jnp.broadcast_to((sub * 0 + a0) * PEER_TOPK + b0 + sub if along_b
                          else (a0 + sub) * PEER_TOPK + b0, (SUBLANES, tt))
         for a0, b0, along_b, _ in groups], axis=0)
    return lax.broadcasted_iota(I32, (PEER_NKEYS, tt), 0), sub, groups, prio


def _topk_head(q_head, k1, k2, consts):
    key_id, sub, groups, prio = consts
    half = PEER_NKEYS
    nt = (((1,), (1,)), ((), ()))
    s1 = lax.dot_general(k1, q_head[:, :half], nt, precision=HIGHEST, preferred_element_type=F32)
    s2 = lax.dot_general(k2, q_head[:, half:], nt, precision=HIGHEST, preferred_element_type=F32)
    v1, i1 = _top_rows(s1, PEER_TOPK, key_id)
    v2, i2 = _top_rows(s2, PEER_TOPK, key_id)
    cand, cand_idx = [], []
    for a0, b0, along_b, valid in groups:
        if along_b:
            val = v1[a0:a0 + 1] + v2[b0:b0 + SUBLANES]
            eid = i1[a0:a0 + 1] * PEER_NKEYS + i2[b0:b0 + SUBLANES]
        else:
            val = v1[a0:a0 + SUBLANES] + v2[b0:b0 + 1]
            eid = i1[a0:a0 + SUBLANES] * PEER_NKEYS + i2[b0:b0 + 1]
        cand.append(jnp.where(sub < valid, val, -jnp.inf))
        cand_idx.append(eid)
    top_s, idx = _top_rows(jnp.concatenate(cand, axis=0), PEER_TOPK, prio,
                           payload=jnp.concatenate(cand_idx, axis=0))
    e = jnp.exp(top_s - top_s[0:1])
    return idx, e / jnp.sum(e, axis=0, keepdims=True)


def _pack_table_kernel(t_ref, o_ref, *, pairs):
    high = jnp.int32(-65536)
    for j in range(pairs):
        lo = t_ref[:, (2 * j) * LANES:(2 * j + 1) * LANES].astype(BF16).astype(F32)
        hi = t_ref[:, (2 * j + 1) * LANES:(2 * j + 2) * LANES].astype(BF16).astype(F32)
        lo_bits = lax.shift_right_logical(pltpu.bitcast(lo, I32), 16)
        words = lax.bitwise_or(lo_bits, lax.bitwise_and(pltpu.bitcast(hi, I32), high))
        o_ref[pl.ds(j, t_ref.shape[0], stride=pairs), :] = words


def _pack_table(table):
    e, d = table.shape
    pairs = d // (2 * LANES)
    rows = 2 * TOKEN_TILE
    return pl.pallas_call(
        functools.partial(_pack_table_kernel, pairs=pairs),
        grid=(e // rows,),
        in_specs=[pl.BlockSpec((rows, d), lambda i: (i, 0))],
        out_specs=pl.BlockSpec((rows * pairs, LANES), lambda i: (i, 0)),
        out_shape=jax.ShapeDtypeStruct((e * pairs, LANES), I32),
        compiler_params=_compiler_params(("arbitrary",)),
        name="pack_table",
    )(table)


def _unpack_words(w):
    lo = pltpu.bitcast(lax.shift_left(w, 16), F32)
    hi = pltpu.bitcast(lax.bitwise_and(w, jnp.int32(-65536)), F32)
    return lo, hi


def _gather_rows(idx_ref, tab_ref, tile_refs, toks, pairs):
    tok_idx = [idx_ref.at[tok] for tok in toks]
    for m in range(PEER_SLOTS):
        for rows, tile_ref in zip(tok_idx, tile_refs):
            row = pl.multiple_of(rows[m], pairs)
            tile_ref[m * pairs:(m + 1) * pairs, :] = tab_ref[pl.ds(row, pairs), :]


def _tile_chunk(tile_ref, j, pairs):
    return _unpack_words(tile_ref[pl.ds(j, PEER_SLOTS, stride=pairs), :])


PEER_TILE_BUFFERS = 8


def _pipelined_tokens(ntok, gather, compute, init, trips_per_iter=1, per_iter=None):
    nb = PEER_TILE_BUFFERS
    group = nb // 2
    assert ntok % (nb * trips_per_iter) == 0
    gather(list(range(group)), list(range(group)))

    def trip(p, carry):
        tok = nb * p
        for half in range(2):
            first = tok + half * group
            ahead = [jnp.minimum(first + group + k, ntok - 1) for k in range(group)]
            gather(ahead, [((half + 1) % 2) * group + k for k in range(group)])
            for k in range(group):
                carry = compute(first + k, half * group + k, carry)
        return carry

    def iteration(i, carry):
        if per_iter is not None:
            per_iter(i)
        for r in range(trips_per_iter):
            carry = trip(i * trips_per_iter + r, carry)
        return carry

    return lax.fori_loop(0, ntok // (nb * trips_per_iter), iteration, init)


def _pick_row(block, r):
    sub = lax.broadcasted_iota(I32, block.shape, 0)
    return jnp.sum(jnp.where(sub == r, block, 0.0), axis=0, keepdims=True)


def _peer_select_act_kernel(q_ref, k1_ref, k2_ref, tab_ref, h_ref, idx_ref, coef_ref,
                            idx_smem, stage_ref, slots_ref, gate_ref, act_ref, sem, *tiles, pairs, nblk):
    n = pl.program_id(0)
    ntok = h_ref.shape[0]
    copy_idx = pltpu.make_async_copy(stage_ref, idx_smem, sem)

    @pl.when(n == 0)
    def _():
        stage_ref[...] = jnp.zeros_like(stage_ref)
        gate_ref[...] = jnp.zeros_like(gate_ref)
        copy_idx.start()

    copy_idx.wait()
    act_ref[...] = jnp.zeros_like(act_ref)
    lane = lax.broadcasted_iota(I32, (PEER_SLOTS, LANES), 1)
    consts = _topk_constants(LANES)
    slot_now = n % 2

    def gather(toks, bufs):
        _gather_rows(idx_smem, tab_ref, [tiles[b] for b in bufs], toks, pairs)

    def compute(tok, buf, carry):
        base = pl.multiple_of((tok // SUBLANES) * SUBLANES, SUBLANES)
        h_row = _pick_row(h_ref[pl.ds(base, SUBLANES), :], tok % SUBLANES)
        terms = []
        for j in range(pairs):
            lo, hi = _tile_chunk(tiles[buf], j, pairs)
            h_lo = h_row[:, (2 * j) * LANES:(2 * j + 1) * LANES]
            h_hi = h_row[:, (2 * j + 1) * LANES:(2 * j + 2) * LANES]
            terms.append(lo * h_lo + hi * h_hi)
        while len(terms) > 1:
            terms = [a + b for a, b in zip(terms[::2], terms[1::2])]
        col = jnp.sum(terms[0], axis=1, keepdims=True)
        part = tok // LANES
        act_ref[part] = jnp.where(lane == tok % LANES, col, act_ref[part])
        return carry

    def select_head(hd):
        rows = pl.ds(pl.multiple_of(hd * PEER_TOPK, PEER_TOPK), PEER_TOPK)
        for part in range(ntok // LANES):
            cols = slice(part * LANES, (part + 1) * LANES)
            idx, gate = _topk_head(q_ref[hd, cols, :], k1_ref[...], k2_ref[...], consts)
            slots_ref[rows, cols] = idx
            gate_ref[slot_now, rows, cols] = gate

    heads = q_ref.shape[0]
    _pipelined_tokens(ntok, gather, compute, 0,
                      trips_per_iter=ntok // (PEER_TILE_BUFFERS * heads), per_iter=select_head)

    offsets = jnp.transpose(slots_ref[...]) * pairs
    idx_ref[...] = offsets
    stage_ref[...] = offsets

    @pl.when(n < nblk)
    def _():
        copy_idx.start()

    act_t = jnp.concatenate([act_ref[p] for p in range(ntok // LANES)], axis=1)
    gelu = 0.5 * act_t * (1.0 + lax.erf(act_t * (2.0 ** -0.5)))
    coef_ref[...] = gate_ref[1 - slot_now] * gelu


def _peer_select_act(q_heads, k1, k2, table_words, h2d):
    heads, t, _ = q_heads.shape
    d = h2d.shape[1]
    pairs = d // (2 * LANES)
    tb = TOKEN_TILE
    nblk = t // tb
    assert heads == PEER_HEADS and tb % (PEER_TILE_BUFFERS * heads) == 0 and tb % LANES == 0
    cur = lambda n: jnp.minimum(n, nblk - 1)
    prev = lambda n: jnp.maximum(n - 1, 0)
    return pl.pallas_call(
        functools.partial(_peer_select_act_kernel, pairs=pairs, nblk=nblk),
        grid=(nblk + 1,),
        in_specs=[pl.BlockSpec((heads, tb, q_heads.shape[2]), lambda n: (0, cur(n), 0)),
                  pl.BlockSpec(k1.shape, lambda n: (0, 0)),
                  pl.BlockSpec(k2.shape, lambda n: (0, 0)),
                  pl.BlockSpec(table_words.shape, lambda n: (0, 0), pipeline_mode=pl.Buffered(1)),
                  pl.BlockSpec((tb, d), lambda n: (prev(n), 0))],
        out_specs=[pl.BlockSpec((tb, PEER_SLOTS), lambda n: (cur(n), 0)),
                   pl.BlockSpec((PEER_SLOTS, tb), lambda n: (0, prev(n)))],
        out_shape=[jax.ShapeDtypeStruct((t, PEER_SLOTS), I32),
                   jax.ShapeDtypeStruct((PEER_SLOTS, t), F32)],
        scratch_shapes=[pltpu.SMEM((tb, PEER_SLOTS), I32),
                        pltpu.VMEM((tb, PEER_SLOTS), I32),
                        pltpu.VMEM((PEER_SLOTS, tb), I32),
                        pltpu.VMEM((2, PEER_SLOTS, tb), F32),
                        pltpu.VMEM((tb // LANES, PEER_SLOTS, LANES), F32),
                        pltpu.SemaphoreType.DMA]
                       + [pltpu.VMEM((pairs * PEER_SLOTS, LANES), I32)] * PEER_TILE_BUFFERS,
        compiler_params=_compiler_params(("arbitrary",), vmem_mb=56),
        name="peer_select_act",
    )(q_heads, k1, k2, table_words, h2d)


def _peer_out_kernel(idx_ref, tab_ref, coef_ref, x_ref, g_ref, o_ref, *tiles, pairs):
    ntok, d = x_ref.shape
    lane = lax.broadcasted_iota(I32, (PEER_SLOTS, ntok), 1)
    sub = lax.broadcasted_iota(I32, (SUBLANES, d), 0)
    groups = PEER_SLOTS // SUBLANES

    def gather(toks, bufs):
        _gather_rows(idx_ref, tab_ref, [tiles[b] for b in bufs], toks, pairs)

    def compute(tok, buf, rows):
        coef = jnp.sum(jnp.where(lane == tok, coef_ref[...], 0.0), axis=1, keepdims=True)
        pieces = []
        for j in range(pairs):
            for vals in _tile_chunk(tiles[buf], j, pairs):
                prod = (vals * coef).reshape(groups, SUBLANES, LANES)
                pieces.append(jnp.sum(jnp.sum(prod, axis=0), axis=0, keepdims=True))
        row = jnp.concatenate(pieces, axis=1)
        rows = jnp.where(sub == tok % SUBLANES, row, rows)
        base = pl.multiple_of((tok // SUBLANES) * SUBLANES, SUBLANES)
        o_ref[pl.ds(base, SUBLANES), :] = x_ref[pl.ds(base, SUBLANES), :] + g_ref[0] * rows
        return rows

    _pipelined_tokens(ntok, gather, compute, jnp.zeros((SUBLANES, d), F32))


def _peer_out(idx, table_words, coef_t, x2d, seq, gate):
    t, d = x2d.shape
    pairs = d // (2 * LANES)
    nt = PEER_TOKENS
    per_seq = seq // nt
    return pl.pallas_call(
        functools.partial(_peer_out_kernel, pairs=pairs),
        grid=(t // nt,),
        in_specs=[pl.BlockSpec((nt, PEER_SLOTS), lambda i: (i, 0), memory_space=pltpu.SMEM),
                  pl.BlockSpec(table_words.shape, lambda i: (0, 0), pipeline_mode=pl.Buffered(1)),
                  pl.BlockSpec((PEER_SLOTS, nt), lambda i: (0, i)),
                  pl.BlockSpec((nt, d), lambda i: (i, 0)),
                  pl.BlockSpec((1, 1, d), lambda i: (i // per_seq, 0, 0))],
        out_specs=pl.BlockSpec((nt, d), lambda i: (i, 0)),
        out_shape=jax.ShapeDtypeStruct((t, d), F32),
        scratch_shapes=[pltpu.VMEM((pairs * PEER_SLOTS, LANES), I32)] * PEER_TILE_BUFFERS,
        compiler_params=_compiler_params(("arbitrary",), vmem_mb=56),
        name="peer_out",
    )(idx, table_words, coef_t, x2d, gate)


def _peer_ffn(x2d, seq, gain, shift, scale, gate, wq_bf16, k1, k2, u_words, v_words):
    n = wq_bf16.shape[1]
    q_heads, h = _norm_proj(x2d, seq, gain, shift, scale, wq_bf16, [(0, n)], [F32], emit_h=True,
                            head_major=PEER_HEADS)
    idx, coef_t = _peer_select_act(q_heads, k1, k2, u_words, h)
    return _peer_out(idx, v_words, coef_t, x2d, seq, gate)


def _final_norm_kernel(x_ref, gain_ref, o_ref):
    x = x_ref[...]
    ms = jnp.mean(x * x, axis=-1, keepdims=True)
    o_ref[...] = x * lax.rsqrt(ms + NORM_EPS) * gain_ref[...]


def _final_norm(x2d, gain):
    t, d = x2d.shape
    tm = TOKEN_TILE
    return pl.pallas_call(
        _final_norm_kernel,
        grid=(t // tm,),
        in_specs=[pl.BlockSpec((tm, d), lambda i: (i, 0)), pl.BlockSpec((1, d), lambda i: (0, 0))],
        out_specs=pl.BlockSpec((tm, d), lambda i: (i, 0)),
        out_shape=jax.ShapeDtypeStruct((t, d), F32),
        compiler_params=_compiler_params(("arbitrary",)),
        name="final_norm",
    )(x2d, gain.reshape(1, d))


def _even_mixer(x2d, b, seq, gain, shift, scale, gate, w_in, w_out, pool_w, pool_scale):
    d = x2d.shape[1]
    pool_width = len(POOL_WINDOWS) * LANES
    sb_width = (w_in.shape[1] - pool_width) // 3
    qkv, pool = _norm_proj(x2d, seq, gain, shift, scale, w_in.astype(BF16),
                           [(0, 3 * sb_width), (3 * sb_width, pool_width)], [BF16, F32])
    sb = _sb_attention(qkv.reshape(b, seq, 3 * sb_width), sb_width // HEAD_DIM)
    return _even_out(sb.reshape(b * seq, sb_width), pool, seq, pool_w.astype(BF16), pool_scale,
                     w_out.astype(BF16), x2d, gate)


def _odd_mixer(x2d, b, seq, gain, shift, scale, gate, w_in, w_out, fox_qnorm, fox_knorm, fox_fbias,
               rwkv_mix, rwkv_w0, rwkv_w2, rwkv_a0, rwkv_a2, rwkv_g2, rwkv_kk, rwkv_ka, rwkv_rk,
               rwkv_ln_w, rwkv_ln_b):
    d = x2d.shape[1]
    width = rwkv_w0.shape[0]
    heads = width // HEAD_DIM
    fox_in = 4 * width + heads
    n_rw = w_in.shape[1] - fox_in
    w_cols = jnp.concatenate([w_in[:, :4 * width], w_in[:, fox_in:], w_in[:, 4 * width:fox_in],
                              jnp.zeros((d, LANES - heads), w_in.dtype)], axis=1).astype(BF16)
    qk, v, og, rw, fproj = _norm_proj(
        x2d, seq, gain, shift, scale, w_cols,
        [(0, 2 * width), (2 * width, width), (3 * width, width), (4 * width, n_rw), (4 * width + n_rw, LANES)],
        [F32, BF16, F32, F32, F32])
    lora = rwkv_w2.shape[0]
    assert 2 * lora == LANES and rwkv_a2.shape[0] == lora and rwkv_g2.shape[0] == LANES
    zeros = jnp.zeros((lora, width), F32)
    params = dict(
        qg=jnp.tile(fox_qnorm, heads), kg=jnp.tile(fox_knorm, heads),
        fb=jnp.concatenate([fox_fbias, jnp.zeros((LANES - heads,), F32)]),
        mix=rwkv_mix, w0=rwkv_w0, a0=rwkv_a0,
        w2=jnp.concatenate([rwkv_w2, zeros], axis=0).astype(BF16),
        a2=jnp.concatenate([zeros, rwkv_a2], axis=0).astype(BF16),
        g2=rwkv_g2.astype(BF16), kk=rwkv_kk, ka=rwkv_ka, rk=rwkv_rk.reshape(-1))
    q_aug, k_aug, v_aug, drow, rt, kt, at, bt, vv, cw, bonus, g = _odd_prep(
        qk, v, fproj, rw, seq, params, _fox_qk_bound(fox_qnorm, fox_knorm))
    r3 = lambda a: a.reshape(b, seq, -1)
    att = _fox_attention(r3(q_aug), r3(k_aug), r3(v_aug), drow)
    y = _rwkv_scan(rt, kt, at, bt, vv, cw, seq)
    return _odd_out(att.reshape(b * seq, width), og, y, bonus, g, rwkv_ln_w, rwkv_ln_b,
                    w_out.astype(BF16), x2d, seq, gate)


def kernel(x, c, ada_w, ada_b, norm_mix, norm_ffn, ev_w_in, ev_w_out, pool_w, pool_scale, od_w_in, od_w_out, fox_qnorm, fox_knorm, fox_fbias, rwkv_mix, rwkv_w0, rwkv_w2, rwkv_a0, rwkv_a2, rwkv_g2, rwkv_kk, rwkv_ka, rwkv_rk, rwkv_ln_w, rwkv_ln_b, peer_wq, peer_k1, peer_k2, peer_u, peer_v, final_norm):
    b, seq, d = x.shape
    depth = ada_w.shape[0]
    assert TOKEN_TILE == ATTN_TILE and seq % (2 * TOKEN_TILE) == 0
    mod = _adaln(c, ada_w, ada_b).reshape(depth, b, 6, 1, d)
    x2d = x.reshape(b * seq, d)
    for layer in range(depth):
        sh1, sc1, g1, sh2, sc2, g2 = (mod[layer, :, i] for i in range(6))
        j = layer // 2
        if layer % 2 == 0:
            x2d = _even_mixer(x2d, b, seq, norm_mix[layer], sh1, sc1, g1, ev_w_in[j], ev_w_out[j],
                              pool_w[j], pool_scale[j])
        else:
            x2d = _odd_mixer(x2d, b, seq, norm_mix[layer], sh1, sc1, g1, od_w_in[j], od_w_out[j],
                             fox_qnorm[j], fox_knorm[j], fox_fbias[j], rwkv_mix[j], rwkv_w0[j],
                             rwkv_w2[j], rwkv_a0[j], rwkv_a2[j], rwkv_g2[j], rwkv_kk[j], rwkv_ka[j],
                             rwkv_rk[j], rwkv_ln_w[j], rwkv_ln_b[j])
        x2d = _peer_ffn(x2d, seq, norm_ffn[layer], sh2, sc2, g2, peer_wq[layer].astype(BF16),
                        peer_k1[layer], peer_k2[layer], _pack_table(peer_u[layer]),
                        _pack_table(peer_v[layer]))
    return _final_norm(x2d, final_norm).reshape(b, seq, d)
```

```python
import functools

import jax
import jax.numpy as jnp
from jax import lax
from jax.experimental import pallas as pl
from jax.experimental.pallas import tpu as pltpu

F32 = jnp.float32
BF16 = jnp.bfloat16
I32 = jnp.int32
HIGHEST = lax.Precision.HIGHEST

LANES = 128
SUBLANES = 8
VMEM_BYTES_V7X = 64 * 1024 * 1024

HEAD_DIM = 64
HEADS_PER_LANE_BLOCK = LANES // HEAD_DIM
NORM_EPS = 1e-6
RWKV_GN_EPS = 64e-5
CHUNK = 64
POOL_WINDOWS = (2, 4, 8, 16)
PEER_TOPK = 16
PEER_NKEYS = 128
PEER_HEADS = 8
PEER_SLOTS = PEER_HEADS * PEER_TOPK

TOKEN_TILE = 256
PROJ_TILE = 512
ATTN_TILE = 256
PEER_TOKENS = 128

F32_EXP_UNDERFLOW = -104.0


def _compiler_params(semantics, vmem_mb=48):
    return pltpu.CompilerParams(dimension_semantics=semantics,
                                vmem_limit_bytes=min(vmem_mb * 1024 * 1024, VMEM_BYTES_V7X))


def _split_bf16(x, n):
    parts = []
    r = x
    for _ in range(n):
        p = r.astype(BF16)
        parts.append(p)
        r = r - p.astype(F32)
    return parts


def _dot_exact_rhs(x, m, n=3):
    acc = None
    for p in _split_bf16(x, n):
        t = jnp.dot(p, m, preferred_element_type=F32)
        acc = t if acc is None else acc + t
    return acc


def _dot_exact_lhs(m, x, n=3):
    acc = None
    for p in _split_bf16(x, n):
        t = jnp.dot(m, p, preferred_element_type=F32)
        acc = t if acc is None else acc + t
    return acc


def _log_sigmoid(z):
    return jnp.minimum(z, 0.0) - jnp.log1p(jnp.exp(-jnp.abs(z)))


def _sigmoid(z):
    return 1.0 / (1.0 + jnp.exp(-z))


def _head_sum_matrix(width):
    r = lax.broadcasted_iota(I32, (width, width), 0) // HEAD_DIM
    c = lax.broadcasted_iota(I32, (width, width), 1) // HEAD_DIM
    return (r == c).astype(BF16)


def _adaln_kernel(c_ref, w_ref, b_ref, o_ref):
    c = c_ref[...]
    c_act = c * _sigmoid(c)
    o_ref[0] = jnp.dot(c_act, w_ref[0], precision=HIGHEST, preferred_element_type=F32) + b_ref[0]


def _adaln(c, ada_w, ada_b):
    b, d = c.shape
    depth, _, n = ada_w.shape
    tn = 1536
    assert n % tn == 0 and b <= SUBLANES
    c_pad = jnp.zeros((SUBLANES, d), F32).at[:b].set(c)
    out = pl.pallas_call(
        _adaln_kernel,
        grid=(depth, n // tn),
        in_specs=[pl.BlockSpec((SUBLANES, d), lambda l, j: (0, 0)),
                  pl.BlockSpec((1, d, tn), lambda l, j: (l, 0, j)),
                  pl.BlockSpec((1, 1, tn), lambda l, j: (l, 0, j))],
        out_specs=pl.BlockSpec((1, SUBLANES, tn), lambda l, j: (l, 0, j)),
        out_shape=jax.ShapeDtypeStruct((depth, SUBLANES, n), F32),
        compiler_params=_compiler_params(("arbitrary", "arbitrary")),
        name="adaln",
    )(c_pad, ada_w, ada_b.reshape(depth, 1, n))
    return out[:, :b, :]


def _norm_proj_kernel(x_ref, gain_ref, shift_ref, scale_ref, w_ref, *out_refs, groups, emit_h):
    x = x_ref[...]
    ms = jnp.mean(x * x, axis=-1, keepdims=True)
    y = x * lax.rsqrt(ms + NORM_EPS) * gain_ref[...]
    h = y * (1.0 + scale_ref[0]) + shift_ref[0]
    hb = h.astype(BF16)
    for (start, width), o_ref in zip(groups, out_refs):
        res = jnp.dot(hb, w_ref[:, start:start + width], preferred_element_type=F32).astype(o_ref.dtype)
        if len(o_ref.shape) == 3:
            per_head = o_ref.shape[2]
            for hd in range(o_ref.shape[0]):
                o_ref[hd] = res[:, hd * per_head:(hd + 1) * per_head]
        else:
            o_ref[...] = res
    if emit_h:
        out_refs[-1][...] = h


def _norm_proj(x2d, seq, gain, shift, scale, w_bf16, groups, dtypes, emit_h=False, head_major=0):
    t, d = x2d.shape
    n = w_bf16.shape[1]
    tm = PROJ_TILE
    assert t % tm == 0 and seq % tm == 0
    tiles_per_seq = seq // tm
    out_shape = [jax.ShapeDtypeStruct((t, wd), dt) for (_, wd), dt in zip(groups, dtypes)]
    out_specs = [pl.BlockSpec((tm, wd), lambda i: (i, 0)) for (_, wd) in groups]
    if head_major:
        (_, wd), = groups
        out_shape = [jax.ShapeDtypeStruct((head_major, t, wd // head_major), dtypes[0])]
        out_specs = [pl.BlockSpec((head_major, tm, wd // head_major), lambda i: (0, i, 0))]
    if emit_h:
        out_shape.append(jax.ShapeDtypeStruct((t, d), F32))
        out_specs.append(pl.BlockSpec((tm, d), lambda i: (i, 0)))
    return pl.pallas_call(
        functools.partial(_norm_proj_kernel, groups=tuple(groups), emit_h=emit_h),
        grid=(t // tm,),
        in_specs=[pl.BlockSpec((tm, d), lambda i: (i, 0)),
                  pl.BlockSpec((1, d), lambda i: (0, 0)),
                  pl.BlockSpec((1, 1, d), lambda i: (i // tiles_per_seq, 0, 0)),
                  pl.BlockSpec((1, 1, d), lambda i: (i // tiles_per_seq, 0, 0)),
                  pl.BlockSpec((d, n), lambda i: (0, 0))],
        out_specs=out_specs,
        out_shape=out_shape,
        compiler_params=_compiler_params(("arbitrary",)),
        name="norm_proj",
    )(x2d, gain.reshape(1, d), shift, scale, w_bf16)


def _sb_attention_kernel(q_ref, k_ref, v_ref, o_ref, acc_ref, carry_ref, *, tile, scale):
    i = pl.program_id(2)
    lane = lax.broadcasted_iota(I32, (1, LANES), 1)
    row = lax.broadcasted_iota(I32, (tile, tile), 0)
    col = lax.broadcasted_iota(I32, (tile, tile), 1)
    strict = col < row
    later = (row > col).astype(BF16)
    q = q_ref[0]
    hs = range(HEADS_PER_LANE_BLOCK)
    qh = [jnp.where((lane >= HEAD_DIM * h) & (lane < HEAD_DIM * (h + 1)), q, jnp.zeros_like(q)) for h in hs]

    def block(j, diag):
        start = pl.multiple_of(j * tile, tile)
        kb = k_ref[0, pl.ds(start, tile), :]
        vb = v_ref[0, pl.ds(start, tile), :]
        z = [lax.dot_general(qh[h], kb, (((1,), (1,)), ((), ())), preferred_element_type=F32) * scale
             for h in hs]
        log_beta = [_log_sigmoid(x) for x in z]
        log_1m = [log_beta[h] - z[h] for h in hs]
        if diag:
            log_1m = [jnp.where(strict, x, 0.0) for x in log_1m]
        stick = [_dot_exact_rhs(log_1m[h], later, 2) + carry_ref[h] for h in hs]
        w = [jnp.exp(log_beta[h] + stick[h]) for h in hs]
        if diag:
            w = [jnp.where(strict, x, 0.0) for x in w]
        for h in hs:
            acc_ref[h] += jnp.dot(w[h].astype(BF16), vb, preferred_element_type=F32)
            carry_ref[h] += jnp.sum(log_1m[h], axis=1, keepdims=True)

    acc_ref[...] = jnp.zeros_like(acc_ref)
    carry_ref[...] = jnp.zeros_like(carry_ref)
    block(i, True)

    def alive():
        return jnp.max(carry_ref[...]) > F32_EXP_UNDERFLOW

    def cond(c):
        return jnp.logical_and(c[0] < i, c[1])

    def body(c):
        block(i - 1 - c[0], False)
        return c[0] + 1, alive()

    lax.while_loop(cond, body, (jnp.int32(0), alive()))
    o_ref[0] = jnp.where(lane < HEAD_DIM, acc_ref[0], acc_ref[1]).astype(o_ref.dtype)


def _sb_attention(qkv, n_heads):
    b, s, _ = qkv.shape
    nblk = n_heads // HEADS_PER_LANE_BLOCK
    tile = ATTN_TILE
    assert s % tile == 0
    return pl.pallas_call(
        functools.partial(_sb_attention_kernel, tile=tile, scale=HEAD_DIM ** -0.5),
        grid=(b, nblk, s // tile),
        in_specs=[pl.BlockSpec((1, tile, LANES), lambda bi, hp, i: (bi, i, hp)),
                  pl.BlockSpec((1, s, LANES), lambda bi, hp, i: (bi, 0, nblk + hp)),
                  pl.BlockSpec((1, s, LANES), lambda bi, hp, i: (bi, 0, 2 * nblk + hp))],
        out_specs=pl.BlockSpec((1, tile, LANES), lambda bi, hp, i: (bi, i, hp)),
        out_shape=jax.ShapeDtypeStruct((b, s, n_heads * HEAD_DIM), BF16),
        scratch_shapes=[pltpu.VMEM((HEADS_PER_LANE_BLOCK, tile, LANES), F32),
                        pltpu.VMEM((HEADS_PER_LANE_BLOCK, tile, 1), F32)],
        compiler_params=_compiler_params(("arbitrary", "arbitrary", "arbitrary")),
        name="sb_attention",
    )(qkv, qkv, qkv)


def _even_out_kernel(sb_ref, pool_ref, prev_ref, pw_ref, ps_ref, wo_ref, x_ref, g_ref, o_ref,
                     *, tm, tiles_per_seq):
    i = pl.program_id(0)
    t0 = (i % tiles_per_seq) * tm
    t = t0 + lax.broadcasted_iota(I32, (tm, 1), 0)
    s = t0 - tm + lax.broadcasted_iota(I32, (1, 2 * tm), 1)
    chunk_end = ((t // CHUNK) + 1) * CHUNK
    group = LANES
    acc = jnp.dot(sb_ref[...], wo_ref[0:4 * group, :], preferred_element_type=F32)
    for g, w in enumerate(POOL_WINDOWS):
        lo = jnp.maximum(t - w // 2, 0)
        hi = jnp.minimum(t + (w - w // 2), chunk_end)
        band = ((s >= lo) & (s < hi)).astype(BF16)
        cur = pool_ref[:, g * group:(g + 1) * group]
        ext = jnp.concatenate([prev_ref[:, g * group:(g + 1) * group], cur], axis=0)
        window_sum = _dot_exact_lhs(band, ext)
        pooled = window_sum / (hi - lo).astype(F32) - cur
        y = jnp.dot(pooled.astype(BF16), pw_ref[g], preferred_element_type=F32)
        y = y * ps_ref[:, g * group:(g + 1) * group]
        acc += jnp.dot(y.astype(BF16), wo_ref[(4 + g) * group:(5 + g) * group, :],
                       preferred_element_type=F32)
    o_ref[...] = x_ref[...] + g_ref[0] * acc


def _even_out(sb2d, pool2d, seq, pool_w_bf16, pool_scale, w_out_bf16, x2d, gate):
    t, d = x2d.shape
    tm = TOKEN_TILE
    tiles_per_seq = seq // tm
    assert seq % CHUNK == 0 and len(POOL_WINDOWS) * LANES == pool2d.shape[1]
    pw = pool2d.shape[1]
    return pl.pallas_call(
        functools.partial(_even_out_kernel, tm=tm, tiles_per_seq=tiles_per_seq),
        grid=(t // tm,),
        in_specs=[pl.BlockSpec((tm, sb2d.shape[1]), lambda i: (i, 0)),
                  pl.BlockSpec((tm, pw), lambda i: (i, 0)),
                  pl.BlockSpec((tm, pw), lambda i: (jnp.maximum(i - 1, 0), 0)),
                  pl.BlockSpec(pool_w_bf16.shape, lambda i: (0, 0, 0)),
                  pl.BlockSpec((1, pw), lambda i: (0, 0)),
                  pl.BlockSpec((d, d), lambda i: (0, 0)),
                  pl.BlockSpec((tm, d), lambda i: (i, 0)),
                  pl.BlockSpec((1, 1, d), lambda i: (i // tiles_per_seq, 0, 0))],
        out_specs=pl.BlockSpec((tm, d), lambda i: (i, 0)),
        out_shape=jax.ShapeDtypeStruct((t, d), F32),
        compiler_params=_compiler_params(("arbitrary",)),
        name="even_out",
    )(sb2d, pool2d, pool2d, pool_w_bf16, pool_scale.reshape(1, pw), w_out_bf16, x2d, gate)


FOX_BIAS_TERMS = 3


def _fox_layout_constants(width, qk_bound):
    heads = width // HEAD_DIM
    nt = FOX_BIAS_TERMS
    wide = heads * LANES
    src = lax.broadcasted_iota(I32, (width, wide), 0)
    dst = lax.broadcasted_iota(I32, (width, wide), 1)
    place = ((dst // LANES == src // HEAD_DIM) & (dst % LANES == src % HEAD_DIM)).astype(BF16)
    src = lax.broadcasted_iota(I32, (nt * LANES, wide), 0)
    dst = lax.broadcasted_iota(I32, (nt * LANES, wide), 1)
    term, head = src // LANES, src % LANES
    place_dq = ((dst // LANES == head) & (dst % LANES == HEAD_DIM + term)).astype(BF16)
    place_dk = ((dst // LANES == head) & (dst % LANES == HEAD_DIM + nt + term)).astype(BF16)
    lane = lax.broadcasted_iota(I32, (1, wide), 1) % LANES
    bound_terms = _split_bf16(-qk_bound.astype(F32).reshape(1, 1), nt)
    q_const = jnp.where((lane >= HEAD_DIM + nt) & (lane < HEAD_DIM + 2 * nt), 1.0, 0.0)
    for k, term_k in enumerate(bound_terms):
        q_const = jnp.where(lane == HEAD_DIM + 2 * nt + k, term_k.astype(F32), q_const)
    k_const = jnp.where(((lane >= HEAD_DIM) & (lane < HEAD_DIM + nt))
                        | ((lane >= HEAD_DIM + 2 * nt) & (lane < HEAD_DIM + 3 * nt)), 1.0, 0.0)
    v_const = jnp.where(lane == HEAD_DIM, 1.0, 0.0)
    return place, place_dq, place_dk, q_const.astype(F32), k_const.astype(F32), v_const.astype(F32)


def _odd_prep_kernel(qk_ref, vin_ref, f_ref, rw_ref, rwprev_ref, hs_ref, tri_ref, ctri_ref,
                     place_ref, pdq_ref, pdk_ref, qc_ref, kc_ref, vc_ref,
                     qg_ref, kg_ref, fb_ref, mix_ref, w0_ref, w2_ref, a0_ref, a2_ref, g2_ref,
                     kkp_ref, ka_ref, rk_ref,
                     qa_ref, ka_out_ref, va_ref, drow_ref, rt_ref, kt_ref, at_ref, bt_ref, v_ref,
                     cw_ref, bonus_ref, g_ref, carry_ref, *, tm, tiles_per_seq, width):
    i = pl.program_id(0)
    first = (i % tiles_per_seq) == 0
    hs = hs_ref[...]

    def head_sum(x):
        return _dot_exact_rhs(x, hs, 2)

    inv_hd = 1.0 / HEAD_DIM
    q = qk_ref[:, :width]
    k = qk_ref[:, width:]
    qn = q * lax.rsqrt(head_sum(q * q) * inv_hd + NORM_EPS) * qg_ref[...] * (HEAD_DIM ** -0.5)
    kn = k * lax.rsqrt(head_sum(k * k) * inv_hd + NORM_EPS) * kg_ref[...]

    log_f = _log_sigmoid(f_ref[...] + fb_ref[...])

    @pl.when(first)
    def _():
        carry_ref[...] = jnp.zeros_like(carry_ref)

    dcum = _dot_exact_lhs(tri_ref[...], log_f) + carry_ref[...]
    carry_ref[...] = dcum[tm - 1:tm, :]
    drow_ref[0, 0] = jnp.transpose(dcum)[:SUBLANES, :]

    place = place_ref[...]
    d_terms = jnp.concatenate(_split_bf16(dcum, FOX_BIAS_TERMS), axis=1)
    qa_ref[...] = (jnp.dot(qn.astype(BF16), place, preferred_element_type=F32)
                   + jnp.dot(d_terms, pdq_ref[...], preferred_element_type=F32) + qc_ref[...]).astype(BF16)
    ka_out_ref[...] = (jnp.dot(kn.astype(BF16), place, preferred_element_type=F32)
                       + jnp.dot(-d_terms, pdk_ref[...], preferred_element_type=F32) + kc_ref[...]).astype(BF16)
    va_ref[...] = (jnp.dot(vin_ref[...], place, preferred_element_type=F32) + vc_ref[...]).astype(BF16)

    pd = rw_ref[...]
    prev_row = jnp.where(first, 0.0, rwprev_ref[SUBLANES - 1:SUBLANES, :])
    row0 = lax.broadcasted_iota(I32, (tm, 1), 0) == 0
    prev = jnp.where(row0, prev_row, pltpu.roll(pd, 1, axis=0))
    pd = pd + (prev - pd) * mix_ref[...]
    r = pd[:, :width]
    kr = pd[:, width:2 * width]
    vr = pd[:, 2 * width:3 * width]
    lora_in = pd[:, 3 * width:3 * width + LANES]
    xg = pd[:, 3 * width + LANES:3 * width + 2 * LANES]
    w_log = _log_sigmoid(w0_ref[...] + jnp.dot(jnp.tanh(lora_in).astype(BF16), w2_ref[...],
                                              preferred_element_type=F32)) - 0.5
    log_w = -jnp.exp(w_log)
    a = _sigmoid(a0_ref[...] + jnp.dot(lora_in.astype(BF16), a2_ref[...], preferred_element_type=F32))
    g_ref[...] = jnp.dot(_sigmoid(xg).astype(BF16), g2_ref[...], preferred_element_type=F32)
    kk = kr * kkp_ref[...]
    kk = kk / jnp.maximum(jnp.sqrt(head_sum(kk * kk)), 1e-12)
    k2 = kr * (1.0 + (a - 1.0) * ka_ref[...])
    bonus_ref[...] = head_sum(r * k2 * rk_ref[...]) * vr
    cw = _dot_exact_lhs(ctri_ref[...], log_w)
    grow = jnp.exp(-cw)
    rt_ref[...] = r * jnp.exp(cw)
    kt_ref[...] = k2 * grow
    bt_ref[...] = kk * a * grow
    at_ref[...] = -kk * jnp.exp(cw - log_w)
    v_ref[...] = vr
    cw_ref[...] = cw


def _odd_prep(qk, v, fproj, rw, seq, p, qk_bound):
    t = qk.shape[0]
    width = qk.shape[1] // 2
    tm = TOKEN_TILE
    tiles_per_seq = seq // tm
    nrw = rw.shape[1]
    assert tm % CHUNK == 0 and seq % tm == 0 and nrw == 3 * width + 2 * LANES
    assert HEAD_DIM + 3 * FOX_BIAS_TERMS <= LANES and width // HEAD_DIM <= SUBLANES
    ri = lax.broadcasted_iota(I32, (tm, tm), 0)
    ci = lax.broadcasted_iota(I32, (tm, tm), 1)
    tri = (ci <= ri).astype(BF16)
    ctri = ((ci <= ri) & (ci // CHUNK == ri // CHUNK)).astype(BF16)
    hs = _head_sum_matrix(width)
    row = lambda a: a.reshape(1, -1)
    full = lambda a: pl.BlockSpec(a.shape, lambda i: (0,) * a.ndim)
    tile = lambda n: pl.BlockSpec((tm, n), lambda i: (i, 0))
    consts = [hs, tri, ctri, *_fox_layout_constants(width, qk_bound),
              row(p['qg']), row(p['kg']), row(p['fb']), row(p['mix']), row(p['w0']),
              p['w2'], row(p['a0']), p['a2'], p['g2'], row(p['kk']), row(p['ka']), row(p['rk'])]
    wide = jax.ShapeDtypeStruct((t, width), F32)
    aug = (width // HEAD_DIM) * LANES
    return pl.pallas_call(
        functools.partial(_odd_prep_kernel, tm=tm, tiles_per_seq=tiles_per_seq, width=width),
        grid=(t // tm,),
        in_specs=[tile(2 * width), tile(width), tile(LANES), tile(nrw),
                  pl.BlockSpec((SUBLANES, nrw), lambda i: (jnp.maximum(i * (tm // SUBLANES) - 1, 0), 0))]
                 + [full(c) for c in consts],
        out_specs=[tile(aug), tile(aug), tile(aug),
                   pl.BlockSpec((1, 1, SUBLANES, tm), lambda i: (i // tiles_per_seq, i % tiles_per_seq, 0, 0))]
                  + [tile(width)] * 8,
        out_shape=[jax.ShapeDtypeStruct((t, aug), BF16)] * 3
                  + [jax.ShapeDtypeStruct((t // seq, tiles_per_seq, SUBLANES, tm), F32)] + [wide] * 8,
        scratch_shapes=[pltpu.VMEM((1, LANES), F32)],
        compiler_params=_compiler_params(("arbitrary",)),
        name="odd_prep",
    )(qk, v, fproj, rw, rw, *consts)


def _fox_attention_kernel(q_ref, k_ref, v_ref, drow_ref, o_ref, acc_ref, *, tile):
    hp = pl.program_id(1)
    i = pl.program_id(2)
    row = lax.broadcasted_iota(I32, (tile, tile), 0)
    col = lax.broadcasted_iota(I32, (tile, tile), 1)
    causal = col <= row
    sub = lax.broadcasted_iota(I32, (SUBLANES, tile), 0)
    hs = range(HEADS_PER_LANE_BLOCK)
    heads = [hp * HEADS_PER_LANE_BLOCK + h for h in hs]
    q = [q_ref[0, :, h * LANES:(h + 1) * LANES] for h in hs]

    def d_row(j, h):
        return jnp.sum(jnp.where(sub == heads[h], drow_ref[0, j], 0.0), axis=0, keepdims=True)

    def blocks(js, diag):
        work = [(pl.multiple_of(j * tile, tile), h) for j in js for h in hs]
        s = [lax.dot_general(q[h], k_ref[0, pl.ds(start, tile), h * LANES:(h + 1) * LANES],
                             (((1,), (1,)), ((), ())), preferred_element_type=F32) for start, h in work]
        p = [jnp.exp(x) for x in s]
        if diag:
            p = [jnp.where(causal, x, 0.0) for x in p]
        pv = [jnp.dot(p[n].astype(BF16), v_ref[0, pl.ds(start, tile), h * LANES:(h + 1) * LANES],
                      preferred_element_type=F32) for n, (start, h) in enumerate(work)]
        for h in hs:
            acc_ref[h] += functools.reduce(jnp.add, [pv[n] for n, (_, hh) in enumerate(work) if hh == h])

    acc_ref[...] = jnp.zeros_like(acc_ref)
    blocks([i], True)

    d_top = [jnp.max(d_row(i, h)) for h in hs]

    def alive(j):
        jc = jnp.maximum(j, 0)
        gaps = [d_top[h] - jnp.min(d_row(jc, h)) for h in hs]
        return functools.reduce(jnp.maximum, gaps) > F32_EXP_UNDERFLOW

    def cond(c):
        return jnp.logical_and(c[0] + 1 < i, c[1])

    def body(c):
        j = i - 1 - c[0]
        blocks([j, j - 1], False)
        return c[0] + 2, alive(j - 2)

    done, live = lax.while_loop(cond, body, (jnp.int32(0), alive(i - 1)))

    @pl.when(jnp.logical_and(done < i, live))
    def _():
        blocks([i - 1 - done], False)

    outs = [acc_ref[h][:, :HEAD_DIM] / acc_ref[h][:, HEAD_DIM:HEAD_DIM + 1] for h in hs]
    o_ref[0] = jnp.concatenate(outs, axis=1)


def _fox_qk_bound(q_gain, k_gain):
    bf16_slack = (1.0 + 2.0 ** -8) ** 2
    return (HEAD_DIM * HEAD_DIM ** -0.5 * bf16_slack) * jnp.max(jnp.abs(q_gain)) * jnp.max(jnp.abs(k_gain))


def _fox_attention(q_aug, k_aug, v_aug, drow):
    b, s, w = q_aug.shape
    nblk = w // (HEADS_PER_LANE_BLOCK * LANES)
    tile = ATTN_TILE
    assert drow.shape == (b, s // tile, SUBLANES, tile)
    pair = HEADS_PER_LANE_BLOCK * LANES
    return pl.pallas_call(
        functools.partial(_fox_attention_kernel, tile=tile),
        grid=(b, nblk, s // tile),
        in_specs=[pl.BlockSpec((1, tile, pair), lambda bi, hp, i: (bi, i, hp)),
                  pl.BlockSpec((1, s, pair), lambda bi, hp, i: (bi, 0, hp)),
                  pl.BlockSpec((1, s, pair), lambda bi, hp, i: (bi, 0, hp)),
                  pl.BlockSpec((1, s // tile, SUBLANES, tile), lambda bi, hp, i: (bi, 0, 0, 0))],
        out_specs=pl.BlockSpec((1, tile, LANES), lambda bi, hp, i: (bi, i, hp)),
        out_shape=jax.ShapeDtypeStruct((b, s, nblk * LANES), F32),
        scratch_shapes=[pltpu.VMEM((HEADS_PER_LANE_BLOCK, tile, LANES), F32)],
        compiler_params=_compiler_params(("arbitrary", "arbitrary", "arbitrary")),
        name="fox_attention",
    )(q_aug, k_aug, v_aug, drow)


_NN = (((1,), (0,)), ((), ()))
_NT = (((1,), (1,)), ((), ()))
_TN = (((0,), (0,)), ((), ()))


def _mm(a, b, dims=_NN):
    a_hi, a_lo = _split_bf16(a, 2)
    b_hi, b_lo = _split_bf16(b, 2)
    dot = lambda x, y: lax.dot_general(x, y, dims, preferred_element_type=F32)
    return dot(a_hi, b_hi) + (dot(a_hi, b_lo) + dot(a_lo, b_hi))


def _head_cols(h):
    return slice(h * HEAD_DIM, (h + 1) * HEAD_DIM)


def _rwkv_intra_kernel(rt_ref, kt_ref, at_ref, bt_ref, v_ref, inv_ref, arb_ref, fvu_ref, fvy_ref,
                       *, chunks, heads):
    c = CHUNK
    ri = lax.broadcasted_iota(I32, (2 * c, 2 * c), 0)
    ci = lax.broadcasted_iota(I32, (2 * c, 2 * c), 1)
    keep = (ci % c) < jnp.where(ri < c, ri, ri - c + 1)
    eye = (lax.broadcasted_iota(I32, (c, c), 0) == lax.broadcasted_iota(I32, (c, c), 1)).astype(F32)
    hs = range(heads)

    def chunk(cidx, carry):
        rows = pl.ds(pl.multiple_of(cidx * c, c), c)
        a_all = [jnp.where(keep,
                           _mm(jnp.concatenate([at_ref[rows, _head_cols(h)], rt_ref[rows, _head_cols(h)]], axis=0),
                               jnp.concatenate([bt_ref[rows, _head_cols(h)], kt_ref[rows, _head_cols(h)]], axis=0),
                               _NT), 0.0) for h in hs]
        a_ab = [a[:c, :c] for a in a_all]
        inv = [eye + a for a in a_ab]
        power = [_mm(a, a) for a in a_ab]
        for _ in range(4):
            both = [_mm(jnp.concatenate([inv[h], power[h]], axis=0), power[h]) for h in hs]
            inv = [inv[h] + both[h][:c] for h in hs]
            power = [both[h][c:] for h in hs]
        inv = [inv[h] + _mm(inv[h], power[h]) for h in hs]
        from_v = [_mm(a_all[h][:, c:], v_ref[rows, _head_cols(h)]) for h in hs]
        inv_ref[rows, :] = jnp.concatenate(inv, axis=1)
        arb_ref[rows, :] = jnp.concatenate([a[c:, :c] for a in a_all], axis=1)
        fvu_ref[rows, :] = jnp.concatenate([f[:c] for f in from_v], axis=1)
        fvy_ref[rows, :] = jnp.concatenate([f[c:] for f in from_v], axis=1)
        return carry

    lax.fori_loop(0, chunks, chunk, 0)


def _rwkv_scan_kernel(rt_ref, kt_ref, at_ref, bt_ref, v_ref, cw_ref, inv_ref, arb_ref, fvu_ref, fvy_ref,
                      y_ref, st_ref, *, steps_per_seq, chunks, heads):
    i = pl.program_id(0)

    @pl.when(i % steps_per_seq == 0)
    def _():
        st_ref[...] = jnp.zeros_like(st_ref)

    c = CHUNK
    hs = range(heads)

    def chunk(cidx, carry):
        rows = pl.ds(pl.multiple_of(cidx * c, c), c)
        last = pl.ds(pl.multiple_of(cidx * c, c) + (c - SUBLANES), SUBLANES)
        ar = [jnp.concatenate([at_ref[rows, _head_cols(h)], rt_ref[rows, _head_cols(h)]], axis=0) for h in hs]
        bk = [jnp.concatenate([bt_ref[rows, _head_cols(h)], kt_ref[rows, _head_cols(h)]], axis=0) for h in hs]
        st = [st_ref[h] for h in hs]
        from_state = [_mm(ar[h], st[h], _NT) for h in hs]
        u = [_mm(inv_ref[rows, _head_cols(h)], from_state[h][:c] + fvu_ref[rows, _head_cols(h)]) for h in hs]
        y = [from_state[h][c:] + fvy_ref[rows, _head_cols(h)] + _mm(arb_ref[rows, _head_cols(h)], u[h])
             for h in hs]
        for h in hs:
            decay = jnp.exp(cw_ref[last, _head_cols(h)][SUBLANES - 1:SUBLANES, :])
            uv = jnp.concatenate([u[h], v_ref[rows, _head_cols(h)]], axis=0)
            st_ref[h] = (st[h] + _mm(uv, bk[h], _TN)) * decay
        y_ref[rows, :] = jnp.concatenate(y, axis=1)
        return carry

    lax.fori_loop(0, chunks, chunk, 0)


def _rwkv_scan(rt, kt, at, bt, v, cw, seq):
    t, w = rt.shape
    heads = w // HEAD_DIM
    ts = 2 * TOKEN_TILE
    assert seq % ts == 0
    spec = pl.BlockSpec((ts, w), lambda i: (i, 0))
    wide = jax.ShapeDtypeStruct((t, w), F32)
    inv, arb, fvu, fvy = pl.pallas_call(
        functools.partial(_rwkv_intra_kernel, chunks=ts // CHUNK, heads=heads),
        grid=(t // ts,),
        in_specs=[spec] * 5,
        out_specs=[spec] * 4,
        out_shape=[wide] * 4,
        compiler_params=_compiler_params(("arbitrary",)),
        name="rwkv_intra",
    )(rt, kt, at, bt, v)
    return pl.pallas_call(
        functools.partial(_rwkv_scan_kernel, steps_per_seq=seq // ts, chunks=ts // CHUNK, heads=heads),
        grid=(t // ts,),
        in_specs=[spec] * 10,
        out_specs=spec,
        out_shape=wide,
        scratch_shapes=[pltpu.VMEM((heads, HEAD_DIM, HEAD_DIM), F32)],
        compiler_params=_compiler_params(("arbitrary",)),
        name="rwkv_scan",
    )(rt, kt, at, bt, v, cw, inv, arb, fvu, fvy)


def _odd_out_kernel(att_ref, og_ref, y_ref, bonus_ref, g_ref, hs_ref, lnw_ref, lnb_ref, wo_ref,
                    x_ref, gate_ref, o_ref, *, width):
    hs = hs_ref[...]
    inv_hd = 1.0 / HEAD_DIM
    fox = att_ref[...] * _sigmoid(og_ref[...])
    y = y_ref[...]
    mu = _dot_exact_rhs(y, hs, 2) * inv_hd
    yc = y - mu
    var = _dot_exact_rhs(yc * yc, hs, 2) * inv_hd
    yn = yc * lax.rsqrt(var + RWKV_GN_EPS) * lnw_ref[...] + lnb_ref[...]
    rw = (yn + bonus_ref[...]) * g_ref[...]
    acc = jnp.dot(fox.astype(BF16), wo_ref[:width, :], preferred_element_type=F32)
    acc += jnp.dot(rw.astype(BF16), wo_ref[width:, :], preferred_element_type=F32)
    o_ref[...] = x_ref[...] + gate_ref[0] * acc


def _odd_out(att, og, y, bonus, g, ln_w, ln_b, w_out_bf16, x2d, seq, gate):
    t, d = x2d.shape
    width = att.shape[1]
    tm = TOKEN_TILE
    tiles_per_seq = seq // tm
    hs = _head_sum_matrix(width)
    tile = lambda n: pl.BlockSpec((tm, n), lambda i: (i, 0))
    return pl.pallas_call(
        functools.partial(_odd_out_kernel, width=width),
        grid=(t // tm,),
        in_specs=[tile(width)] * 5
                 + [pl.BlockSpec(hs.shape, lambda i: (0, 0)),
                    pl.BlockSpec((1, width), lambda i: (0, 0)), pl.BlockSpec((1, width), lambda i: (0, 0)),
                    pl.BlockSpec((d, d), lambda i: (0, 0)), tile(d),
                    pl.BlockSpec((1, 1, d), lambda i: (i // tiles_per_seq, 0, 0))],
        out_specs=tile(d),
        out_shape=jax.ShapeDtypeStruct((t, d), F32),
        compiler_params=_compiler_params(("arbitrary",)),
        name="odd_out",
    )(att, og, y, bonus, g, hs, ln_w.reshape(1, width), ln_b.reshape(1, width), w_out_bf16, x2d, gate)


def _top_rows(s, k, prio, payload=None):
    big = jnp.int32(2 ** 30)
    vals, picks = [], []
    for _ in range(k):
        m = jnp.max(s, axis=0, keepdims=True)
        am = jnp.min(jnp.where(s == m, prio, big), axis=0, keepdims=True)
        hit = prio == am
        vals.append(m)
        if payload is None:
            picks.append(am)
        else:
            picks.append(jnp.max(jnp.where(hit, payload, -1), axis=0, keepdims=True))
        s = jnp.where(hit, -jnp.inf, s)
    return jnp.concatenate(vals, axis=0), jnp.concatenate(picks, axis=0)


def _staircase_groups():
    k, g = PEER_TOPK, SUBLANES
    groups = []
    for b in range(k):
        count = k // (b + 1)
        if count <= 1:
            break
        for a0 in range(0, count, g):
            groups.append((a0, b, False, min(g, count - a0)))
    first_single = next(b for b in range(k) if k // (b + 1) <= 1)
    for b0 in range(first_single, k, g):
        groups.append((0, b0, True, min(g, k - b0)))
    return groups


def _topk_constants(tt):
    sub = lax.broadcasted_iota(I32, (SUBLANES, 1), 0)
    groups = _staircase_groups()
    prio = jnp.concatenate(
        [jnp.broadcast_to((sub * 0 + a0) * PEER_TOPK + b0 + sub if along_b
                          else (a0 + sub) * PEER_TOPK + b0, (SUBLANES, tt))
         for a0, b0, along_b, _ in groups], axis=0)
    return lax.broadcasted_iota(I32, (PEER_NKEYS, tt), 0), sub, groups, prio


def _topk_head(q_head, k1, k2, consts):
    key_id, sub, groups, prio = consts
    half = PEER_NKEYS
    nt = (((1,), (1,)), ((), ()))
    s1 = lax.dot_general(k1, q_head[:, :half], nt, precision=HIGHEST, preferred_element_type=F32)
    s2 = lax.dot_general(k2, q_head[:, half:], nt, precision=HIGHEST, preferred_element_type=F32)
    v1, i1 = _top_rows(s1, PEER_TOPK, key_id)
    v2, i2 = _top_rows(s2, PEER_TOPK, key_id)
    cand, cand_idx = [], []
    for a0, b0, along_b, valid in groups:
        if along_b:
            val = v1[a0:a0 + 1] + v2[b0:b0 + SUBLANES]
            eid = i1[a0:a0 + 1] * PEER_NKEYS + i2[b0:b0 + SUBLANES]
        else:
            val = v1[a0:a0 + SUBLANES] + v2[b0:b0 + 1]
            eid = i1[a0:a0 + SUBLANES] * PEER_NKEYS + i2[b0:b0 + 1]
        cand.append(jnp.where(sub < valid, val, -jnp.inf))
        cand_idx.append(eid)
    top_s, idx = _top_rows(jnp.concatenate(cand, axis=0), PEER_TOPK, prio,
                           payload=jnp.concatenate(cand_idx, axis=0))
    e = jnp.exp(top_s - top_s[0:1])
    return idx, e / jnp.sum(e, axis=0, keepdims=True)


def _pack_table_kernel(t_ref, o_ref, *, pairs):
    high = jnp.int32(-65536)
    for j in range(pairs):
        lo = t_ref[:, (2 * j) * LANES:(2 * j + 1) * LANES].astype(BF16).astype(F32)
        hi = t_ref[:, (2 * j + 1) * LANES:(2 * j + 2) * LANES].astype(BF16).astype(F32)
        lo_bits = lax.shift_right_logical(pltpu.bitcast(lo, I32), 16)
        words = lax.bitwise_or(lo_bits, lax.bitwise_and(pltpu.bitcast(hi, I32), high))
        o_ref[pl.ds(j, t_ref.shape[0], stride=pairs), :] = words


def _pack_table(table):
    e, d = table.shape
    pairs = d // (2 * LANES)
    rows = 2 * TOKEN_TILE
    return pl.pallas_call(
        functools.partial(_pack_table_kernel, pairs=pairs),
        grid=(e // rows,),
        in_specs=[pl.BlockSpec((rows, d), lambda i: (i, 0))],
        out_specs=pl.BlockSpec((rows * pairs, LANES), lambda i: (i, 0)),
        out_shape=jax.ShapeDtypeStruct((e * pairs, LANES), I32),
        compiler_params=_compiler_params(("arbitrary",)),
        name="pack_table",
    )(table)


def _unpack_words(w):
    lo = pltpu.bitcast(lax.shift_left(w, 16), F32)
    hi = pltpu.bitcast(lax.bitwise_and(w, jnp.int32(-65536)), F32)
    return lo, hi


def _gather_rows(idx_ref, tab_ref, tile_refs, toks, pairs):
    tok_idx = [idx_ref.at[tok] for tok in toks]
    for m in range(PEER_SLOTS):
        for rows, tile_ref in zip(tok_idx, tile_refs):
            row = pl.multiple_of(rows[m], pairs)
            tile_ref[m * pairs:(m + 1) * pairs, :] = tab_ref[pl.ds(row, pairs), :]


def _tile_chunk(tile_ref, j, pairs):
    return _unpack_words(tile_ref[pl.ds(j, PEER_SLOTS, stride=pairs), :])


PEER_TILE_BUFFERS = 8


def _pipelined_tokens(ntok, gather, compute, init, trips_per_iter=1, per_iter=None):
    nb = PEER_TILE_BUFFERS
    group = nb // 2
    assert ntok % (nb * trips_per_iter) == 0
    gather(list(range(group)), list(range(group)))

    def trip(p, carry):
        tok = nb * p
        for half in range(2):
            first = tok + half * group
            ahead = [jnp.minimum(first + group + k, ntok - 1) for k in range(group)]
            gather(ahead, [((half + 1) % 2) * group + k for k in range(group)])
            for k in range(group):
                carry = compute(first + k, half * group + k, carry)
        return carry

    def iteration(i, carry):
        if per_iter is not None:
            per_iter(i)
        for r in range(trips_per_iter):
            carry = trip(i * trips_per_iter + r, carry)
        return carry

    return lax.fori_loop(0, ntok // (nb * trips_per_iter), iteration, init)


def _pick_row(block, r):
    sub = lax.broadcasted_iota(I32, block.shape, 0)
    return jnp.sum(jnp.where(sub == r, block, 0.0), axis=0, keepdims=True)


def _peer_select_act_kernel(q_ref, k1_ref, k2_ref, tab_ref, h_ref, idx_ref, coef_ref,
                            idx_smem, stage_ref, slots_ref, gate_ref, act_ref, sem, *tiles, pairs, nblk):
    n = pl.program_id(0)
    ntok = h_ref.shape[0]
    copy_idx = pltpu.make_async_copy(stage_ref, idx_smem, sem)

    @pl.when(n == 0)
    def _():
        stage_ref[...] = jnp.zeros_like(stage_ref)
        gate_ref[...] = jnp.zeros_like(gate_ref)
        copy_idx.start()

    copy_idx.wait()
    act_ref[...] = jnp.zeros_like(act_ref)
    lane = lax.broadcasted_iota(I32, (PEER_SLOTS, LANES), 1)
    consts = _topk_constants(LANES)
    slot_now = n % 2

    def gather(toks, bufs):
        _gather_rows(idx_smem, tab_ref, [tiles[b] for b in bufs], toks, pairs)

    def compute(tok, buf, carry):
        base = pl.multiple_of((tok // SUBLANES) * SUBLANES, SUBLANES)
        h_row = _pick_row(h_ref[pl.ds(base, SUBLANES), :], tok % SUBLANES)
        terms = []
        for j in range(pairs):
            lo, hi = _tile_chunk(tiles[buf], j, pairs)
            h_lo = h_row[:, (2 * j) * LANES:(2 * j + 1) * LANES]
            h_hi = h_row[:, (2 * j + 1) * LANES:(2 * j + 2) * LANES]
            terms.append(lo * h_lo + hi * h_hi)
        while len(terms) > 1:
            terms = [a + b for a, b in zip(terms[::2], terms[1::2])]
        col = jnp.sum(terms[0], axis=1, keepdims=True)
        part = tok // LANES
        act_ref[part] = jnp.where(lane == tok % LANES, col, act_ref[part])
        return carry

    parts = ntok // LANES
    heads_per_iter = 2

    def select(i):
        for k in range(heads_per_iter):
            hd = i * heads_per_iter + k
            rows = pl.ds(pl.multiple_of(hd * PEER_TOPK, PEER_TOPK), PEER_TOPK)
            for part in range(parts):
                toks = slice(part * LANES, (part + 1) * LANES)
                idx, gate = _topk_head(q_ref[hd, toks, :], k1_ref[...], k2_ref[...], consts)
                slots_ref[part, rows, :] = idx
                gate_ref[slot_now, part, rows, :] = gate

    heads = q_ref.shape[0]
    _pipelined_tokens(ntok, gather, compute, 0,
                      trips_per_iter=ntok * heads_per_iter // (PEER_TILE_BUFFERS * heads), per_iter=select)

    offsets = jnp.concatenate([jnp.transpose(slots_ref[p]) for p in range(parts)], axis=0) * pairs
    idx_ref[...] = offsets
    stage_ref[...] = offsets

    @pl.when(n < nblk)
    def _():
        copy_idx.start()

    act_t = jnp.concatenate([act_ref[p] for p in range(parts)], axis=1)
    gate_prev = jnp.concatenate([gate_ref[1 - slot_now, p] for p in range(parts)], axis=1)
    gelu = 0.5 * act_t * (1.0 + lax.erf(act_t * (2.0 ** -0.5)))
    coef_ref[...] = gate_prev * gelu


def _peer_select_act(q_heads, k1, k2, table_words, h2d):
    heads, t, _ = q_heads.shape
    d = h2d.shape[1]
    pairs = d // (2 * LANES)
    tb = TOKEN_TILE
    nblk = t // tb
    assert heads == PEER_HEADS and tb % (PEER_TILE_BUFFERS * heads) == 0 and tb % LANES == 0
    cur = lambda n: jnp.minimum(n, nblk - 1)
    prev = lambda n: jnp.maximum(n - 1, 0)
    return pl.pallas_call(
        functools.partial(_peer_select_act_kernel, pairs=pairs, nblk=nblk),
        grid=(nblk + 1,),
        in_specs=[pl.BlockSpec((heads, tb, q_heads.shape[2]), lambda n: (0, cur(n), 0)),
                  pl.BlockSpec(k1.shape, lambda n: (0, 0)),
                  pl.BlockSpec(k2.shape, lambda n: (0, 0)),
                  pl.BlockSpec(table_words.shape, lambda n: (0, 0), pipeline_mode=pl.Buffered(1)),
                  pl.BlockSpec((tb, d), lambda n: (prev(n), 0))],
        out_specs=[pl.BlockSpec((tb, PEER_SLOTS), lambda n: (cur(n), 0)),
                   pl.BlockSpec((PEER_SLOTS, tb), lambda n: (0, prev(n)))],
        out_shape=[jax.ShapeDtypeStruct((t, PEER_SLOTS), I32),
                   jax.ShapeDtypeStruct((PEER_SLOTS, t), F32)],
        scratch_shapes=[pltpu.SMEM((tb, PEER_SLOTS), I32),
                        pltpu.VMEM((tb, PEER_SLOTS), I32),
                        pltpu.VMEM((tb // LANES, PEER_SLOTS, LANES), I32),
                        pltpu.VMEM((2, tb // LANES, PEER_SLOTS, LANES), F32),
                        pltpu.VMEM((tb // LANES, PEER_SLOTS, LANES), F32),
                        pltpu.SemaphoreType.DMA]
                       + [pltpu.VMEM((pairs * PEER_SLOTS, LANES), I32)] * PEER_TILE_BUFFERS,
        compiler_params=_compiler_params(("arbitrary",), vmem_mb=56),
        name="peer_select_act",
    )(q_heads, k1, k2, table_words, h2d)


def _peer_out_kernel(idx_ref, tab_ref, coef_ref, x_ref, g_ref, o_ref, *tiles, pairs):
    ntok, d = x_ref.shape
    lane = lax.broadcasted_iota(I32, (PEER_SLOTS, ntok), 1)
    sub = lax.broadcasted_iota(I32, (SUBLANES, d), 0)
    groups = PEER_SLOTS // SUBLANES

    def gather(toks, bufs):
        _gather_rows(idx_ref, tab_ref, [tiles[b] for b in bufs], toks, pairs)

    def compute(tok, buf, rows):
        coef = jnp.sum(jnp.where(lane == tok, coef_ref[...], 0.0), axis=1, keepdims=True)
        pieces = []
        for j in range(pairs):
            for vals in _tile_chunk(tiles[buf], j, pairs):
                prod = (vals * coef).reshape(groups, SUBLANES, LANES)
                pieces.append(jnp.sum(jnp.sum(prod, axis=0), axis=0, keepdims=True))
        row = jnp.concatenate(pieces, axis=1)
        rows = jnp.where(sub == tok % SUBLANES, row, rows)
        base = pl.multiple_of((tok // SUBLANES) * SUBLANES, SUBLANES)
        o_ref[pl.ds(base, SUBLANES), :] = x_ref[pl.ds(base, SUBLANES), :] + g_ref[0] * rows
        return rows

    _pipelined_tokens(ntok, gather, compute, jnp.zeros((SUBLANES, d), F32))


def _peer_out(idx, table_words, coef_t, x2d, seq, gate):
    t, d = x2d.shape
    pairs = d // (2 * LANES)
    nt = PEER_TOKENS
    per_seq = seq // nt
    return pl.pallas_call(
        functools.partial(_peer_out_kernel, pairs=pairs),
        grid=(t // nt,),
        in_specs=[pl.BlockSpec((nt, PEER_SLOTS), lambda i: (i, 0), memory_space=pltpu.SMEM),
                  pl.BlockSpec(table_words.shape, lambda i: (0, 0), pipeline_mode=pl.Buffered(1)),
                  pl.BlockSpec((PEER_SLOTS, nt), lambda i: (0, i)),
                  pl.BlockSpec((nt, d), lambda i: (i, 0)),
                  pl.BlockSpec((1, 1, d), lambda i: (i // per_seq, 0, 0))],
        out_specs=pl.BlockSpec((nt, d), lambda i: (i, 0)),
        out_shape=jax.ShapeDtypeStruct((t, d), F32),
        scratch_shapes=[pltpu.VMEM((pairs * PEER_SLOTS, LANES), I32)] * PEER_TILE_BUFFERS,
        compiler_params=_compiler_params(("arbitrary",), vmem_mb=56),
        name="peer_out",
    )(idx, table_words, coef_t, x2d, gate)


def _peer_ffn(x2d, seq, gain, shift, scale, gate, wq_bf16, k1, k2, u_words, v_words):
    n = wq_bf16.shape[1]
    q_heads, h = _norm_proj(x2d, seq, gain, shift, scale, wq_bf16, [(0, n)], [F32], emit_h=True,
                            head_major=PEER_HEADS)
    idx, coef_t = _peer_select_act(q_heads, k1, k2, u_words, h)
    return _peer_out(idx, v_words, coef_t, x2d, seq, gate)


def _final_norm_kernel(x_ref, gain_ref, o_ref):
    x = x_ref[...]
    ms = jnp.mean(x * x, axis=-1, keepdims=True)
    o_ref[...] = x * lax.rsqrt(ms + NORM_EPS) * gain_ref[...]


def _final_norm(x2d, gain):
    t, d = x2d.shape
    tm = TOKEN_TILE
    return pl.pallas_call(
        _final_norm_kernel,
        grid=(t // tm,),
        in_specs=[pl.BlockSpec((tm, d), lambda i: (i, 0)), pl.BlockSpec((1, d), lambda i: (0, 0))],
        out_specs=pl.BlockSpec((tm, d), lambda i: (i, 0)),
        out_shape=jax.ShapeDtypeStruct((t, d), F32),
        compiler_params=_compiler_params(("arbitrary",)),
        name="final_norm",
    )(x2d, gain.reshape(1, d))


def _even_mixer(x2d, b, seq, gain, shift, scale, gate, w_in, w_out, pool_w, pool_scale):
    d = x2d.shape[1]
    pool_width = len(POOL_WINDOWS) * LANES
    sb_width = (w_in.shape[1] - pool_width) // 3
    qkv, pool = _norm_proj(x2d, seq, gain, shift, scale, w_in.astype(BF16),
                           [(0, 3 * sb_width), (3 * sb_width, pool_width)], [BF16, F32])
    sb = _sb_attention(qkv.reshape(b, seq, 3 * sb_width), sb_width // HEAD_DIM)
    return _even_out(sb.reshape(b * seq, sb_width), pool, seq, pool_w.astype(BF16), pool_scale,
                     w_out.astype(BF16), x2d, gate)


def _odd_mixer(x2d, b, seq, gain, shift, scale, gate, w_in, w_out, fox_qnorm, fox_knorm, fox_fbias,
               rwkv_mix, rwkv_w0, rwkv_w2, rwkv_a0, rwkv_a2, rwkv_g2, rwkv_kk, rwkv_ka, rwkv_rk,
               rwkv_ln_w, rwkv_ln_b):
    d = x2d.shape[1]
    width = rwkv_w0.shape[0]
    heads = width // HEAD_DIM
    fox_in = 4 * width + heads
    n_rw = w_in.shape[1] - fox_in
    w_cols = jnp.concatenate([w_in[:, :4 * width], w_in[:, fox_in:], w_in[:, 4 * width:fox_in],
                              jnp.zeros((d, LANES - heads), w_in.dtype)], axis=1).astype(BF16)
    qk, v, og, rw, fproj = _norm_proj(
        x2d, seq, gain, shift, scale, w_cols,
        [(0, 2 * width), (2 * width, width), (3 * width, width), (4 * width, n_rw), (4 * width + n_rw, LANES)],
        [F32, BF16, F32, F32, F32])
    lora = rwkv_w2.shape[0]
    assert 2 * lora == LANES and rwkv_a2.shape[0] == lora and rwkv_g2.shape[0] == LANES
    zeros = jnp.zeros((lora, width), F32)
    params = dict(
        qg=jnp.tile(fox_qnorm, heads), kg=jnp.tile(fox_knorm, heads),
        fb=jnp.concatenate([fox_fbias, jnp.zeros((LANES - heads,), F32)]),
        mix=rwkv_mix, w0=rwkv_w0, a0=rwkv_a0,
        w2=jnp.concatenate([rwkv_w2, zeros], axis=0).astype(BF16),
        a2=jnp.concatenate([zeros, rwkv_a2], axis=0).astype(BF16),
        g2=rwkv_g2.astype(BF16), kk=rwkv_kk, ka=rwkv_ka, rk=rwkv_rk.reshape(-1))
    q_aug, k_aug, v_aug, drow, rt, kt, at, bt, vv, cw, bonus, g = _odd_prep(
        qk, v, fproj, rw, seq, params, _fox_qk_bound(fox_qnorm, fox_knorm))
    r3 = lambda a: a.reshape(b, seq, -1)
    att = _fox_attention(r3(q_aug), r3(k_aug), r3(v_aug), drow)
    y = _rwkv_scan(rt, kt, at, bt, vv, cw, seq)
    return _odd_out(att.reshape(b * seq, width), og, y, bonus, g, rwkv_ln_w, rwkv_ln_b,
                    w_out.astype(BF16), x2d, seq, gate)


def kernel(x, c, ada_w, ada_b, norm_mix, norm_ffn, ev_w_in, ev_w_out, pool_w, pool_scale, od_w_in, od_w_out, fox_qnorm, fox_knorm, fox_fbias, rwkv_mix, rwkv_w0, rwkv_w2, rwkv_a0, rwkv_a2, rwkv_g2, rwkv_kk, rwkv_ka, rwkv_rk, rwkv_ln_w, rwkv_ln_b, peer_wq, peer_k1, peer_k2, peer_u, peer_v, final_norm):
    b, seq, d = x.shape
    depth = ada_w.shape[0]
    assert TOKEN_TILE == ATTN_TILE and seq % (2 * TOKEN_TILE) == 0
    mod = _adaln(c, ada_w, ada_b).reshape(depth, b, 6, 1, d)
    x2d = x.reshape(b * seq, d)
    for layer in range(depth):
        sh1, sc1, g1, sh2, sc2, g2 = (mod[layer, :, i] for i in range(6))
        j = layer // 2
        if layer % 2 == 0:
            x2d = _even_mixer(x2d, b, seq, norm_mix[layer], sh1, sc1, g1, ev_w_in[j], ev_w_out[j],
                              pool_w[j], pool_scale[j])
        else:
            x2d = _odd_mixer(x2d, b, seq, norm_mix[layer], sh1, sc1, g1, od_w_in[j], od_w_out[j],
                             fox_qnorm[j], fox_knorm[j], fox_fbias[j], rwkv_mix[j], rwkv_w0[j],
                             rwkv_w2[j], rwkv_a0[j], rwkv_a2[j], rwkv_g2[j], rwkv_kk[j], rwkv_ka[j],
                             rwkv_rk[j], rwkv_ln_w[j], rwkv_ln_b[j])
        x2d = _peer_ffn(x2d, seq, norm_ffn[layer], sh2, sc2, g2, peer_wq[layer].astype(BF16),
                        peer_k1[layer], peer_k2[layer], _pack_table(peer_u[layer]),
                        _pack_table(peer_v[layer]))
    return _final_norm(x2d, final_norm).reshape(b, seq, d)
```

```python
import functools

import jax
import jax.numpy as jnp
from jax import lax
from jax.experimental import pallas as pl
from jax.experimental.pallas import tpu as pltpu

F32 = jnp.float32
BF16 = jnp.bfloat16
I32 = jnp.int32
HIGHEST = lax.Precision.HIGHEST

LANES = 128
SUBLANES = 8
VMEM_BYTES_V7X = 64 * 1024 * 1024

HEAD_DIM = 64
HEADS_PER_LANE_BLOCK = LANES // HEAD_DIM
NORM_EPS = 1e-6
RWKV_GN_EPS = 64e-5
CHUNK = 64
POOL_WINDOWS = (2, 4, 8, 16)
PEER_TOPK = 16
PEER_NKEYS = 128
PEER_HEADS = 8
PEER_SLOTS = PEER_HEADS * PEER_TOPK

TOKEN_TILE = 256
PROJ_TILE = 512
ATTN_TILE = 256
PEER_TOKENS = 128

F32_EXP_UNDERFLOW = -104.0


def _compiler_params(semantics, vmem_mb=48):
    return pltpu.CompilerParams(dimension_semantics=semantics,
                                vmem_limit_bytes=min(vmem_mb * 1024 * 1024, VMEM_BYTES_V7X))


def _split_bf16(x, n):
    parts = []
    r = x
    for _ in range(n):
        p = r.astype(BF16)
        parts.append(p)
        r = r - p.astype(F32)
    return parts


def _dot_exact_rhs(x, m, n=3):
    acc = None
    for p in _split_bf16(x, n):
        t = jnp.dot(p, m, preferred_element_type=F32)
        acc = t if acc is None else acc + t
    return acc


def _dot_exact_lhs(m, x, n=3):
    acc = None
    for p in _split_bf16(x, n):
        t = jnp.dot(m, p, preferred_element_type=F32)
        acc = t if acc is None else acc + t
    return acc


def _log_sigmoid(z):
    return jnp.minimum(z, 0.0) - jnp.log1p(jnp.exp(-jnp.abs(z)))


def _sigmoid(z):
    return 1.0 / (1.0 + jnp.exp(-z))


def _head_sum_matrix(width):
    r = lax.broadcasted_iota(I32, (width, width), 0) // HEAD_DIM
    c = lax.broadcasted_iota(I32, (width, width), 1) // HEAD_DIM
    return (r == c).astype(BF16)


def _adaln_kernel(c_ref, w_ref, b_ref, o_ref):
    c = c_ref[...]
    c_act = c * _sigmoid(c)
    o_ref[0] = jnp.dot(c_act, w_ref[0], precision=HIGHEST, preferred_element_type=F32) + b_ref[0]


def _adaln(c, ada_w, ada_b):
    b, d = c.shape
    depth, _, n = ada_w.shape
    tn = 1536
    assert n % tn == 0 and b <= SUBLANES
    c_pad = jnp.zeros((SUBLANES, d), F32).at[:b].set(c)
    out = pl.pallas_call(
        _adaln_kernel,
        grid=(depth, n // tn),
        in_specs=[pl.BlockSpec((SUBLANES, d), lambda l, j: (0, 0)),
                  pl.BlockSpec((1, d, tn), lambda l, j: (l, 0, j)),
                  pl.BlockSpec((1, 1, tn), lambda l, j: (l, 0, j))],
        out_specs=pl.BlockSpec((1, SUBLANES, tn), lambda l, j: (l, 0, j)),
        out_shape=jax.ShapeDtypeStruct((depth, SUBLANES, n), F32),
        compiler_params=_compiler_params(("arbitrary", "arbitrary")),
        name="adaln",
    )(c_pad, ada_w, ada_b.reshape(depth, 1, n))
    return out[:, :b, :]


def _norm_proj_kernel(x_ref, gain_ref, shift_ref, scale_ref, w_ref, *out_refs, groups, emit_h):
    x = x_ref[...]
    ms = jnp.mean(x * x, axis=-1, keepdims=True)
    y = x * lax.rsqrt(ms + NORM_EPS) * gain_ref[...]
    h = y * (1.0 + scale_ref[0]) + shift_ref[0]
    hb = h.astype(BF16)
    for (start, width), o_ref in zip(groups, out_refs):
        res = jnp.dot(hb, w_ref[:, start:start + width], preferred_element_type=F32).astype(o_ref.dtype)
        if len(o_ref.shape) == 3:
            per_head = o_ref.shape[2]
            for hd in range(o_ref.shape[0]):
                o_ref[hd] = res[:, hd * per_head:(hd + 1) * per_head]
        else:
            o_ref[...] = res
    if emit_h:
        out_refs[-1][...] = h


def _norm_proj(x2d, seq, gain, shift, scale, w_bf16, groups, dtypes, emit_h=False, head_major=0):
    t, d = x2d.shape
    n = w_bf16.shape[1]
    tm = PROJ_TILE
    assert t % tm == 0 and seq % tm == 0
    tiles_per_seq = seq // tm
    out_shape = [jax.ShapeDtypeStruct((t, wd), dt) for (_, wd), dt in zip(groups, dtypes)]
    out_specs = [pl.BlockSpec((tm, wd), lambda i: (i, 0)) for (_, wd) in groups]
    if head_major:
        (_, wd), = groups
        out_shape = [jax.ShapeDtypeStruct((head_major, t, wd // head_major), dtypes[0])]
        out_specs = [pl.BlockSpec((head_major, tm, wd // head_major), lambda i: (0, i, 0))]
    if emit_h:
        out_shape.append(jax.ShapeDtypeStruct((t, d), F32))
        out_specs.append(pl.BlockSpec((tm, d), lambda i: (i, 0)))
    return pl.pallas_call(
        functools.partial(_norm_proj_kernel, groups=tuple(groups), emit_h=emit_h),
        grid=(t // tm,),
        in_specs=[pl.BlockSpec((tm, d), lambda i: (i, 0)),
                  pl.BlockSpec((1, d), lambda i: (0, 0)),
                  pl.BlockSpec((1, 1, d), lambda i: (i // tiles_per_seq, 0, 0)),
                  pl.BlockSpec((1, 1, d), lambda i: (i // tiles_per_seq, 0, 0)),
                  pl.BlockSpec((d, n), lambda i: (0, 0))],
        out_specs=out_specs,
        out_shape=out_shape,
        compiler_params=_compiler_params(("arbitrary",)),
        name="norm_proj",
    )(x2d, gain.reshape(1, d), shift, scale, w_bf16)


def _sb_attention_kernel(q_ref, k_ref, v_ref, o_ref, acc_ref, carry_ref, *, tile, scale):
    i = pl.program_id(2)
    lane = lax.broadcasted_iota(I32, (1, LANES), 1)
    row = lax.broadcasted_iota(I32, (tile, tile), 0)
    col = lax.broadcasted_iota(I32, (tile, tile), 1)
    strict = col < row
    later = (row > col).astype(BF16)
    q = q_ref[0]
    hs = range(HEADS_PER_LANE_BLOCK)
    qh = [jnp.where((lane >= HEAD_DIM * h) & (lane < HEAD_DIM * (h + 1)), q, jnp.zeros_like(q)) for h in hs]

    def block(j, diag):
        start = pl.multiple_of(j * tile, tile)
        kb = k_ref[0, pl.ds(start, tile), :]
        vb = v_ref[0, pl.ds(start, tile), :]
        z = [lax.dot_general(qh[h], kb, (((1,), (1,)), ((), ())), preferred_element_type=F32) * scale
             for h in hs]
        log_beta = [_log_sigmoid(x) for x in z]
        log_1m = [log_beta[h] - z[h] for h in hs]
        if diag:
            log_1m = [jnp.where(strict, x, 0.0) for x in log_1m]
        stick = [_dot_exact_rhs(log_1m[h], later, 2) + carry_ref[h] for h in hs]
        w = [jnp.exp(log_beta[h] + stick[h]) for h in hs]
        if diag:
            w = [jnp.where(strict, x, 0.0) for x in w]
        for h in hs:
            acc_ref[h] += jnp.dot(w[h].astype(BF16), vb, preferred_element_type=F32)
            carry_ref[h] += jnp.sum(log_1m[h], axis=1, keepdims=True)

    acc_ref[...] = jnp.zeros_like(acc_ref)
    carry_ref[...] = jnp.zeros_like(carry_ref)
    block(i, True)

    def alive():
        return jnp.max(carry_ref[...]) > F32_EXP_UNDERFLOW

    def cond(c):
        return jnp.logical_and(c[0] < i, c[1])

    def body(c):
        block(i - 1 - c[0], False)
        return c[0] + 1, alive()

    lax.while_loop(cond, body, (jnp.int32(0), alive()))
    o_ref[0] = jnp.where(lane < HEAD_DIM, acc_ref[0], acc_ref[1]).astype(o_ref.dtype)


def _sb_attention(qkv, n_heads):
    b, s, _ = qkv.shape
    nblk = n_heads // HEADS_PER_LANE_BLOCK
    tile = ATTN_TILE
    assert s % tile == 0
    return pl.pallas_call(
        functools.partial(_sb_attention_kernel, tile=tile, scale=HEAD_DIM ** -0.5),
        grid=(b, nblk, s // tile),
        in_specs=[pl.BlockSpec((1, tile, LANES), lambda bi, hp, i: (bi, i, hp)),
                  pl.BlockSpec((1, s, LANES), lambda bi, hp, i: (bi, 0, nblk + hp)),
                  pl.BlockSpec((1, s, LANES), lambda bi, hp, i: (bi, 0, 2 * nblk + hp))],
        out_specs=pl.BlockSpec((1, tile, LANES), lambda bi, hp, i: (bi, i, hp)),
        out_shape=jax.ShapeDtypeStruct((b, s, n_heads * HEAD_DIM), BF16),
        scratch_shapes=[pltpu.VMEM((HEADS_PER_LANE_BLOCK, tile, LANES), F32),
                        pltpu.VMEM((HEADS_PER_LANE_BLOCK, tile, 1), F32)],
        compiler_params=_compiler_params(("arbitrary", "arbitrary", "arbitrary")),
        name="sb_attention",
    )(qkv, qkv, qkv)


def _even_out_kernel(sb_ref, pool_ref, prev_ref, pw_ref, ps_ref, wo_ref, x_ref, g_ref, o_ref,
                     *, tm, tiles_per_seq):
    i = pl.program_id(0)
    t0 = (i % tiles_per_seq) * tm
    t = t0 + lax.broadcasted_iota(I32, (tm, 1), 0)
    s = t0 - tm + lax.broadcasted_iota(I32, (1, 2 * tm), 1)
    chunk_end = ((t // CHUNK) + 1) * CHUNK
    group = LANES
    acc = jnp.dot(sb_ref[...], wo_ref[0:4 * group, :], preferred_element_type=F32)
    for g, w in enumerate(POOL_WINDOWS):
        lo = jnp.maximum(t - w // 2, 0)
        hi = jnp.minimum(t + (w - w // 2), chunk_end)
        band = ((s >= lo) & (s < hi)).astype(BF16)
        cur = pool_ref[:, g * group:(g + 1) * group]
        ext = jnp.concatenate([prev_ref[:, g * group:(g + 1) * group], cur], axis=0)
        window_sum = _dot_exact_lhs(band, ext)
        pooled = window_sum / (hi - lo).astype(F32) - cur
        y = jnp.dot(pooled.astype(BF16), pw_ref[g], preferred_element_type=F32)
        y = y * ps_ref[:, g * group:(g + 1) * group]
        acc += jnp.dot(y.astype(BF16), wo_ref[(4 + g) * group:(5 + g) * group, :],
                       preferred_element_type=F32)
    o_ref[...] = x_ref[...] + g_ref[0] * acc


def _even_out(sb2d, pool2d, seq, pool_w_bf16, pool_scale, w_out_bf16, x2d, gate):
    t, d = x2d.shape
    tm = TOKEN_TILE
    tiles_per_seq = seq // tm
    assert seq % CHUNK == 0 and len(POOL_WINDOWS) * LANES == pool2d.shape[1]
    pw = pool2d.shape[1]
    return pl.pallas_call(
        functools.partial(_even_out_kernel, tm=tm, tiles_per_seq=tiles_per_seq),
        grid=(t // tm,),
        in_specs=[pl.BlockSpec((tm, sb2d.shape[1]), lambda i: (i, 0)),
                  pl.BlockSpec((tm, pw), lambda i: (i, 0)),
                  pl.BlockSpec((tm, pw), lambda i: (jnp.maximum(i - 1, 0), 0)),
                  pl.BlockSpec(pool_w_bf16.shape, lambda i: (0, 0, 0)),
                  pl.BlockSpec((1, pw), lambda i: (0, 0)),
                  pl.BlockSpec((d, d), lambda i: (0, 0)),
                  pl.BlockSpec((tm, d), lambda i: (i, 0)),
                  pl.BlockSpec((1, 1, d), lambda i: (i // tiles_per_seq, 0, 0))],
        out_specs=pl.BlockSpec((tm, d), lambda i: (i, 0)),
        out_shape=jax.ShapeDtypeStruct((t, d), F32),
        compiler_params=_compiler_params(("arbitrary",)),
        name="even_out",
    )(sb2d, pool2d, pool2d, pool_w_bf16, pool_scale.reshape(1, pw), w_out_bf16, x2d, gate)


FOX_BIAS_TERMS = 3


def _fox_layout_constants(width, qk_bound):
    heads = width // HEAD_DIM
    nt = FOX_BIAS_TERMS
    wide = heads * LANES
    src = lax.broadcasted_iota(I32, (width, wide), 0)
    dst = lax.broadcasted_iota(I32, (width, wide), 1)
    place = ((dst // LANES == src // HEAD_DIM) & (dst % LANES == src % HEAD_DIM)).astype(BF16)
    src = lax.broadcasted_iota(I32, (nt * LANES, wide), 0)
    dst = lax.broadcasted_iota(I32, (nt * LANES, wide), 1)
    term, head = src // LANES, src % LANES
    place_dq = ((dst // LANES == head) & (dst % LANES == HEAD_DIM + term)).astype(BF16)
    place_dk = ((dst // LANES == head) & (dst % LANES == HEAD_DIM + nt + term)).astype(BF16)
    lane = lax.broadcasted_iota(I32, (1, wide), 1) % LANES
    bound_terms = _split_bf16(-qk_bound.astype(F32).reshape(1, 1), nt)
    q_const = jnp.where((lane >= HEAD_DIM + nt) & (lane < HEAD_DIM + 2 * nt), 1.0, 0.0)
    for k, term_k in enumerate(bound_terms):
        q_const = jnp.where(lane == HEAD_DIM + 2 * nt + k, term_k.astype(F32), q_const)
    k_const = jnp.where(((lane >= HEAD_DIM) & (lane < HEAD_DIM + nt))
                        | ((lane >= HEAD_DIM + 2 * nt) & (lane < HEAD_DIM + 3 * nt)), 1.0, 0.0)
    v_const = jnp.where(lane == HEAD_DIM, 1.0, 0.0)
    return place, place_dq, place_dk, q_const.astype(F32), k_const.astype(F32), v_const.astype(F32)


def _odd_prep_kernel(qk_ref, vin_ref, f_ref, rw_ref, rwprev_ref, hs_ref, tri_ref, ctri_ref,
                     place_ref, pdq_ref, pdk_ref, qc_ref, kc_ref, vc_ref,
                     qg_ref, kg_ref, fb_ref, mix_ref, w0_ref, w2_ref, a0_ref, a2_ref, g2_ref,
                     kkp_ref, ka_ref, rk_ref,
                     qa_ref, ka_out_ref, va_ref, drow_ref, rt_ref, kt_ref, at_ref, bt_ref, v_ref,
                     cw_ref, bonus_ref, g_ref, carry_ref, *, tm, tiles_per_seq, width):
    i = pl.program_id(0)
    first = (i % tiles_per_seq) == 0
    hs = hs_ref[...]

    def head_sum(x):
        return _dot_exact_rhs(x, hs, 2)

    inv_hd = 1.0 / HEAD_DIM
    q = qk_ref[:, :width]
    k = qk_ref[:, width:]
    qn = q * lax.rsqrt(head_sum(q * q) * inv_hd + NORM_EPS) * qg_ref[...] * (HEAD_DIM ** -0.5)
    kn = k * lax.rsqrt(head_sum(k * k) * inv_hd + NORM_EPS) * kg_ref[...]

    log_f = _log_sigmoid(f_ref[...] + fb_ref[...])

    @pl.when(first)
    def _():
        carry_ref[...] = jnp.zeros_like(carry_ref)

    dcum = _dot_exact_lhs(tri_ref[...], log_f) + carry_ref[...]
    carry_ref[...] = dcum[tm - 1:tm, :]
    drow_ref[0, 0] = jnp.transpose(dcum)[:SUBLANES, :]

    place = place_ref[...]
    d_terms = jnp.concatenate(_split_bf16(dcum, FOX_BIAS_TERMS), axis=1)
    qa_ref[...] = (jnp.dot(qn.astype(BF16), place, preferred_element_type=F32)
                   + jnp.dot(d_terms, pdq_ref[...], preferred_element_type=F32) + qc_ref[...]).astype(BF16)
    ka_out_ref[...] = (jnp.dot(kn.astype(BF16), place, preferred_element_type=F32)
                       + jnp.dot(-d_terms, pdk_ref[...], preferred_element_type=F32) + kc_ref[...]).astype(BF16)
    va_ref[...] = (jnp.dot(vin_ref[...], place, preferred_element_type=F32) + vc_ref[...]).astype(BF16)

    pd = rw_ref[...]
    prev_row = jnp.where(first, 0.0, rwprev_ref[SUBLANES - 1:SUBLANES, :])
    row0 = lax.broadcasted_iota(I32, (tm, 1), 0) == 0
    prev = jnp.where(row0, prev_row, pltpu.roll(pd, 1, axis=0))
    pd = pd + (prev - pd) * mix_ref[...]
    r = pd[:, :width]
    kr = pd[:, width:2 * width]
    vr = pd[:, 2 * width:3 * width]
    lora_in = pd[:, 3 * width:3 * width + LANES]
    xg = pd[:, 3 * width + LANES:3 * width + 2 * LANES]
    w_log = _log_sigmoid(w0_ref[...] + jnp.dot(jnp.tanh(lora_in).astype(BF16), w2_ref[...],
                                              preferred_element_type=F32)) - 0.5
    log_w = -jnp.exp(w_log)
    a = _sigmoid(a0_ref[...] + jnp.dot(lora_in.astype(BF16), a2_ref[...], preferred_element_type=F32))
    g_ref[...] = jnp.dot(_sigmoid(xg).astype(BF16), g2_ref[...], preferred_element_type=F32)
    kk = kr * kkp_ref[...]
    kk = kk / jnp.maximum(jnp.sqrt(head_sum(kk * kk)), 1e-12)
    k2 = kr * (1.0 + (a - 1.0) * ka_ref[...])
    bonus_ref[...] = head_sum(r * k2 * rk_ref[...]) * vr
    cw = _dot_exact_lhs(ctri_ref[...], log_w)
    grow = jnp.exp(-cw)
    rt_ref[...] = r * jnp.exp(cw)
    kt_ref[...] = k2 * grow
    bt_ref[...] = kk * a * grow
    at_ref[...] = -kk * jnp.exp(cw - log_w)
    v_ref[...] = vr
    cw_ref[...] = cw


def _odd_prep(qk, v, fproj, rw, seq, p, qk_bound):
    t = qk.shape[0]
    width = qk.shape[1] // 2
    tm = TOKEN_TILE
    tiles_per_seq = seq // tm
    nrw = rw.shape[1]
    assert tm % CHUNK == 0 and seq % tm == 0 and nrw == 3 * width + 2 * LANES
    assert HEAD_DIM + 3 * FOX_BIAS_TERMS <= LANES and width // HEAD_DIM <= SUBLANES
    ri = lax.broadcasted_iota(I32, (tm, tm), 0)
    ci = lax.broadcasted_iota(I32, (tm, tm), 1)
    tri = (ci <= ri).astype(BF16)
    ctri = ((ci <= ri) & (ci // CHUNK == ri // CHUNK)).astype(BF16)
    hs = _head_sum_matrix(width)
    row = lambda a: a.reshape(1, -1)
    full = lambda a: pl.BlockSpec(a.shape, lambda i: (0,) * a.ndim)
    tile = lambda n: pl.BlockSpec((tm, n), lambda i: (i, 0))
    consts = [hs, tri, ctri, *_fox_layout_constants(width, qk_bound),
              row(p['qg']), row(p['kg']), row(p['fb']), row(p['mix']), row(p['w0']),
              p['w2'], row(p['a0']), p['a2'], p['g2'], row(p['kk']), row(p['ka']), row(p['rk'])]
    wide = jax.ShapeDtypeStruct((t, width), F32)
    aug = (width // HEAD_DIM) * LANES
    return pl.pallas_call(
        functools.partial(_odd_prep_kernel, tm=tm, tiles_per_seq=tiles_per_seq, width=width),
        grid=(t // tm,),
        in_specs=[tile(2 * width), tile(width), tile(LANES), tile(nrw),
                  pl.BlockSpec((SUBLANES, nrw), lambda i: (jnp.maximum(i * (tm // SUBLANES) - 1, 0), 0))]
                 + [full(c) for c in consts],
        out_specs=[tile(aug), tile(aug), tile(aug),
                   pl.BlockSpec((1, 1, SUBLANES, tm), lambda i: (i // tiles_per_seq, i % tiles_per_seq, 0, 0))]
                  + [tile(width)] * 8,
        out_shape=[jax.ShapeDtypeStruct((t, aug), BF16)] * 3
                  + [jax.ShapeDtypeStruct((t // seq, tiles_per_seq, SUBLANES, tm), F32)] + [wide] * 8,
        scratch_shapes=[pltpu.VMEM((1, LANES), F32)],
        compiler_params=_compiler_params(("arbitrary",)),
        name="odd_prep",
    )(qk, v, fproj, rw, rw, *consts)


def _fox_attention_kernel(q_ref, k_ref, v_ref, drow_ref, o_ref, acc_ref, m_ref, *, tile, running_max):
    hp = pl.program_id(1)
    i = pl.program_id(2)
    row = lax.broadcasted_iota(I32, (tile, tile), 0)
    col = lax.broadcasted_iota(I32, (tile, tile), 1)
    causal = col <= row
    sub = lax.broadcasted_iota(I32, (SUBLANES, tile), 0)
    hs = range(HEADS_PER_LANE_BLOCK)
    heads = [hp * HEADS_PER_LANE_BLOCK + h for h in hs]
    q = [q_ref[0, :, h * LANES:(h + 1) * LANES] for h in hs]

    def d_row(j, h):
        return jnp.sum(jnp.where(sub == heads[h], drow_ref[0, j], 0.0), axis=0, keepdims=True)

    def blocks(js, diag):
        work = [(pl.multiple_of(j * tile, tile), h) for j in js for h in hs]
        s = [lax.dot_general(q[h], k_ref[0, pl.ds(start, tile), h * LANES:(h + 1) * LANES],
                             (((1,), (1,)), ((), ())), preferred_element_type=F32) for start, h in work]
        if diag:
            s = [jnp.where(causal, x, -jnp.inf) for x in s]
        if running_max:
            m_prev = [m_ref[h] for h in hs]
            m_new = [functools.reduce(jnp.maximum, [m_prev[h]] + [jnp.max(s[n], axis=1, keepdims=True)
                                                                 for n, (_, hh) in enumerate(work) if hh == h])
                     for h in hs]
            p = [jnp.exp(s[n] - m_new[h]) for n, (_, h) in enumerate(work)]
        else:
            p = [jnp.exp(x) for x in s]
        pv = [jnp.dot(p[n].astype(BF16), v_ref[0, pl.ds(start, tile), h * LANES:(h + 1) * LANES],
                      preferred_element_type=F32) for n, (start, h) in enumerate(work)]
        for h in hs:
            total = functools.reduce(jnp.add, [pv[n] for n, (_, hh) in enumerate(work) if hh == h])
            if running_max:
                acc_ref[h] = jnp.exp(m_prev[h] - m_new[h]) * acc_ref[h] + total
                m_ref[h] = m_new[h]
            else:
                acc_ref[h] += total

    acc_ref[...] = jnp.zeros_like(acc_ref)
    m_ref[...] = jnp.full_like(m_ref, -jnp.inf)
    blocks([i], True)

    d_top = [jnp.max(d_row(i, h)) for h in hs]

    def alive(j):
        jc = jnp.maximum(j, 0)
        gaps = [d_top[h] - jnp.min(d_row(jc, h)) - (jnp.min(m_ref[h]) if running_max else 0.0) for h in hs]
        return functools.reduce(jnp.maximum, gaps) > F32_EXP_UNDERFLOW

    def cond(c):
        return jnp.logical_and(c[0] + 1 < i, c[1])

    def body(c):
        j = i - 1 - c[0]
        blocks([j, j - 1], False)
        return c[0] + 2, alive(j - 2)

    done, live = lax.while_loop(cond, body, (jnp.int32(0), alive(i - 1)))

    @pl.when(jnp.logical_and(done < i, live))
    def _():
        blocks([i - 1 - done], False)

    outs = [acc_ref[h][:, :HEAD_DIM] / acc_ref[h][:, HEAD_DIM:HEAD_DIM + 1] for h in hs]
    o_ref[0] = jnp.concatenate(outs, axis=1)


def _fox_qk_bound(q_gain, k_gain):
    bf16_slack = (1.0 + 2.0 ** -8) ** 2
    return (HEAD_DIM * HEAD_DIM ** -0.5 * bf16_slack) * jnp.max(jnp.abs(q_gain)) * jnp.max(jnp.abs(k_gain))


FOX_FIXED_OFFSET_MAX = 40.0


def _fox_attention(q_aug, k_aug, v_aug, drow, qk_bound):
    b, s, w = q_aug.shape
    nblk = w // (HEADS_PER_LANE_BLOCK * LANES)
    tile = ATTN_TILE
    assert drow.shape == (b, s // tile, SUBLANES, tile)
    pair = HEADS_PER_LANE_BLOCK * LANES

    def call(running_max):
        return pl.pallas_call(
            functools.partial(_fox_attention_kernel, tile=tile, running_max=running_max),
            grid=(b, nblk, s // tile),
            in_specs=[pl.BlockSpec((1, tile, pair), lambda bi, hp, i: (bi, i, hp)),
                      pl.BlockSpec((1, s, pair), lambda bi, hp, i: (bi, 0, hp)),
                      pl.BlockSpec((1, s, pair), lambda bi, hp, i: (bi, 0, hp)),
                      pl.BlockSpec((1, s // tile, SUBLANES, tile), lambda bi, hp, i: (bi, 0, 0, 0))],
            out_specs=pl.BlockSpec((1, tile, LANES), lambda bi, hp, i: (bi, i, hp)),
            out_shape=jax.ShapeDtypeStruct((b, s, nblk * LANES), F32),
            scratch_shapes=[pltpu.VMEM((HEADS_PER_LANE_BLOCK, tile, LANES), F32),
                            pltpu.VMEM((HEADS_PER_LANE_BLOCK, tile, 1), F32)],
            compiler_params=_compiler_params(("arbitrary", "arbitrary", "arbitrary")),
            name="fox_attention_rescaled" if running_max else "fox_attention",
        )(q_aug, k_aug, v_aug, drow)

    return lax.cond(qk_bound <= FOX_FIXED_OFFSET_MAX, lambda: call(False), lambda: call(True))


_NN = (((1,), (0,)), ((), ()))
_NT = (((1,), (1,)), ((), ()))
_TN = (((0,), (0,)), ((), ()))


def _mm(a, b, dims=_NN):
    a_hi, a_lo = _split_bf16(a, 2)
    b_hi, b_lo = _split_bf16(b, 2)
    dot = lambda x, y: lax.dot_general(x, y, dims, preferred_element_type=F32)
    return dot(a_hi, b_hi) + (dot(a_hi, b_lo) + dot(a_lo, b_hi))


def _head_cols(h):
    return slice(h * HEAD_DIM, (h + 1) * HEAD_DIM)


def _rwkv_intra_kernel(rt_ref, kt_ref, at_ref, bt_ref, v_ref, inv_ref, arb_ref, fvu_ref, fvy_ref,
                       *, chunks, heads):
    c = CHUNK
    ri = lax.broadcasted_iota(I32, (2 * c, 2 * c), 0)
    ci = lax.broadcasted_iota(I32, (2 * c, 2 * c), 1)
    keep = (ci % c) < jnp.where(ri < c, ri, ri - c + 1)
    eye = (lax.broadcasted_iota(I32, (c, c), 0) == lax.broadcasted_iota(I32, (c, c), 1)).astype(F32)
    hs = range(heads)

    def chunk(cidx, carry):
        rows = pl.ds(pl.multiple_of(cidx * c, c), c)
        a_all = [jnp.where(keep,
                           _mm(jnp.concatenate([at_ref[rows, _head_cols(h)], rt_ref[rows, _head_cols(h)]], axis=0),
                               jnp.concatenate([bt_ref[rows, _head_cols(h)], kt_ref[rows, _head_cols(h)]], axis=0),
                               _NT), 0.0) for h in hs]
        a_ab = [a[:c, :c] for a in a_all]
        inv = [eye + a for a in a_ab]
        power = [_mm(a, a) for a in a_ab]
        for _ in range(4):
            both = [_mm(jnp.concatenate([inv[h], power[h]], axis=0), power[h]) for h in hs]
            inv = [inv[h] + both[h][:c] for h in hs]
            power = [both[h][c:] for h in hs]
        inv = [inv[h] + _mm(inv[h], power[h]) for h in hs]
        from_v = [_mm(a_all[h][:, c:], v_ref[rows, _head_cols(h)]) for h in hs]
        inv_ref[rows, :] = jnp.concatenate(inv, axis=1)
        arb_ref[rows, :] = jnp.concatenate([a[c:, :c] for a in a_all], axis=1)
        fvu_ref[rows, :] = jnp.concatenate([f[:c] for f in from_v], axis=1)
        fvy_ref[rows, :] = jnp.concatenate([f[c:] for f in from_v], axis=1)
        return carry

    lax.fori_loop(0, chunks, chunk, 0)


def _rwkv_scan_kernel(rt_ref, kt_ref, at_ref, bt_ref, v_ref, cw_ref, inv_ref, arb_ref, fvu_ref, fvy_ref,
                      y_ref, st_ref, *, steps_per_seq, chunks, heads):
    i = pl.program_id(0)

    @pl.when(i % steps_per_seq == 0)
    def _():
        st_ref[...] = jnp.zeros_like(st_ref)

    c = CHUNK
    hs = range(heads)

    def chunk(cidx, carry):
        rows = pl.ds(pl.multiple_of(cidx * c, c), c)
        last = pl.ds(pl.multiple_of(cidx * c, c) + (c - SUBLANES), SUBLANES)
        ar = [jnp.concatenate([at_ref[rows, _head_cols(h)], rt_ref[rows, _head_cols(h)]], axis=0) for h in hs]
        bk = [jnp.concatenate([bt_ref[rows, _head_cols(h)], kt_ref[rows, _head_cols(h)]], axis=0) for h in hs]
        st = [st_ref[h] for h in hs]
        from_state = [_mm(ar[h], st[h], _NT) for h in hs]
        u = [_mm(inv_ref[rows, _head_cols(h)], from_state[h][:c] + fvu_ref[rows, _head_cols(h)]) for h in hs]
        y = [from_state[h][c:] + fvy_ref[rows, _head_cols(h)] + _mm(arb_ref[rows, _head_cols(h)], u[h])
             for h in hs]
        for h in hs:
            decay = jnp.exp(cw_ref[last, _head_cols(h)][SUBLANES - 1:SUBLANES, :])
            uv = jnp.concatenate([u[h], v_ref[rows, _head_cols(h)]], axis=0)
            st_ref[h] = (st[h] + _mm(uv, bk[h], _TN)) * decay
        y_ref[rows, :] = jnp.concatenate(y, axis=1)
        return carry

    lax.fori_loop(0, chunks, chunk, 0)


def _rwkv_scan(rt, kt, at, bt, v, cw, seq):
    t, w = rt.shape
    heads = w // HEAD_DIM
    ts = 2 * TOKEN_TILE
    assert seq % ts == 0
    spec = pl.BlockSpec((ts, w), lambda i: (i, 0))
    wide = jax.ShapeDtypeStruct((t, w), F32)
    inv, arb, fvu, fvy = pl.pallas_call(
        functools.partial(_rwkv_intra_kernel, chunks=ts // CHUNK, heads=heads),
        grid=(t // ts,),
        in_specs=[spec] * 5,
        out_specs=[spec] * 4,
        out_shape=[wide] * 4,
        compiler_params=_compiler_params(("arbitrary",)),
        name="rwkv_intra",
    )(rt, kt, at, bt, v)
    return pl.pallas_call(
        functools.partial(_rwkv_scan_kernel, steps_per_seq=seq // ts, chunks=ts // CHUNK, heads=heads),
        grid=(t // ts,),
        in_specs=[spec] * 10,
        out_specs=spec,
        out_shape=wide,
        scratch_shapes=[pltpu.VMEM((heads, HEAD_DIM, HEAD_DIM), F32)],
        compiler_params=_compiler_params(("arbitrary",)),
        name="rwkv_scan",
    )(rt, kt, at, bt, v, cw, inv, arb, fvu, fvy)


def _odd_out_kernel(att_ref, og_ref, y_ref, bonus_ref, g_ref, hs_ref, lnw_ref, lnb_ref, wo_ref,
                    x_ref, gate_ref, o_ref, *, width):
    hs = hs_ref[...]
    inv_hd = 1.0 / HEAD_DIM
    fox = att_ref[...] * _sigmoid(og_ref[...])
    y = y_ref[...]
    mu = _dot_exact_rhs(y, hs, 2) * inv_hd
    yc = y - mu
    var = _dot_exact_rhs(yc * yc, hs, 2) * inv_hd
    yn = yc * lax.rsqrt(var + RWKV_GN_EPS) * lnw_ref[...] + lnb_ref[...]
    rw = (yn + bonus_ref[...]) * g_ref[...]
    acc = jnp.dot(fox.astype(BF16), wo_ref[:width, :], preferred_element_type=F32)
    acc += jnp.dot(rw.astype(BF16), wo_ref[width:, :], preferred_element_type=F32)
    o_ref[...] = x_ref[...] + gate_ref[0] * acc


def _odd_out(att, og, y, bonus, g, ln_w, ln_b, w_out_bf16, x2d, seq, gate):
    t, d = x2d.shape
    width = att.shape[1]
    tm = TOKEN_TILE
    tiles_per_seq = seq // tm
    hs = _head_sum_matrix(width)
    tile = lambda n: pl.BlockSpec((tm, n), lambda i: (i, 0))
    return pl.pallas_call(
        functools.partial(_odd_out_kernel, width=width),
        grid=(t // tm,),
        in_specs=[tile(width)] * 5
                 + [pl.BlockSpec(hs.shape, lambda i: (0, 0)),
                    pl.BlockSpec((1, width), lambda i: (0, 0)), pl.BlockSpec((1, width), lambda i: (0, 0)),
                    pl.BlockSpec((d, d), lambda i: (0, 0)), tile(d),
                    pl.BlockSpec((1, 1, d), lambda i: (i // tiles_per_seq, 0, 0))],
        out_specs=tile(d),
        out_shape=jax.ShapeDtypeStruct((t, d), F32),
        compiler_params=_compiler_params(("arbitrary",)),
        name="odd_out",
    )(att, og, y, bonus, g, hs, ln_w.reshape(1, width), ln_b.reshape(1, width), w_out_bf16, x2d, gate)


def _top_rows(s, k, prio, payload=None):
    big = jnp.int32(2 ** 30)
    vals, picks = [], []
    for _ in range(k):
        m = jnp.max(s, axis=0, keepdims=True)
        am = jnp.min(jnp.where(s == m, prio, big), axis=0, keepdims=True)
        hit = prio == am
        vals.append(m)
        if payload is None:
            picks.append(am)
        else:
            picks.append(jnp.max(jnp.where(hit, payload, -1), axis=0, keepdims=True))
        s = jnp.where(hit, -jnp.inf, s)
    return jnp.concatenate(vals, axis=0), jnp.concatenate(picks, axis=0)


def _staircase_groups():
    k, g = PEER_TOPK, SUBLANES
    groups = []
    for b in range(k):
        count = k // (b + 1)
        if count <= 1:
            break
        for a0 in range(0, count, g):
            groups.append((a0, b, False, min(g, count - a0)))
    first_single = next(b for b in range(k) if k // (b + 1) <= 1)
    for b0 in range(first_single, k, g):
        groups.append((0, b0, True, min(g, k - b0)))
    return groups


def _topk_constants(tt):
    sub = lax.broadcasted_iota(I32, (SUBLANES, 1), 0)
    groups = _staircase_groups()
    prio = jnp.concatenate(
        [jnp.broadcast_to((sub * 0 + a0) * PEER_TOPK + b0 + sub if along_b
                          else (a0 + sub) * PEER_TOPK + b0, (SUBLANES, tt))
         for a0, b0, along_b, _ in groups], axis=0)
    return lax.broadcasted_iota(I32, (PEER_NKEYS, tt), 0), sub, groups, prio


def _topk_head(q_head, k1, k2, consts):
    key_id, sub, groups, prio = consts
    half = PEER_NKEYS
    nt = (((1,), (1,)), ((), ()))
    s1 = lax.dot_general(k1, q_head[:, :half], nt, precision=HIGHEST, preferred_element_type=F32)
    s2 = lax.dot_general(k2, q_head[:, half:], nt, precision=HIGHEST, preferred_element_type=F32)
    v1, i1 = _top_rows(s1, PEER_TOPK, key_id)
    v2, i2 = _top_rows(s2, PEER_TOPK, key_id)
    cand, cand_idx = [], []
    for a0, b0, along_b, valid in groups:
        if along_b:
            val = v1[a0:a0 + 1] + v2[b0:b0 + SUBLANES]
            eid = i1[a0:a0 + 1] * PEER_NKEYS + i2[b0:b0 + SUBLANES]
        else:
            val = v1[a0:a0 + SUBLANES] + v2[b0:b0 + 1]
            eid = i1[a0:a0 + SUBLANES] * PEER_NKEYS + i2[b0:b0 + 1]
        cand.append(jnp.where(sub < valid, val, -jnp.inf))
        cand_idx.append(eid)
    top_s, idx = _top_rows(jnp.concatenate(cand, axis=0), PEER_TOPK, prio,
                           payload=jnp.concatenate(cand_idx, axis=0))
    e = jnp.exp(top_s - top_s[0:1])
    return idx, e / jnp.sum(e, axis=0, keepdims=True)


def _pack_table_kernel(t_ref, o_ref, *, pairs):
    high = jnp.int32(-65536)
    for j in range(pairs):
        lo = t_ref[:, (2 * j) * LANES:(2 * j + 1) * LANES].astype(BF16).astype(F32)
        hi = t_ref[:, (2 * j + 1) * LANES:(2 * j + 2) * LANES].astype(BF16).astype(F32)
        lo_bits = lax.shift_right_logical(pltpu.bitcast(lo, I32), 16)
        words = lax.bitwise_or(lo_bits, lax.bitwise_and(pltpu.bitcast(hi, I32), high))
        o_ref[pl.ds(j, t_ref.shape[0], stride=pairs), :] = words


def _pack_table(table):
    e, d = table.shape
    pairs = d // (2 * LANES)
    rows = 2 * TOKEN_TILE
    return pl.pallas_call(
        functools.partial(_pack_table_kernel, pairs=pairs),
        grid=(e // rows,),
        in_specs=[pl.BlockSpec((rows, d), lambda i: (i, 0))],
        out_specs=pl.BlockSpec((rows * pairs, LANES), lambda i: (i, 0)),
        out_shape=jax.ShapeDtypeStruct((e * pairs, LANES), I32),
        compiler_params=_compiler_params(("arbitrary",)),
        name="pack_table",
    )(table)


def _unpack_words(w):
    lo = pltpu.bitcast(lax.shift_left(w, 16), F32)
    hi = pltpu.bitcast(lax.bitwise_and(w, jnp.int32(-65536)), F32)
    return lo, hi


def _gather_rows(idx_ref, tab_ref, tile_refs, toks, pairs):
    tok_idx = [idx_ref.at[tok] for tok in toks]
    for m in range(PEER_SLOTS):
        for rows, tile_ref in zip(tok_idx, tile_refs):
            row = pl.multiple_of(rows[m], pairs)
            tile_ref[m * pairs:(m + 1) * pairs, :] = tab_ref[pl.ds(row, pairs), :]


def _tile_chunk(tile_ref, j, pairs):
    return _unpack_words(tile_ref[pl.ds(j, PEER_SLOTS, stride=pairs), :])


PEER_TILE_BUFFERS = 8


def _pipelined_tokens(ntok, gather, compute, init, trips_per_iter=1, per_iter=None):
    nb = PEER_TILE_BUFFERS
    group = nb // 2
    assert ntok % (nb * trips_per_iter) == 0
    gather(list(range(group)), list(range(group)))

    def trip(p, carry):
        tok = nb * p
        for half in range(2):
            first = tok + half * group
            ahead = [jnp.minimum(first + group + k, ntok - 1) for k in range(group)]
            gather(ahead, [((half + 1) % 2) * group + k for k in range(group)])
            for k in range(group):
                carry = compute(first + k, half * group + k, carry)
        return carry

    def iteration(i, carry):
        if per_iter is not None:
            per_iter(i)
        for r in range(trips_per_iter):
            carry = trip(i * trips_per_iter + r, carry)
        return carry

    return lax.fori_loop(0, ntok // (nb * trips_per_iter), iteration, init)


def _pick_row(block, r):
    sub = lax.broadcasted_iota(I32, block.shape, 0)
    return jnp.sum(jnp.where(sub == r, block, 0.0), axis=0, keepdims=True)


def _peer_select_act_kernel(q_ref, k1_ref, k2_ref, tab_ref, h_ref, idx_ref, coef_ref,
                            idx_smem, stage_ref, slots_ref, gate_ref, act_ref, sem, *tiles, pairs, nblk):
    n = pl.program_id(0)
    ntok = h_ref.shape[0]
    copy_idx = pltpu.make_async_copy(stage_ref, idx_smem, sem)

    @pl.when(n == 0)
    def _():
        stage_ref[...] = jnp.zeros_like(stage_ref)
        gate_ref[...] = jnp.zeros_like(gate_ref)
        copy_idx.start()

    copy_idx.wait()
    act_ref[...] = jnp.zeros_like(act_ref)
    lane = lax.broadcasted_iota(I32, (PEER_SLOTS, LANES), 1)
    consts = _topk_constants(LANES)
    slot_now = n % 2

    def gather(toks, bufs):
        _gather_rows(idx_smem, tab_ref, [tiles[b] for b in bufs], toks, pairs)

    def compute(tok, buf, carry):
        base = pl.multiple_of((tok // SUBLANES) * SUBLANES, SUBLANES)
        h_row = _pick_row(h_ref[pl.ds(base, SUBLANES), :], tok % SUBLANES)
        terms = []
        for j in range(pairs):
            lo, hi = _tile_chunk(tiles[buf], j, pairs)
            h_lo = h_row[:, (2 * j) * LANES:(2 * j + 1) * LANES]
            h_hi = h_row[:, (2 * j + 1) * LANES:(2 * j + 2) * LANES]
            terms.append(lo * h_lo + hi * h_hi)
        while len(terms) > 1:
            terms = [a + b for a, b in zip(terms[::2], terms[1::2])]
        col = jnp.sum(terms[0], axis=1, keepdims=True)
        part = tok // LANES
        act_ref[part] = jnp.where(lane == tok % LANES, col, act_ref[part])
        return carry

    parts = ntok // LANES
    heads_per_iter = 2

    def select(i):
        for k in range(heads_per_iter):
            hd = i * heads_per_iter + k
            rows = pl.ds(pl.multiple_of(hd * PEER_TOPK, PEER_TOPK), PEER_TOPK)
            for part in range(parts):
                toks = slice(part * LANES, (part + 1) * LANES)
                idx, gate = _topk_head(q_ref[hd, toks, :], k1_ref[...], k2_ref[...], consts)
                slots_ref[part, rows, :] = idx
                gate_ref[slot_now, part, rows, :] = gate

    heads = q_ref.shape[0]
    _pipelined_tokens(ntok, gather, compute, 0,
                      trips_per_iter=ntok * heads_per_iter // (PEER_TILE_BUFFERS * heads), per_iter=select)

    offsets = jnp.concatenate([jnp.transpose(slots_ref[p]) for p in range(parts)], axis=0) * pairs
    idx_ref[...] = offsets
    stage_ref[...] = offsets

    @pl.when(n < nblk)
    def _():
        copy_idx.start()

    act_t = jnp.concatenate([act_ref[p] for p in range(parts)], axis=1)
    gate_prev = jnp.concatenate([gate_ref[1 - slot_now, p] for p in range(parts)], axis=1)
    gelu = 0.5 * act_t * (1.0 + lax.erf(act_t * (2.0 ** -0.5)))
    coef_ref[...] = gate_prev * gelu


def _peer_select_act(q_heads, k1, k2, table_words, h2d):
    heads, t, _ = q_heads.shape
    d = h2d.shape[1]
    pairs = d // (2 * LANES)
    tb = TOKEN_TILE
    nblk = t // tb
    assert heads == PEER_HEADS and tb % (PEER_TILE_BUFFERS * heads) == 0 and tb % LANES == 0
    cur = lambda n: jnp.minimum(n, nblk - 1)
    prev = lambda n: jnp.maximum(n - 1, 0)
    return pl.pallas_call(
        functools.partial(_peer_select_act_kernel, pairs=pairs, nblk=nblk),
        grid=(nblk + 1,),
        in_specs=[pl.BlockSpec((heads, tb, q_heads.shape[2]), lambda n: (0, cur(n), 0)),
                  pl.BlockSpec(k1.shape, lambda n: (0, 0)),
                  pl.BlockSpec(k2.shape, lambda n: (0, 0)),
                  pl.BlockSpec(table_words.shape, lambda n: (0, 0), pipeline_mode=pl.Buffered(1)),
                  pl.BlockSpec((tb, d), lambda n: (prev(n), 0))],
        out_specs=[pl.BlockSpec((tb, PEER_SLOTS), lambda n: (cur(n), 0)),
                   pl.BlockSpec((PEER_SLOTS, tb), lambda n: (0, prev(n)))],
        out_shape=[jax.ShapeDtypeStruct((t, PEER_SLOTS), I32),
                   jax.ShapeDtypeStruct((PEER_SLOTS, t), F32)],
        scratch_shapes=[pltpu.SMEM((tb, PEER_SLOTS), I32),
                        pltpu.VMEM((tb, PEER_SLOTS), I32),
                        pltpu.VMEM((tb // LANES, PEER_SLOTS, LANES), I32),
                        pltpu.VMEM((2, tb // LANES, PEER_SLOTS, LANES), F32),
                        pltpu.VMEM((tb // LANES, PEER_SLOTS, LANES), F32),
                        pltpu.SemaphoreType.DMA]
                       + [pltpu.VMEM((pairs * PEER_SLOTS, LANES), I32)] * PEER_TILE_BUFFERS,
        compiler_params=_compiler_params(("arbitrary",), vmem_mb=56),
        name="peer_select_act",
    )(q_heads, k1, k2, table_words, h2d)


def _peer_out_kernel(idx_ref, tab_ref, coef_ref, x_ref, g_ref, o_ref, *tiles, pairs):
    ntok, d = x_ref.shape
    lane = lax.broadcasted_iota(I32, (PEER_SLOTS, ntok), 1)
    sub = lax.broadcasted_iota(I32, (SUBLANES, d), 0)
    groups = PEER_SLOTS // SUBLANES

    def gather(toks, bufs):
        _gather_rows(idx_ref, tab_ref, [tiles[b] for b in bufs], toks, pairs)

    def compute(tok, buf, rows):
        coef = jnp.sum(jnp.where(lane == tok, coef_ref[...], 0.0), axis=1, keepdims=True)
        pieces = []
        for j in range(pairs):
            for vals in _tile_chunk(tiles[buf], j, pairs):
                prod = (vals * coef).reshape(groups, SUBLANES, LANES)
                pieces.append(jnp.sum(jnp.sum(prod, axis=0), axis=0, keepdims=True))
        row = jnp.concatenate(pieces, axis=1)
        rows = jnp.where(sub == tok % SUBLANES, row, rows)
        base = pl.multiple_of((tok // SUBLANES) * SUBLANES, SUBLANES)
        o_ref[pl.ds(base, SUBLANES), :] = x_ref[pl.ds(base, SUBLANES), :] + g_ref[0] * rows
        return rows

    _pipelined_tokens(ntok, gather, compute, jnp.zeros((SUBLANES, d), F32))


def _peer_out(idx, table_words, coef_t, x2d, seq, gate):
    t, d = x2d.shape
    pairs = d // (2 * LANES)
    nt = PEER_TOKENS
    per_seq = seq // nt
    return pl.pallas_call(
        functools.partial(_peer_out_kernel, pairs=pairs),
        grid=(t // nt,),
        in_specs=[pl.BlockSpec((nt, PEER_SLOTS), lambda i: (i, 0), memory_space=pltpu.SMEM),
                  pl.BlockSpec(table_words.shape, lambda i: (0, 0), pipeline_mode=pl.Buffered(1)),
                  pl.BlockSpec((PEER_SLOTS, nt), lambda i: (0, i)),
                  pl.BlockSpec((nt, d), lambda i: (i, 0)),
                  pl.BlockSpec((1, 1, d), lambda i: (i // per_seq, 0, 0))],
        out_specs=pl.BlockSpec((nt, d), lambda i: (i, 0)),
        out_shape=jax.ShapeDtypeStruct((t, d), F32),
        scratch_shapes=[pltpu.VMEM((pairs * PEER_SLOTS, LANES), I32)] * PEER_TILE_BUFFERS,
        compiler_params=_compiler_params(("arbitrary",), vmem_mb=56),
        name="peer_out",
    )(idx, table_words, coef_t, x2d, gate)


def _peer_ffn(x2d, seq, gain, shift, scale, gate, wq_bf16, k1, k2, u_words, v_words):
    n = wq_bf16.shape[1]
    q_heads, h = _norm_proj(x2d, seq, gain, shift, scale, wq_bf16, [(0, n)], [F32], emit_h=True,
                            head_major=PEER_HEADS)
    idx, coef_t = _peer_select_act(q_heads, k1, k2, u_words, h)
    return _peer_out(idx, v_words, coef_t, x2d, seq, gate)


def _final_norm_kernel(x_ref, gain_ref, o_ref):
    x = x_ref[...]
    ms = jnp.mean(x * x, axis=-1, keepdims=True)
    o_ref[...] = x * lax.rsqrt(ms + NORM_EPS) * gain_ref[...]


def _final_norm(x2d, gain):
    t, d = x2d.shape
    tm = TOKEN_TILE
    return pl.pallas_call(
        _final_norm_kernel,
        grid=(t // tm,),
        in_specs=[pl.BlockSpec((tm, d), lambda i: (i, 0)), pl.BlockSpec((1, d), lambda i: (0, 0))],
        out_specs=pl.BlockSpec((tm, d), lambda i: (i, 0)),
        out_shape=jax.ShapeDtypeStruct((t, d), F32),
        compiler_params=_compiler_params(("arbitrary",)),
        name="final_norm",
    )(x2d, gain.reshape(1, d))


def _even_mixer(x2d, b, seq, gain, shift, scale, gate, w_in, w_out, pool_w, pool_scale):
    d = x2d.shape[1]
    pool_width = len(POOL_WINDOWS) * LANES
    sb_width = (w_in.shape[1] - pool_width) // 3
    qkv, pool = _norm_proj(x2d, seq, gain, shift, scale, w_in.astype(BF16),
                           [(0, 3 * sb_width), (3 * sb_width, pool_width)], [BF16, F32])
    sb = _sb_attention(qkv.reshape(b, seq, 3 * sb_width), sb_width // HEAD_DIM)
    return _even_out(sb.reshape(b * seq, sb_width), pool, seq, pool_w.astype(BF16), pool_scale,
                     w_out.astype(BF16), x2d, gate)


def _odd_mixer(x2d, b, seq, gain, shift, scale, gate, w_in, w_out, fox_qnorm, fox_knorm, fox_fbias,
               rwkv_mix, rwkv_w0, rwkv_w2, rwkv_a0, rwkv_a2, rwkv_g2, rwkv_kk, rwkv_ka, rwkv_rk,
               rwkv_ln_w, rwkv_ln_b):
    d = x2d.shape[1]
    width = rwkv_w0.shape[0]
    heads = width // HEAD_DIM
    fox_in = 4 * width + heads
    n_rw = w_in.shape[1] - fox_in
    w_cols = jnp.concatenate([w_in[:, :4 * width], w_in[:, fox_in:], w_in[:, 4 * width:fox_in],
                              jnp.zeros((d, LANES - heads), w_in.dtype)], axis=1).astype(BF16)
    qk, v, og, rw, fproj = _norm_proj(
        x2d, seq, gain, shift, scale, w_cols,
        [(0, 2 * width), (2 * width, width), (3 * width, width), (4 * width, n_rw), (4 * width + n_rw, LANES)],
        [F32, BF16, F32, F32, F32])
    lora = rwkv_w2.shape[0]
    assert 2 * lora == LANES and rwkv_a2.shape[0] == lora and rwkv_g2.shape[0] == LANES
    zeros = jnp.zeros((lora, width), F32)
    params = dict(
        qg=jnp.tile(fox_qnorm, heads), kg=jnp.tile(fox_knorm, heads),
        fb=jnp.concatenate([fox_fbias, jnp.zeros((LANES - heads,), F32)]),
        mix=rwkv_mix, w0=rwkv_w0, a0=rwkv_a0,
        w2=jnp.concatenate([rwkv_w2, zeros], axis=0).astype(BF16),
        a2=jnp.concatenate([zeros, rwkv_a2], axis=0).astype(BF16),
        g2=rwkv_g2.astype(BF16), kk=rwkv_kk, ka=rwkv_ka, rk=rwkv_rk.reshape(-1))
    qk_bound = _fox_qk_bound(fox_qnorm, fox_knorm)
    q_aug, k_aug, v_aug, drow, rt, kt, at, bt, vv, cw, bonus, g = _odd_prep(
        qk, v, fproj, rw, seq, params, qk_bound)
    r3 = lambda a: a.reshape(b, seq, -1)
    att = _fox_attention(r3(q_aug), r3(k_aug), r3(v_aug), drow, qk_bound)
    y = _rwkv_scan(rt, kt, at, bt, vv, cw, seq)
    return _odd_out(att.reshape(b * seq, width), og, y, bonus, g, rwkv_ln_w, rwkv_ln_b,
                    w_out.astype(BF16), x2d, seq, gate)


def kernel(x, c, ada_w, ada_b, norm_mix, norm_ffn, ev_w_in, ev_w_out, pool_w, pool_scale, od_w_in, od_w_out, fox_qnorm, fox_knorm, fox_fbias, rwkv_mix, rwkv_w0, rwkv_w2, rwkv_a0, rwkv_a2, rwkv_g2, rwkv_kk, rwkv_ka, rwkv_rk, rwkv_ln_w, rwkv_ln_b, peer_wq, peer_k1, peer_k2, peer_u, peer_v, final_norm):
    b, seq, d = x.shape
    depth = ada_w.shape[0]
    assert TOKEN_TILE == ATTN_TILE and seq % (2 * TOKEN_TILE) == 0
    mod = _adaln(c, ada_w, ada_b).reshape(depth, b, 6, 1, d)
    x2d = x.reshape(b * seq, d)
    for layer in range(depth):
        sh1, sc1, g1, sh2, sc2, g2 = (mod[layer, :, i] for i in range(6))
        j = layer // 2
        if layer % 2 == 0:
            x2d = _even_mixer(x2d, b, seq, norm_mix[layer], sh1, sc1, g1, ev_w_in[j], ev_w_out[j],
                              pool_w[j], pool_scale[j])
        else:
            x2d = _odd_mixer(x2d, b, seq, norm_mix[layer], sh1, sc1, g1, od_w_in[j], od_w_out[j],
                             fox_qnorm[j], fox_knorm[j], fox_fbias[j], rwkv_mix[j], rwkv_w0[j],
                             rwkv_w2[j], rwkv_a0[j], rwkv_a2[j], rwkv_g2[j], rwkv_kk[j], rwkv_ka[j],
                             rwkv_rk[j], rwkv_ln_w[j], rwkv_ln_b[j])
        x2d = _peer_ffn(x2d, seq, norm_ffn[layer], sh2, sc2, g2, peer_wq[layer].astype(BF16),
                        peer_k1[layer], peer_k2[layer], _pack_table(peer_u[layer]),
                        _pack_table(peer_v[layer]))
    return _final_norm(x2d, final_norm).reshape(b, seq, d)
```
